```python
import math
import jax, jax.numpy as jnp
from jax import lax
import numpy as np

D_MODEL = 1024
BATCH = 1
SEQ = 16384
DEPTH = 2

GRID_W = 64
CTX_LEN = 256
HEAD_DIM = 64
ROPE_BASE = 10000.0
EPS = 1e-6
NEG_INF = -1e30
A_Q_HEADS = 8
A_KV_HEADS = 2
A_GROUP = A_Q_HEADS // A_KV_HEADS
A_WIDTH = A_Q_HEADS * HEAD_DIM
A_KV_WIDTH = A_KV_HEADS * HEAD_DIM
WINDOW = 128
BLOCK = 128
B_HEADS = 4
B_WIDTH = B_HEADS * 2 * HEAD_DIM
ATTN_WIDTH = A_WIDTH + B_WIDTH
ATTN_IN = A_WIDTH + 2 * A_KV_WIDTH + 3 * B_WIDTH + ATTN_WIDTH
RNN_WIDTH = 1280
RNN_BLOCKS = 16
RNN_BLOCK_DIM = RNN_WIDTH // RNN_BLOCKS
CONV_WIDTH = 4
CONV_PAD_LEFT = 2
RG_C = 8.0
N_ATTN_LAYERS = (DEPTH + 1) // 2
N_REC_LAYERS = DEPTH // 2

kernel_name = "hybrid_swa_diffattn_rglru_prefix_ctx"


def rmsnorm(x, g):
    x32 = x.astype(jnp.float32)
    y = x32 * lax.rsqrt(jnp.mean(x32 * x32, axis=-1, keepdims=True) + EPS)
    return (y * g.astype(jnp.float32)).astype(x.dtype)


def adaln(cv, w, b):
    m = jax.nn.silu(cv) @ w + b
    return jnp.split(m, 3, axis=-1)


def axial_rope_tables(n):
    rows = n // GRID_W
    row = jnp.repeat(jnp.arange(rows, dtype=jnp.float32), GRID_W)
    col = jnp.tile(jnp.arange(GRID_W, dtype=jnp.float32), rows)
    n_freq = HEAD_DIM // 4
    inv_freq = ROPE_BASE ** (-jnp.arange(n_freq, dtype=jnp.float32) / n_freq)
    ar = row[:, None] * inv_freq
    ac = col[:, None] * inv_freq
    ang = jnp.concatenate([ar, ar, ac, ac], axis=-1)
    return jnp.cos(ang), jnp.sin(ang)


def apply_rope(x, cos, sin):
    q = HEAD_DIM // 4
    x0, x1, x2, x3 = x[..., :q], x[..., q:2 * q], x[..., 2 * q:3 * q], x[..., 3 * q:]
    rot = jnp.concatenate([-x1, x0, -x3, x2], axis=-1)
    shape = (1, x.shape[1]) + (1,) * (x.ndim - 3) + (HEAD_DIM,)
    return x * cos.reshape(shape).astype(x.dtype) + rot * sin.reshape(shape).astype(x.dtype)


def split_attn(p):
    Bn, n = p.shape[:2]
    s0 = A_WIDTH
    s1 = s0 + A_KV_WIDTH
    s2 = s1 + A_KV_WIDTH
    s3 = s2 + B_WIDTH
    s4 = s3 + B_WIDTH
    s5 = s4 + B_WIDTH
    qa = p[..., :s0].reshape(Bn, n, A_KV_HEADS, A_GROUP, HEAD_DIM)
    ka = p[..., s0:s1].reshape(Bn, n, A_KV_HEADS, HEAD_DIM)
    va = p[..., s1:s2].reshape(Bn, n, A_KV_HEADS, HEAD_DIM)
    qb = p[..., s2:s3].reshape(Bn, n, B_HEADS, 2, HEAD_DIM)
    kb = p[..., s3:s4].reshape(Bn, n, B_HEADS, 2, HEAD_DIM)
    vb = p[..., s4:s5].reshape(Bn, n, B_HEADS, 2 * HEAD_DIM)
    gate = p[..., s5:]
    return qa, ka, va, qb[..., 0, :], qb[..., 1, :], kb[..., 0, :], kb[..., 1, :], vb, gate


def window_attention(q, k, v, kc, vc, sink):
    Bn, n = q.shape[:2]
    nb = n // BLOCK
    n_ctx = kc.shape[1]
    scale = HEAD_DIM ** -0.5
    qb = q.reshape(Bn, nb, BLOCK, A_KV_HEADS, A_GROUP, HEAD_DIM)

    def band(t):
        t = t.reshape(Bn, nb, BLOCK, A_KV_HEADS, HEAD_DIM)
        t = jnp.pad(t, ((0, 0), (1, 1), (0, 0), (0, 0), (0, 0)))
        return jnp.concatenate([t[:, :-2], t[:, 1:-1], t[:, 2:]], axis=2)

    kw, vw = band(k), band(v)
    s_w = jnp.einsum('bnqhgd,bnkhd->bnhgqk', qb, kw).astype(jnp.float32) * scale
    qi = jnp.arange(BLOCK)[:, None]
    kj = jnp.arange(3 * BLOCK)[None, :]
    in_band = jnp.abs(kj - BLOCK - qi) <= WINDOW
    kblk = jnp.arange(nb)[:, None, None] + kj[None] // BLOCK - 1
    mask = in_band[None] & (kblk >= 0) & (kblk < nb)
    s_w = jnp.where(mask[None, :, None, None], s_w, NEG_INF)
    s_c = jnp.einsum('bnqhgd,bchd->bnhgqc', qb, kc).astype(jnp.float32) * scale
    sk = jnp.broadcast_to(sink.astype(jnp.float32).reshape(1, 1, A_KV_HEADS, A_GROUP, 1, 1), s_w.shape[:-1] + (1,))
    p = jax.nn.softmax(jnp.concatenate([s_w, s_c, sk], axis=-1), axis=-1)
    pw = p[..., :3 * BLOCK].astype(v.dtype)
    pc = p[..., 3 * BLOCK:3 * BLOCK + n_ctx].astype(v.dtype)
    o = jnp.einsum('bnhgqk,bnkhd->bnqhgd', pw, vw) + jnp.einsum('bnhgqc,bchd->bnqhgd', pc, vc)
    return o.reshape(Bn, n, A_WIDTH)


def ctx_sink_attention(qc, kc, vc, sink):
    Bn, n = qc.shape[:2]
    scale = HEAD_DIM ** -0.5
    s = jnp.einsum('bqhgd,bkhd->bhgqk', qc, kc).astype(jnp.float32) * scale
    sk = jnp.broadcast_to(sink.astype(jnp.float32).reshape(1, A_KV_HEADS, A_GROUP, 1, 1), s.shape[:-1] + (1,))
    p = jax.nn.softmax(jnp.concatenate([s, sk], axis=-1), axis=-1)[..., :-1].astype(vc.dtype)
    o = jnp.einsum('bhgqk,bkhd->bqhgd', p, vc)
    return o.reshape(Bn, n, A_WIDTH)


def diff_attention_block(q1, q2, k1, k2, v, lam):
    scale = HEAD_DIM ** -0.5
    s1 = jnp.einsum('bqhd,bkhd->bhqk', q1, k1).astype(jnp.float32) * scale
    s2 = jnp.einsum('bqhd,bkhd->bhqk', q2, k2).astype(jnp.float32) * scale
    p = jax.nn.softmax(s1, axis=-1) - lam * jax.nn.softmax(s2, axis=-1)
    return jnp.einsum('bhqk,bkhe->bqhe', p.astype(v.dtype), v)


def diff_attention_latent(q1, q2, k1, k2, v, lam):
    Bn, n = q1.shape[:2]
    nb = n // BLOCK

    def blocks(t):
        return t.reshape(Bn, nb, BLOCK, B_HEADS, HEAD_DIM).transpose(1, 0, 2, 3, 4)

    o = lax.map(lambda qs: diff_attention_block(qs[0], qs[1], k1, k2, v, lam), (blocks(q1), blocks(q2)))
    return o.transpose(1, 0, 2, 3, 4).reshape(Bn, n, B_HEADS, 2 * HEAD_DIM)


def diff_out(o, g, lam_init):
    Bn, n = o.shape[:2]
    return (rmsnorm(o, g) * (1.0 - lam_init)).reshape(Bn, n, B_WIDTH)


def attn_mixer(hl, hc, w_in, w_out, sink, lam_q1, lam_k1, lam_q2, lam_k2, subln_g, lam_init, cos, sin, need_ctx):
    qa, ka, va, q1, q2, k1, k2, vb, gl = split_attn(hl @ w_in)
    qac, kac, vac, q1c, q2c, k1c, k2c, vbc, gc = split_attn(hc @ w_in)
    qa, ka, q1, q2, k1, k2 = [apply_rope(t, cos, sin) for t in (qa, ka, q1, q2, k1, k2)]
    lam = (jnp.exp(jnp.sum(lam_q1.astype(jnp.float32) * lam_k1.astype(jnp.float32)))
           - jnp.exp(jnp.sum(lam_q2.astype(jnp.float32) * lam_k2.astype(jnp.float32))) + lam_init)
    oa = window_attention(qa, ka, va, kac, vac, sink)
    ob = diff_attention_latent(q1, q2, jnp.concatenate([k1, k1c], axis=1), jnp.concatenate([k2, k2c], axis=1),
                               jnp.concatenate([vb, vbc], axis=1), lam)
    ob = diff_out(ob, subln_g, lam_init)
    out_l = (jnp.concatenate([oa, ob], axis=-1) * jax.nn.silu(gl)) @ w_out
    if not need_ctx:
        return out_l, None
    oac = ctx_sink_attention(qac, kac, vac, sink)
    obc = diff_out(diff_attention_block(q1c, q2c, k1c, k2c, vbc, lam), subln_g, lam_init)
    out_c = (jnp.concatenate([oac, obc], axis=-1) * jax.nn.silu(gc)) @ w_out
    return out_l, out_c


def dwconv(u, w, b):
    y = lax.conv_general_dilated(u, w[:, None, :], window_strides=(1,),
                                 padding=[(CONV_PAD_LEFT, CONV_WIDTH - 1 - CONV_PAD_LEFT)],
                                 dimension_numbers=('NWC', 'WIO', 'NWC'), feature_group_count=u.shape[-1])
    return y + b


def block_diag(u, w):
    Bn, n = u.shape[:2]
    ub = u.reshape(Bn, n, RNN_BLOCKS, RNN_BLOCK_DIM)
    return jnp.einsum('bnhi,hij->bnhj', ub, w).reshape(Bn, n, RNN_WIDTH)


def rglru_coeffs(u, wa, ba, wx, bx, lam):
    r = jax.nn.sigmoid((block_diag(u, wa) + ba).astype(jnp.float32))
    i = jax.nn.sigmoid((block_diag(u, wx) + bx).astype(jnp.float32))
    log_a = -RG_C * r * jax.nn.softplus(-lam.astype(jnp.float32))
    a = jnp.exp(log_a)
    b = jnp.sqrt(-jnp.expm1(2.0 * log_a)) * i * u.astype(jnp.float32)
    return a, b


def linear_scan(a, b, h0, reverse):
    def combine(e1, e2):
        a1, b1 = e1
        a2, b2 = e2
        return a1 * a2, a2 * b1 + b2
    A, Bc = lax.associative_scan(combine, (a, b), axis=1, reverse=reverse)
    return A * h0[:, None, :] + Bc


def rec_mixer(hl, hc, w_in, conv_w, conv_b, wa, ba, wx, bx, lam, w_out, need_ctx):
    pl = hl @ w_in
    pc = hc @ w_in
    xl, gl = pl[..., :RNN_WIDTH], pl[..., RNN_WIDTH:]
    xc, gc = pc[..., :RNN_WIDTH], pc[..., RNN_WIDTH:]
    ul = dwconv(xl, conv_w, conv_b)
    uc = dwconv(xc, conv_w, conv_b)
    h_zero = jnp.zeros((uc.shape[0], RNN_WIDTH), jnp.float32)
    yl = jnp.zeros(ul.shape, jnp.float32)
    yc = jnp.zeros(uc.shape, jnp.float32)
    for d, rev in enumerate((False, True)):
        ac, bc = rglru_coeffs(uc, wa[d], ba[d], wx[d], bx[d], lam[d])
        hc_seq = linear_scan(ac, bc, h_zero, rev)
        h0 = hc_seq[:, 0] if rev else hc_seq[:, -1]
        al, bl = rglru_coeffs(ul, wa[d], ba[d], wx[d], bx[d], lam[d])
        yl = yl + linear_scan(al, bl, h0, rev)
        yc = yc + hc_seq
    out_l = (yl.astype(hl.dtype) * jax.nn.silu(gl)) @ w_out
    if not need_ctx:
        return out_l, None
    out_c = (yc.astype(hc.dtype) * jax.nn.silu(gc)) @ w_out
    return out_l, out_c


def setup_inputs(seed: int = 0) -> dict:
    key = jax.random.key(seed)
    ks = jax.random.split(key, 26)
    nrm = jax.random.normal
    D = D_MODEL
    u = jax.random.uniform(ks[22], (N_REC_LAYERS, 2, RNN_WIDTH), minval=0.9, maxval=0.999)
    a = u ** (1.0 / RG_C)
    return {
        'x': nrm(ks[0], (BATCH, SEQ, D)),
        'c': nrm(ks[1], (BATCH, D)),
        'ctx': nrm(ks[2], (BATCH, CTX_LEN, D)),
        'c_ctx': nrm(ks[3], (D,)),
        'norm_g': 1.0 + 0.05 * nrm(ks[4], (DEPTH, D)),
        'ada_w': nrm(ks[5], (DEPTH, D, 3 * D)) * D ** -0.5,
        'ada_b': 0.02 * nrm(ks[6], (DEPTH, 3 * D)),
        'attn_w_in': nrm(ks[7], (N_ATTN_LAYERS, D, ATTN_IN)) * D ** -0.5,
        'attn_w_out': nrm(ks[8], (N_ATTN_LAYERS, ATTN_WIDTH, D)) * ATTN_WIDTH ** -0.5,
        'attn_sink': nrm(ks[9], (N_ATTN_LAYERS, A_Q_HEADS)),
        'lam_q1': 0.1 * nrm(ks[10], (N_ATTN_LAYERS, HEAD_DIM)),
        'lam_k1': 0.1 * nrm(ks[11], (N_ATTN_LAYERS, HEAD_DIM)),
        'lam_q2': 0.1 * nrm(ks[12], (N_ATTN_LAYERS, HEAD_DIM)),
        'lam_k2': 0.1 * nrm(ks[13], (N_ATTN_LAYERS, HEAD_DIM)),
        'subln_g': 1.0 + 0.05 * nrm(ks[14], (N_ATTN_LAYERS, 2 * HEAD_DIM)),
        'rec_w_in': nrm(ks[15], (N_REC_LAYERS, D, 2 * RNN_WIDTH)) * D ** -0.5,
        'rec_conv_w': nrm(ks[16], (N_REC_LAYERS, CONV_WIDTH, RNN_WIDTH)) * CONV_WIDTH ** -0.5,
        'rec_conv_b': 0.02 * nrm(ks[17], (N_REC_LAYERS, RNN_WIDTH)),
        'rec_wa': nrm(ks[18], (N_REC_LAYERS, 2, RNN_BLOCKS, RNN_BLOCK_DIM, RNN_BLOCK_DIM)) * RNN_BLOCK_DIM ** -0.5,
        'rec_ba': 0.02 * nrm(ks[19], (N_REC_LAYERS, 2, RNN_WIDTH)),
        'rec_wx': nrm(ks[20], (N_REC_LAYERS, 2, RNN_BLOCKS, RNN_BLOCK_DIM, RNN_BLOCK_DIM)) * RNN_BLOCK_DIM ** -0.5,
        'rec_bx': 0.02 * nrm(ks[21], (N_REC_LAYERS, 2, RNN_WIDTH)),
        'rec_lam': jnp.log(a) - jnp.log1p(-a),
        'rec_w_out': nrm(ks[23], (N_REC_LAYERS, RNN_WIDTH, D)) * RNN_WIDTH ** -0.5,
        'final_g': 1.0 + 0.05 * nrm(ks[24], (D,)),
    }


def reference(x, c, ctx, c_ctx, norm_g, ada_w, ada_b, attn_w_in, attn_w_out, attn_sink, lam_q1, lam_k1,
              lam_q2, lam_k2, subln_g, rec_w_in, rec_conv_w, rec_conv_b, rec_wa, rec_ba, rec_wx, rec_bx,
              rec_lam, rec_w_out, final_g):
    n = x.shape[1]
    cos, sin = axial_rope_tables(n)
    xl, xc = x, ctx
    for l in range(DEPTH):
        need_ctx = l < DEPTH - 1
        sh, sc, gt = adaln(c, ada_w[l], ada_b[l])
        shc, scc, gtc = adaln(c_ctx, ada_w[l], ada_b[l])
        hl = rmsnorm(xl, norm_g[l]) * (1.0 + sc[:, None, :]) + sh[:, None, :]
        hc = rmsnorm(xc, norm_g[l]) * (1.0 + scc) + shc
        j = l // 2
        if l % 2 == 0:
            lam_init = 0.8 - 0.6 * math.exp(-0.3 * l)
            out_l, out_c = attn_mixer(hl, hc, attn_w_in[j], attn_w_out[j], attn_sink[j], lam_q1[j], lam_k1[j],
                                      lam_q2[j], lam_k2[j], subln_g[j], lam_init, cos, sin, need_ctx)
        else:
            out_l, out_c = rec_mixer(hl, hc, rec_w_in[j], rec_conv_w[j], rec_conv_b[j], rec_wa[j], rec_ba[j],
                                     rec_wx[j], rec_bx[j], rec_lam[j], rec_w_out[j], need_ctx)
        xl = xl + gt[:, None, :] * out_l
        if need_ctx:
            xc = xc + gtc * out_c
    return rmsnorm(xl, final_g)
```

```python
import functools
import math

import jax
import jax.numpy as jnp
from jax import lax
from jax.experimental import pallas as pl
from jax.experimental.pallas import tpu as pltpu

D_MODEL = 1024
GRID_W = 64
HEAD_DIM = 64
ROPE_BASE = 10000.0
EPS = 1e-6
NEG_INF = -1e30
A_Q_HEADS = 8
A_KV_HEADS = 2
A_GROUP = A_Q_HEADS // A_KV_HEADS
A_WIDTH = A_Q_HEADS * HEAD_DIM
A_KV_WIDTH = A_KV_HEADS * HEAD_DIM
WINDOW = 128
BLOCK = 128
B_HEADS = 4
B_WIDTH = B_HEADS * 2 * HEAD_DIM
ATTN_WIDTH = A_WIDTH + B_WIDTH
RNN_WIDTH = 1280
RNN_BLOCKS = 16
RNN_BLOCK_DIM = RNN_WIDTH // RNN_BLOCKS
CONV_WIDTH = 4
RG_C = 8.0
Q_SCALE = HEAD_DIM ** -0.5

LANES = 128
SUBLANES = 8
VMEM_LIMIT = 48 * 1024 * 1024

ROW_TILE = 512
SCAN_TILE = 256
DIFF_TQ = 256
DIFF_TK = 512

_NT = (((1,), (1,)), ((), ()))


def _silu(x):
    return x * jax.nn.sigmoid(x)


def _norm_mod(x, g, sc, sh):
    ms = jnp.mean(x * x, axis=-1, keepdims=True)
    return (x * lax.rsqrt(ms + EPS) * g) * (1.0 + sc) + sh


def _params(*sem):
    return pltpu.CompilerParams(dimension_semantics=sem, vmem_limit_bytes=VMEM_LIMIT)


def _adaln_kernel(cv_ref, w_ref, b_ref, o_ref):
    s = _silu(cv_ref[...])
    o_ref[0] = jnp.dot(s, w_ref[0], preferred_element_type=jnp.float32) + b_ref[0]


def _adaln(cv, ada_w, ada_b):
    depth, d, n3 = ada_w.shape
    tn = 1024
    return pl.pallas_call(
        _adaln_kernel,
        out_shape=jax.ShapeDtypeStruct((depth, SUBLANES, n3), jnp.float32),
        grid=(depth, n3 // tn),
        in_specs=[
            pl.BlockSpec((SUBLANES, d), lambda l, j: (0, 0)),
            pl.BlockSpec((1, d, tn), lambda l, j: (l, 0, j)),
            pl.BlockSpec((1, 1, tn), lambda l, j: (l, 0, j)),
        ],
        out_specs=pl.BlockSpec((1, SUBLANES, tn), lambda l, j: (l, 0, j)),
        compiler_params=_params("parallel", "parallel"),
        name="adaln",
    )(cv, ada_w, ada_b.reshape(depth, 1, n3))


def _rope(x, cos, sin_signed, first_half):
    rot = jnp.where(first_half, pltpu.roll(x, LANES - 16, 1), pltpu.roll(x, 16, 1))
    return x * cos + rot * sin_signed


def _attn_in_kernel(x_ref, g_ref, sc_ref, sh_ref, w_ref, wvt_ref, cos_ref, sin_ref,
                    qa_ref, ka_ref, va_ref, qb_ref, kb_ref, vbt_ref, gate_ref, *, rope):
    h = _norm_mod(x_ref[...], g_ref[...], sc_ref[...], sh_ref[...]).astype(jnp.bfloat16)
    p = jnp.dot(h, w_ref[...], preferred_element_type=jnp.float32)
    vbt_ref[...] = lax.dot_general(wvt_ref[...], h, _NT,
                                   preferred_element_type=jnp.float32).astype(jnp.bfloat16)
    if rope:
        cos = cos_ref[...]
        sin = sin_ref[...]
        lane = lax.broadcasted_iota(jnp.int32, cos.shape, 1)
        first_half = (lane % 32) < 16

    def chunk(c):
        x = p[:, c * LANES:(c + 1) * LANES]
        return _rope(x, cos, sin, first_half) if rope else x

    o0 = A_WIDTH // LANES
    o1 = o0 + A_KV_WIDTH // LANES
    o2 = o1 + A_KV_WIDTH // LANES
    o3 = o2 + B_WIDTH // LANES
    o4 = o3 + B_WIDTH // LANES
    for c in range(o0):
        qa_ref[:, c * LANES:(c + 1) * LANES] = (chunk(c) * Q_SCALE).astype(jnp.bfloat16)
    for c in range(o0, o1):
        ka_ref[:, (c - o0) * LANES:(c - o0 + 1) * LANES] = chunk(c).astype(jnp.bfloat16)
    va_ref[...] = p[:, o1 * LANES:o2 * LANES].astype(jnp.bfloat16)
    for c in range(o2, o3):
        qb_ref[:, (c - o2) * LANES:(c - o2 + 1) * LANES] = (chunk(c) * Q_SCALE).astype(jnp.bfloat16)
    for c in range(o3, o4):
        kb_ref[:, (c - o3) * LANES:(c - o3 + 1) * LANES] = chunk(c).astype(jnp.bfloat16)
    gate_ref[...] = p[:, o4 * LANES:]


def _attn_in_proj(x, g, sc, sh, w, wvt, cos, sin, *, rope):
    n, d = x.shape
    tm = min(ROW_TILE, n)
    nw = w.shape[1]
    row = lambda i: (i, 0)
    fixed = lambda i: (0, 0)
    bf = jnp.bfloat16
    return pl.pallas_call(
        functools.partial(_attn_in_kernel, rope=rope),
        out_shape=[
            jax.ShapeDtypeStruct((n, A_WIDTH), bf),
            jax.ShapeDtypeStruct((n, A_KV_WIDTH), bf),
            jax.ShapeDtypeStruct((n, A_KV_WIDTH), bf),
            jax.ShapeDtypeStruct((n, B_WIDTH), bf),
            jax.ShapeDtypeStruct((n, B_WIDTH), bf),
            jax.ShapeDtypeStruct((B_WIDTH, n), bf),
            jax.ShapeDtypeStruct((n, ATTN_WIDTH), jnp.float32),
        ],
        grid=(n // tm,),
        in_specs=[
            pl.BlockSpec((tm, d), row),
            pl.BlockSpec((1, d), fixed),
            pl.BlockSpec((1, d), fixed),
            pl.BlockSpec((1, d), fixed),
            pl.BlockSpec((d, nw), fixed),
            pl.BlockSpec((B_WIDTH, d), fixed),
            pl.BlockSpec((tm, LANES), row),
            pl.BlockSpec((tm, LANES), row),
        ],
        out_specs=[
            pl.BlockSpec((tm, A_WIDTH), row),
            pl.BlockSpec((tm, A_KV_WIDTH), row),
            pl.BlockSpec((tm, A_KV_WIDTH), row),
            pl.BlockSpec((tm, B_WIDTH), row),
            pl.BlockSpec((tm, B_WIDTH), row),
            pl.BlockSpec((B_WIDTH, tm), lambda i: (0, i)),
            pl.BlockSpec((tm, ATTN_WIDTH), row),
        ],
        compiler_params=_params("parallel"),
        name="attn_in_proj",
    )(x, g, sc, sh, w, wvt, cos, sin)


def _gqa_kernel(*refs, has_window, nb):
    if has_window:
        (sink_ref, q_ref, kp_ref, kc_ref, kn_ref, vp_ref, vc_ref, vn_ref,
         kx_ref, vx_ref, o_ref) = refs
    else:
        sink_ref, q_ref, kx_ref, vx_ref, o_ref = refs
    i = pl.program_id(0)
    q = q_ref[...]
    kx = kx_ref[...]
    vx = vx_ref[...]
    tq = q.shape[0]
    if has_window:
        kw = jnp.concatenate([kp_ref[...], kc_ref[...], kn_ref[...]], axis=0)
        vw = jnp.concatenate([vp_ref[...], vc_ref[...], vn_ref[...]], axis=0)
        qi = lax.broadcasted_iota(jnp.int32, (tq, 3 * BLOCK), 0)
        kj = lax.broadcasted_iota(jnp.int32, (tq, 3 * BLOCK), 1)
        d = kj - BLOCK - qi
        valid = (d >= -WINDOW) & (d <= WINDOW)
        valid = valid & ((kj >= BLOCK) | (i > 0)) & ((kj < 2 * BLOCK) | (i < nb - 1))
    outs = []
    for g in range(A_KV_HEADS):
        lo, hi = g * HEAD_DIM, (g + 1) * HEAD_DIM
        kxg, vxg = kx[:, lo:hi], vx[:, lo:hi]
        if has_window:
            kwg, vwg = kw[:, lo:hi], vw[:, lo:hi]
        for j in range(A_GROUP):
            hq = g * A_GROUP + j
            qh = q[:, hq * HEAD_DIM:(hq + 1) * HEAD_DIM]
            sk = sink_ref[hq]
            s_c = lax.dot_general(qh, kxg, _NT, preferred_element_type=jnp.float32)
            m = jnp.maximum(jnp.max(s_c, axis=-1, keepdims=True), sk)
            if has_window:
                s_w = lax.dot_general(qh, kwg, _NT, preferred_element_type=jnp.float32)
                s_w = jnp.where(valid, s_w, NEG_INF)
                m = jnp.maximum(m, jnp.max(s_w, axis=-1, keepdims=True))
            p_c = jnp.exp(s_c - m)
            l = jnp.sum(p_c, axis=-1, keepdims=True) + jnp.exp(sk - m)
            o = jnp.dot(p_c.astype(jnp.bfloat16), vxg, preferred_element_type=jnp.float32)
            if has_window:
                p_w = jnp.exp(s_w - m)
                l = l + jnp.sum(p_w, axis=-1, keepdims=True)
                o = o + jnp.dot(p_w.astype(jnp.bfloat16), vwg, preferred_element_type=jnp.float32)
            outs.append(o / l)
    o_ref[...] = jnp.concatenate(outs, axis=-1)


def _gqa_attention(sink, q, k, v, kx, vx, *, has_window):
    n = q.shape[0]
    nx = kx.shape[0]
    smem = pl.BlockSpec(memory_space=pltpu.SMEM)
    fixed = lambda i: (0, 0)
    if has_window:
        nb = n // BLOCK
        tq = BLOCK
        prev = lambda i: (jnp.maximum(i - 1, 0), 0)
        cur = lambda i: (i, 0)
        nxt = lambda i: (jnp.minimum(i + 1, nb - 1), 0)
        kv_specs = [pl.BlockSpec((BLOCK, A_KV_WIDTH), f) for f in (prev, cur, nxt)]
        in_specs = ([smem, pl.BlockSpec((tq, A_WIDTH), cur)] + kv_specs + kv_specs +
                    [pl.BlockSpec((nx, A_KV_WIDTH), fixed)] * 2)
        args = (sink, q, k, k, k, v, v, v, kx, vx)
    else:
        nb = 1
        tq = n
        in_specs = [smem, pl.BlockSpec((tq, A_WIDTH), fixed)] + [pl.BlockSpec((nx, A_KV_WIDTH), fixed)] * 2
        args = (sink, q, kx, vx)
    return pl.pallas_call(
        functools.partial(_gqa_kernel, has_window=has_window, nb=nb),
        out_shape=jax.ShapeDtypeStruct((n, A_WIDTH), jnp.float32),
        grid=(nb,),
        in_specs=in_specs,
        out_specs=pl.BlockSpec((tq, A_WIDTH), lambda i: (i, 0)),
        compiler_params=_params("parallel"),
        name="gqa_window" if has_window else "gqa_ctx",
    )(*args)


def _diff_kernel(*refs, has_latent, lam_init, n_chunks):
    if has_latent:
        lam_ref, g_ref, q_ref, k_ref, vt_ref, kx_ref, vxt_ref, o_ref = refs
    else:
        lam_ref, g_ref, q_ref, kx_ref, vxt_ref, o_ref = refs
    q = q_ref[...]
    tq = q.shape[0]
    lane = lax.broadcasted_iota(jnp.int32, q.shape, 1)
    zero = jnp.zeros_like(q)
    qz = (jnp.where(lane < HEAD_DIM, q, zero), jnp.where(lane >= HEAD_DIM, q, zero))

    def step(k, vt, carry):
        new = []
        for mp in range(2):
            m, l, acc = carry[mp]
            s = lax.dot_general(k, qz[mp], _NT, preferred_element_type=jnp.float32)
            m_new = jnp.maximum(m, jnp.max(s, axis=0, keepdims=True))
            alpha = jnp.exp(m - m_new)
            p = jnp.exp(s - m_new)
            l = alpha * l + jnp.sum(p, axis=0, keepdims=True)
            acc = alpha * acc + jnp.dot(vt, p.astype(jnp.bfloat16),
                                        preferred_element_type=jnp.float32)
            new.append((m_new, l, acc))
        return tuple(new)

    init_one = (jnp.full((1, tq), NEG_INF, jnp.float32), jnp.zeros((1, tq), jnp.float32),
                jnp.zeros((2 * HEAD_DIM, tq), jnp.float32))
    carry = step(kx_ref[...], vxt_ref[...], (init_one, init_one))
    if has_latent:
        def body(c, carry):
            off = pl.multiple_of(c * DIFF_TK, DIFF_TK)
            return step(k_ref[pl.ds(off, DIFF_TK), :], vt_ref[:, pl.ds(off, DIFF_TK)], carry)
        carry = lax.fori_loop(0, n_chunks, body, carry)

    lv = lam_ref[...]
    lam = (jnp.exp(jnp.sum(lv[0:1] * lv[1:2], keepdims=True))
           - jnp.exp(jnp.sum(lv[2:3] * lv[3:4], keepdims=True)) + lam_init)
    (_, l1, a1), (_, l2, a2) = carry
    o = a1 / l1 - lam * (a2 / l2)
    ms = jnp.mean(o * o, axis=0, keepdims=True)
    o = (o * lax.rsqrt(ms + EPS)) * g_ref[...] * (1.0 - lam_init)
    o_ref[...] = o.T


def _diff_attention(lamv, subln_col, q, k, vt, kx, vxt, *, has_latent, lam_init):
    n = q.shape[0]
    nx = kx.shape[0]
    tq = min(DIFF_TQ, n)
    hw = 2 * HEAD_DIM
    fixed = lambda h, i: (0, 0)
    in_specs = [pl.BlockSpec((4, HEAD_DIM), fixed), pl.BlockSpec((hw, 1), fixed),
                pl.BlockSpec((tq, hw), lambda h, i: (i, h))]
    args = [lamv, subln_col, q]
    n_chunks = 0
    if has_latent:
        nk = k.shape[0]
        n_chunks = nk // DIFF_TK
        in_specs += [pl.BlockSpec((nk, hw), lambda h, i: (0, h)),
                     pl.BlockSpec((hw, nk), lambda h, i: (h, 0))]
        args += [k, vt]
    in_specs += [pl.BlockSpec((nx, hw), lambda h, i: (0, h)),
                 pl.BlockSpec((hw, nx), lambda h, i: (h, 0))]
    args += [kx, vxt]
    return pl.pallas_call(
        functools.partial(_diff_kernel, has_latent=has_latent, lam_init=lam_init,
                          n_chunks=n_chunks),
        out_shape=jax.ShapeDtypeStruct((n, B_WIDTH), jnp.float32),
        grid=(B_HEADS, n // tq),
        in_specs=in_specs,
        out_specs=pl.BlockSpec((tq, hw), lambda h, i: (i, h)),
        compiler_params=_params("parallel", "parallel"),
        name="diff_attn" if has_latent else "diff_attn_ctx",
    )(*args)


def _attn_out_kernel(oa_ref, ob_ref, gate_ref, x_ref, gt_ref, w_ref, o_ref):
    a = jnp.concatenate([oa_ref[...], ob_ref[...]], axis=-1) * _silu(gate_ref[...])
    y = jnp.dot(a.astype(jnp.bfloat16), w_ref[...], preferred_element_type=jnp.float32)
    o_ref[...] = x_ref[...] + gt_ref[...] * y


def _attn_out_proj(oa, ob, gate, x, gt, w):
    n, d = x.shape
    tm = min(ROW_TILE, n)
    row = lambda i: (i, 0)
    fixed = lambda i: (0, 0)
    return pl.pallas_call(
        _attn_out_kernel,
        out_shape=jax.ShapeDtypeStruct((n, d), jnp.float32),
        grid=(n // tm,),
        in_specs=[
            pl.BlockSpec((tm, A_WIDTH), row),
            pl.BlockSpec((tm, B_WIDTH), row),
            pl.BlockSpec((tm, ATTN_WIDTH), row),
            pl.BlockSpec((tm, d), row),
            pl.BlockSpec((1, d), fixed),
            pl.BlockSpec((ATTN_WIDTH, d), fixed),
        ],
        out_specs=pl.BlockSpec((tm, d), row),
        compiler_params=_params("parallel"),
        name="attn_out_proj",
    )(oa, ob, gate, x, gt, w)


def _rec_in_kernel(x_ref, g_ref, sc_ref, sh_ref, w_ref, xr_ref, gr_ref):
    h = _norm_mod(x_ref[...], g_ref[...], sc_ref[...], sh_ref[...]).astype(jnp.bfloat16)
    p = jnp.dot(h, w_ref[...], preferred_element_type=jnp.float32)
    xr_ref[...] = p[:, :RNN_WIDTH]
    gr_ref[...] = p[:, RNN_WIDTH:]


def _rec_in_proj(x, g, sc, sh, w):
    n, d = x.shape
    tm = min(ROW_TILE, n)
    row = lambda i: (i, 0)
    fixed = lambda i: (0, 0)
    return pl.pallas_call(
        _rec_in_kernel,
        out_shape=[jax.ShapeDtypeStruct((n, RNN_WIDTH), jnp.float32)] * 2,
        grid=(n // tm,),
        in_specs=[
            pl.BlockSpec((tm, d), row),
            pl.BlockSpec((1, d), fixed),
            pl.BlockSpec((1, d), fixed),
            pl.BlockSpec((1, d), fixed),
            pl.BlockSpec((d, 2 * RNN_WIDTH), fixed),
        ],
        out_specs=[pl.BlockSpec((tm, RNN_WIDTH), row)] * 2,
        compiler_params=_params("parallel"),
        name="rec_in_proj",
    )(x, g, sc, sh, w)


def _rec_scan_kernel(*refs, reverse, fuse_out, nt):
    if fuse_out:
        (x_ref, xp_ref, xn_ref, cw_ref, cb_ref, w_ref, bias_ref, lam_ref, h0_ref,
         yr_ref, gr_ref, xl_ref, gt_ref, wo_ref, fg_ref,
         o_ref, xe_ref, a_ref, b_ref, y_ref, h_ref) = refs
    else:
        (x_ref, xp_ref, xn_ref, cw_ref, cb_ref, w_ref, bias_ref, lam_ref, h0_ref,
         y_ref, hf_ref, xe_ref, a_ref, b_ref, h_ref) = refs
    i = pl.program_id(0)
    t = i if not reverse else nt - 1 - i
    tm = x_ref.shape[0]

    @pl.when(i == 0)
    def _():
        h_ref[...] = h0_ref[...]

    xe_ref[0:SUBLANES, :] = jnp.where(t > 0, xp_ref[...], 0.0)
    xe_ref[SUBLANES:SUBLANES + tm, :] = x_ref[...]
    xe_ref[SUBLANES + tm:, :] = jnp.where(t < nt - 1, xn_ref[...], 0.0)
    cw = cw_ref[...]
    u = cb_ref[...] + sum(cw[k:k + 1] * xe_ref[SUBLANES - 2 + k:SUBLANES - 2 + k + tm, :]
                          for k in range(CONV_WIDTH))
    z = jnp.dot(u.astype(jnp.bfloat16), w_ref[...], preferred_element_type=jnp.float32) + bias_ref[...]
    r = jax.nn.sigmoid(z[:, :RNN_WIDTH])
    ig = jax.nn.sigmoid(z[:, RNN_WIDTH:])
    nl = -lam_ref[...]
    softplus = jnp.maximum(nl, 0.0) + jnp.log1p(jnp.exp(-jnp.abs(nl)))
    log_a = -RG_C * r * softplus
    a = jnp.exp(log_a)
    a_ref[...] = a
    b_ref[...] = jnp.sqrt(1.0 - a * a) * ig * u

    def row(s, h):
        rr = (tm - 1 - s) if reverse else s
        h = a_ref[pl.ds(rr, 1), :] * h + b_ref[pl.ds(rr, 1), :]
        y_ref[pl.ds(rr, 1), :] = h
        return h

    h = lax.fori_loop(0, tm, row, h_ref[...], unroll=8)
    h_ref[...] = h

    if fuse_out:
        y = (y_ref[...] + yr_ref[...]) * _silu(gr_ref[...])
        out = jnp.dot(y.astype(jnp.bfloat16), wo_ref[...], preferred_element_type=jnp.float32)
        xl = xl_ref[...] + gt_ref[...] * out
        ms = jnp.mean(xl * xl, axis=-1, keepdims=True)
        o_ref[...] = xl * lax.rsqrt(ms + EPS) * fg_ref[...]
    else:
        hf_ref[...] = h


def _rec_scan(xr, cw, cb, w, bias, lam, h0, *, reverse, fused=None):
    n = xr.shape[0]
    tm = min(SCAN_TILE, n)
    nt = n // tm
    per = tm // SUBLANES
    nblk8 = n // SUBLANES
    pos = (lambda i: nt - 1 - i) if reverse else (lambda i: i)
    row = lambda i: (pos(i), 0)
    prev = lambda i: (jnp.maximum(pos(i) * per - 1, 0), 0)
    nxt = lambda i: (jnp.minimum((pos(i) + 1) * per, nblk8 - 1), 0)
    fixed = lambda i: (0, 0)
    vec = pl.BlockSpec((1, RNN_WIDTH), fixed)
    in_specs = [
        pl.BlockSpec((tm, RNN_WIDTH), row),
        pl.BlockSpec((SUBLANES, RNN_WIDTH), prev),
        pl.BlockSpec((SUBLANES, RNN_WIDTH), nxt),
        pl.BlockSpec((CONV_WIDTH, RNN_WIDTH), fixed),
        vec,
        pl.BlockSpec((RNN_WIDTH, 2 * RNN_WIDTH), fixed),
        pl.BlockSpec((1, 2 * RNN_WIDTH), fixed),
        vec,
        vec,
    ]
    args = [xr, xr, xr, cw, cb, w, bias, lam, h0]
    scratch = [pltpu.VMEM((tm + 2 * SUBLANES, RNN_WIDTH), jnp.float32),
               pltpu.VMEM((tm, RNN_WIDTH), jnp.float32),
               pltpu.VMEM((tm, RNN_WIDTH), jnp.float32)]
    if fused is not None:
        yr, gr, xl, gt, wo, fg = fused
        d = xl.shape[1]
        in_specs += [
            pl.BlockSpec((tm, RNN_WIDTH), row),
            pl.BlockSpec((tm, RNN_WIDTH), row),
            pl.BlockSpec((tm, d), row),
            pl.BlockSpec((1, d), fixed),
            pl.BlockSpec((RNN_WIDTH, d), fixed),
            pl.BlockSpec((1, d), fixed),
        ]
        args += [yr, gr, xl, gt, wo, fg]
        out_shape = jax.ShapeDtypeStruct((n, d), jnp.float32)
        out_specs = pl.BlockSpec((tm, d), row)
        scratch = scratch + [pltpu.VMEM((tm, RNN_WIDTH), jnp.float32)]
    else:
        out_shape = [jax.ShapeDtypeStruct((n, RNN_WIDTH), jnp.float32),
                     jax.ShapeDtypeStruct((1, RNN_WIDTH), jnp.float32)]
        out_specs = [pl.BlockSpec((tm, RNN_WIDTH), row), vec]
    scratch = scratch + [pltpu.VMEM((1, RNN_WIDTH), jnp.float32)]
    return pl.pallas_call(
        functools.partial(_rec_scan_kernel, reverse=reverse, fuse_out=fused is not None, nt=nt),
        out_shape=out_shape,
        grid=(nt,),
        in_specs=in_specs,
        out_specs=out_specs,
        scratch_shapes=scratch,
        compiler_params=_params("arbitrary"),
        name="rec_scan_out" if fused is not None else ("rec_scan_rev" if reverse else "rec_scan_fwd"),
    )(*args)


def _rope_tables(n):
    pos = jnp.arange(n, dtype=jnp.int32)
    row = (pos // GRID_W).astype(jnp.float32)
    col = (pos % GRID_W).astype(jnp.float32)
    n_freq = HEAD_DIM // 4
    inv_freq = ROPE_BASE ** (-jnp.arange(n_freq, dtype=jnp.float32) / n_freq)
    ar = row[:, None] * inv_freq
    ac = col[:, None] * inv_freq
    ang = jnp.concatenate([ar, ar, ac, ac], axis=-1)
    sign = jnp.where((jnp.arange(HEAD_DIM) % 32) < 16, -1.0, 1.0)
    cos = jnp.cos(ang)
    sin = jnp.sin(ang) * sign
    return jnp.tile(cos, (1, 2)), jnp.tile(sin, (1, 2))


def _block_diag(w):
    nb, bi, bj = w.shape
    eye = jnp.eye(nb, dtype=w.dtype)
    return jnp.einsum('hij,hg->higj', w, eye).reshape(nb * bi, nb * bj)


def kernel(x, c, ctx, c_ctx, norm_g, ada_w, ada_b, attn_w_in, attn_w_out, attn_sink, lam_q1, lam_k1,
           lam_q2, lam_k2, subln_g, rec_w_in, rec_conv_w, rec_conv_b, rec_wa, rec_ba, rec_wx, rec_bx,
           rec_lam, rec_w_out, final_g):
    assert x.shape[0] == 1 and c.shape[0] == 1 and ctx.shape[0] == 1
    bf = jnp.bfloat16
    d = D_MODEL
    xl = x[0]
    xc = ctx[0]
    n = xl.shape[0]

    cv = jnp.zeros((SUBLANES, d), jnp.float32).at[0].set(c[0]).at[1].set(c_ctx)
    mods = _adaln(cv, ada_w, ada_b)

    def mod(l, r):
        m = mods[l, r]
        return m[None, :d], m[None, d:2 * d], m[None, 2 * d:]

    lam_init = 0.8 - 0.6 * math.exp(-0.3 * 0)
    s5 = A_WIDTH + 2 * A_KV_WIDTH + 2 * B_WIDTH
    s6 = s5 + B_WIDTH
    w_in = attn_w_in[0]
    w_main = jnp.concatenate([w_in[:, :s5], w_in[:, s6:]], axis=1).astype(bf)
    w_vt = w_in[:, s5:s6].T.astype(bf)
    w_out = attn_w_out[0].astype(bf)
    g0 = norm_g[0][None]
    cos, sin = _rope_tables(n)
    sh, sc, gt = mod(0, 0)
    shc, scc, gtc = mod(0, 1)
    qa, ka, va, qb, kb, vbt, gate = _attn_in_proj(xl, g0, sc, sh, w_main, w_vt, cos, sin, rope=True)
    nc = xc.shape[0]
    qac, kac, vac, qbc, kbc, vbtc, gatec = _attn_in_proj(
        xc, g0, scc, shc, w_main, w_vt, cos[:nc], sin[:nc], rope=False)

    sink = attn_sink[0]
    lamv = jnp.stack([lam_q1[0], lam_k1[0], lam_q2[0], lam_k2[0]])
    subln_col = subln_g[0][:, None]
    oa = _gqa_attention(sink, qa, ka, va, kac, vac, has_window=True)
    ob = _diff_attention(lamv, subln_col, qb, kb, vbt, kbc, vbtc, has_latent=True, lam_init=lam_init)
    oac = _gqa_attention(sink, qac, None, None, kac, vac, has_window=False)
    obc = _diff_attention(lamv, subln_col, qbc, None, None, kbc, vbtc, has_latent=False,
                          lam_init=lam_init)
    xl = _attn_out_proj(oa, ob, gate, xl, gt, w_out)
    xc = _attn_out_proj(oac, obc, gatec, xc, gtc, w_out)

    g1 = norm_g[1][None]
    sh, sc, gt = mod(1, 0)
    shc, scc, _ = mod(1, 1)
    w_rin = rec_w_in[0].astype(bf)
    xr, gr = _rec_in_proj(xl, g1, sc, sh, w_rin)
    xrc, _ = _rec_in_proj(xc, g1, scc, shc, w_rin)
    cw = rec_conv_w[0]
    cb = rec_conv_b[0][None]
    zero_h = jnp.zeros((1, RNN_WIDTH), jnp.float32)

    def coeff(dr):
        w = jnp.concatenate([_block_diag(rec_wa[0, dr]), _block_diag(rec_wx[0, dr])], axis=1).astype(bf)
        bias = jnp.concatenate([rec_ba[0, dr], rec_bx[0, dr]])[None]
        return w, bias, rec_lam[0, dr][None]

    wf, bias_f, lam_f = coeff(0)
    wr, bias_r, lam_r = coeff(1)
    _, h0f = _rec_scan(xrc, cw, cb, wf, bias_f, lam_f, zero_h, reverse=False)
    _, h0r = _rec_scan(xrc, cw, cb, wr, bias_r, lam_r, zero_h, reverse=True)
    yr, _ = _rec_scan(xr, cw, cb, wr, bias_r, lam_r, h0r, reverse=True)
    out = _rec_scan(xr, cw, cb, wf, bias_f, lam_f, h0f, reverse=False,
                    fused=(yr, gr, xl, gt, rec_w_out[0].astype(bf), final_g[None]))
    return out[None]
```

```python
import functools
import math

import jax
import jax.numpy as jnp
from jax import lax
from jax.experimental import pallas as pl
from jax.experimental.pallas import tpu as pltpu

D_MODEL = 1024
GRID_W = 64
HEAD_DIM = 64
ROPE_BASE = 10000.0
EPS = 1e-6
NEG_INF = -1e30
A_Q_HEADS = 8
A_KV_HEADS = 2
A_GROUP = A_Q_HEADS // A_KV_HEADS
A_WIDTH = A_Q_HEADS * HEAD_DIM
A_KV_WIDTH = A_KV_HEADS * HEAD_DIM
WINDOW = 128
BLOCK = 128
B_HEADS = 4
B_WIDTH = B_HEADS * 2 * HEAD_DIM
ATTN_WIDTH = A_WIDTH + B_WIDTH
RNN_WIDTH = 1280
RNN_BLOCKS = 16
RNN_BLOCK_DIM = RNN_WIDTH // RNN_BLOCKS
CONV_WIDTH = 4
RG_C = 8.0
Q_SCALE = HEAD_DIM ** -0.5
QB_SCALE = Q_SCALE * math.log2(math.e)

LANES = 128
SUBLANES = 8
VMEM_LIMIT = 48 * 1024 * 1024

ROW_TILE = 512
SCAN_TILE = 256
DIFF_TQ = 256
DIFF_TK = 512
DIFF_STAGES = 8
DIFF_SUM_ROWS = 16

_NT = (((1,), (1,)), ((), ()))


def _silu(x):
    return x * jax.nn.sigmoid(x)


def _norm_mod(x, g, sc, sh):
    ms = jnp.mean(x * x, axis=-1, keepdims=True)
    return (x * lax.rsqrt(ms + EPS) * g) * (1.0 + sc) + sh


def _params(*sem, flags=None):
    return pltpu.CompilerParams(dimension_semantics=sem, vmem_limit_bytes=VMEM_LIMIT, flags=flags)


def _adaln_kernel(cv_ref, w_ref, b_ref, o_ref):
    s = _silu(cv_ref[...])
    o_ref[0] = jnp.dot(s, w_ref[0], preferred_element_type=jnp.float32) + b_ref[0]


def _adaln(cv, ada_w, ada_b):
    depth, d, n3 = ada_w.shape
    tn = 1024
    return pl.pallas_call(
        _adaln_kernel,
        out_shape=jax.ShapeDtypeStruct((depth, SUBLANES, n3), jnp.float32),
        grid=(depth, n3 // tn),
        in_specs=[
            pl.BlockSpec((SUBLANES, d), lambda l, j: (0, 0)),
            pl.BlockSpec((1, d, tn), lambda l, j: (l, 0, j)),
            pl.BlockSpec((1, 1, tn), lambda l, j: (l, 0, j)),
        ],
        out_specs=pl.BlockSpec((1, SUBLANES, tn), lambda l, j: (l, 0, j)),
        compiler_params=_params("parallel", "parallel"),
        name="adaln",
    )(cv, ada_w, ada_b.reshape(depth, 1, n3))


def _rope(x, cos, sin_signed, first_half):
    rot = jnp.where(first_half, pltpu.roll(x, LANES - 16, 1), pltpu.roll(x, 16, 1))
    return x * cos + rot * sin_signed


def _attn_in_kernel(x_ref, g_ref, sc_ref, sh_ref, w_ref, wvt_ref, cos_ref, sin_ref,
                    qa_ref, ka_ref, va_ref, qb_ref, kb_ref, vbt_ref, gate_ref, *, rope):
    h = _norm_mod(x_ref[...], g_ref[...], sc_ref[...], sh_ref[...]).astype(jnp.bfloat16)
    p = jnp.dot(h, w_ref[...], preferred_element_type=jnp.float32)
    vbt_ref[...] = lax.dot_general(wvt_ref[...], h, _NT,
                                   preferred_element_type=jnp.float32).astype(jnp.bfloat16)
    if rope:
        cos = cos_ref[...]
        sin = sin_ref[...]
        lane = lax.broadcasted_iota(jnp.int32, cos.shape, 1)
        first_half = (lane % 32) < 16

    def chunk(c):
        x = p[:, c * LANES:(c + 1) * LANES]
        return _rope(x, cos, sin, first_half) if rope else x

    o0 = A_WIDTH // LANES
    o1 = o0 + A_KV_WIDTH // LANES
    o2 = o1 + A_KV_WIDTH // LANES
    o3 = o2 + B_WIDTH // LANES
    o4 = o3 + B_WIDTH // LANES
    for c in range(o0):
        qa_ref[:, c * LANES:(c + 1) * LANES] = (chunk(c) * Q_SCALE).astype(jnp.bfloat16)
    for c in range(o0, o1):
        ka_ref[:, (c - o0) * LANES:(c - o0 + 1) * LANES] = chunk(c).astype(jnp.bfloat16)
    va_ref[...] = p[:, o1 * LANES:o2 * LANES].astype(jnp.bfloat16)
    for c in range(o2, o3):
        qb_ref[:, (c - o2) * LANES:(c - o2 + 1) * LANES] = (chunk(c) * QB_SCALE).astype(jnp.bfloat16)
    for c in range(o3, o4):
        kb_ref[:, (c - o3) * LANES:(c - o3 + 1) * LANES] = chunk(c).astype(jnp.bfloat16)
    gate_ref[...] = p[:, o4 * LANES:]


def _attn_in_proj(x, g, sc, sh, w, wvt, cos, sin, *, rope):
    n, d = x.shape
    tm = min(ROW_TILE, n)
    nw = w.shape[1]
    row = lambda i: (i, 0)
    fixed = lambda i: (0, 0)
    bf = jnp.bfloat16
    return pl.pallas_call(
        functools.partial(_attn_in_kernel, rope=rope),
        out_shape=[
            jax.ShapeDtypeStruct((n, A_WIDTH), bf),
            jax.ShapeDtypeStruct((n, A_KV_WIDTH), bf),
            jax.ShapeDtypeStruct((n, A_KV_WIDTH), bf),
            jax.ShapeDtypeStruct((n, B_WIDTH), bf),
            jax.ShapeDtypeStruct((n, B_WIDTH), bf),
            jax.ShapeDtypeStruct((B_WIDTH, n), bf),
            jax.ShapeDtypeStruct((n, ATTN_WIDTH), jnp.float32),
        ],
        grid=(n // tm,),
        in_specs=[
            pl.BlockSpec((tm, d), row),
            pl.BlockSpec((1, d), fixed),
            pl.BlockSpec((1, d), fixed),
            pl.BlockSpec((1, d), fixed),
            pl.BlockSpec((d, nw), fixed),
            pl.BlockSpec((B_WIDTH, d), fixed),
            pl.BlockSpec((tm, LANES), row),
            pl.BlockSpec((tm, LANES), row),
        ],
        out_specs=[
            pl.BlockSpec((tm, A_WIDTH), row),
            pl.BlockSpec((tm, A_KV_WIDTH), row),
            pl.BlockSpec((tm, A_KV_WIDTH), row),
            pl.BlockSpec((tm, B_WIDTH), row),
            pl.BlockSpec((tm, B_WIDTH), row),
            pl.BlockSpec((B_WIDTH, tm), lambda i: (0, i)),
            pl.BlockSpec((tm, ATTN_WIDTH), row),
        ],
        compiler_params=_params("parallel"),
        name="attn_in_proj",
    )(x, g, sc, sh, w, wvt, cos, sin)


def _gqa_kernel(*refs, has_window, nb):
    if has_window:
        (sink_ref, q_ref, kp_ref, kc_ref, kn_ref, vp_ref, vc_ref, vn_ref,
         kx_ref, vx_ref, o_ref) = refs
    else:
        sink_ref, q_ref, kx_ref, vx_ref, o_ref = refs
    i = pl.program_id(0)
    q = q_ref[...]
    kx = kx_ref[...]
    vx = vx_ref[...]
    tq = q.shape[0]
    if has_window:
        kw = jnp.concatenate([kp_ref[...], kc_ref[...], kn_ref[...]], axis=0)
        vw = jnp.concatenate([vp_ref[...], vc_ref[...], vn_ref[...]], axis=0)
        qi = lax.broadcasted_iota(jnp.int32, (tq, 3 * BLOCK), 0)
        kj = lax.broadcasted_iota(jnp.int32, (tq, 3 * BLOCK), 1)
        d = kj - BLOCK - qi
        valid = (d >= -WINDOW) & (d <= WINDOW)
        valid = valid & ((kj >= BLOCK) | (i > 0)) & ((kj < 2 * BLOCK) | (i < nb - 1))
    outs = []
    for g in range(A_KV_HEADS):
        lo, hi = g * HEAD_DIM, (g + 1) * HEAD_DIM
        kxg, vxg = kx[:, lo:hi], vx[:, lo:hi]
        if has_window:
            kwg, vwg = kw[:, lo:hi], vw[:, lo:hi]
        for j in range(A_GROUP):
            hq = g * A_GROUP + j
            qh = q[:, hq * HEAD_DIM:(hq + 1) * HEAD_DIM]
            sk = sink_ref[hq]
            s_c = lax.dot_general(qh, kxg, _NT, preferred_element_type=jnp.float32)
            m = jnp.maximum(jnp.max(s_c, axis=-1, keepdims=True), sk)
            if has_window:
                s_w = lax.dot_general(qh, kwg, _NT, preferred_element_type=jnp.float32)
                s_w = jnp.where(valid, s_w, NEG_INF)
                m = jnp.maximum(m, jnp.max(s_w, axis=-1, keepdims=True))
            p_c = jnp.exp(s_c - m)
            l = jnp.sum(p_c, axis=-1, keepdims=True) + jnp.exp(sk - m)
            o = jnp.dot(p_c.astype(jnp.bfloat16), vxg, preferred_element_type=jnp.float32)
            if has_window:
                p_w = jnp.exp(s_w - m)
                l = l + jnp.sum(p_w, axis=-1, keepdims=True)
                o = o + jnp.dot(p_w.astype(jnp.bfloat16), vwg, preferred_element_type=jnp.float32)
            outs.append(o / l)
    o_ref[...] = jnp.concatenate(outs, axis=-1)


def _gqa_attention(sink, q, k, v, kx, vx, *, has_window):
    n = q.shape[0]
    nx = kx.shape[0]
    smem = pl.BlockSpec(memory_space=pltpu.SMEM)
    fixed = lambda i: (0, 0)
    if has_window:
        nb = n // BLOCK
        tq = BLOCK
        prev = lambda i: (jnp.maximum(i - 1, 0), 0)
        cur = lambda i: (i, 0)
        nxt = lambda i: (jnp.minimum(i + 1, nb - 1), 0)
        kv_specs = [pl.BlockSpec((BLOCK, A_KV_WIDTH), f) for f in (prev, cur, nxt)]
        in_specs = ([smem, pl.BlockSpec((tq, A_WIDTH), cur)] + kv_specs + kv_specs +
                    [pl.BlockSpec((nx, A_KV_WIDTH), fixed)] * 2)
        args = (sink, q, k, k, k, v, v, v, kx, vx)
    else:
        nb = 1
        tq = n
        in_specs = [smem, pl.BlockSpec((tq, A_WIDTH), fixed)] + [pl.BlockSpec((nx, A_KV_WIDTH), fixed)] * 2
        args = (sink, q, kx, vx)
    return pl.pallas_call(
        functools.partial(_gqa_kernel, has_window=has_window, nb=nb),
        out_shape=jax.ShapeDtypeStruct((n, A_WIDTH), jnp.float32),
        grid=(nb,),
        in_specs=in_specs,
        out_specs=pl.BlockSpec((tq, A_WIDTH), lambda i: (i, 0)),
        compiler_params=_params("parallel"),
        name="gqa_window" if has_window else "gqa_ctx",
    )(*args)


def _diff_masked_q(q_ref):
    q = q_ref[...]
    lane = lax.broadcasted_iota(jnp.int32, q.shape, 1)
    zero = jnp.zeros_like(q)
    return jnp.concatenate([jnp.where(lane < HEAD_DIM, q, zero),
                            jnp.where(lane >= HEAD_DIM, q, zero)], axis=0)


def _diff_probs(s, m):
    return jnp.exp2((s - m).astype(jnp.bfloat16))


def _diff_pv(vt, p):
    ones = jnp.ones((DIFF_SUM_ROWS, vt.shape[1]), vt.dtype)
    return jnp.dot(jnp.concatenate([vt, ones], axis=0), p, preferred_element_type=jnp.float32)


def _diff_ctx_step(qz, kx, vxt):
    s = lax.dot_general(kx, qz, _NT, preferred_element_type=jnp.float32)
    m = jnp.max(s, axis=0, keepdims=True)
    return m, _diff_pv(vxt, _diff_probs(s, m))


def _diff_finalize(lam_ref, g_ref, acc, lam_init):
    hw = 2 * HEAD_DIM
    tq = acc.shape[1] // 2
    lv = lam_ref[...]
    lam = (jnp.exp(jnp.sum(lv[0:1] * lv[1:2], keepdims=True))
           - jnp.exp(jnp.sum(lv[2:3] * lv[3:4], keepdims=True)) + lam_init)
    on = acc[:hw] / acc[hw:hw + 1]
    o = on[:, :tq] - lam * on[:, tq:]
    ms = jnp.mean(o * o, axis=0, keepdims=True)
    o = (o * lax.rsqrt(ms + EPS)) * g_ref[...] * (1.0 - lam_init)
    return o.T


def _diff_ctx_kernel(lam_ref, g_ref, q_ref, kx_ref, vxt_ref, o_ref, *, lam_init):
    _, acc = _diff_ctx_step(_diff_masked_q(q_ref), kx_ref[...], vxt_ref[...])
    o_ref[...] = _diff_finalize(lam_ref, g_ref, acc, lam_init)


def _diff_kernel(lam_ref, g_ref, q_ref, k_ref, vt_ref, kx_ref, vxt_ref, o_ref,
                 s0, s1, s2, s3, p0, p1, p2, p3, acc_ref, *, lam_init, n_chunks):
    tk = DIFF_TK
    qz = _diff_masked_q(q_ref)
    tq2 = qz.shape[0]
    s_buf = (s0, s1, s2, s3)
    p_buf = (p0, p1, p2, p3)

    def scores(c, s_ref):
        k = k_ref[pl.ds(pl.multiple_of(c * tk, tk), tk), :]
        s = lax.dot_general(k, qz, _NT, preferred_element_type=jnp.float32)
        s_ref[...] = s
        return jnp.max(s, axis=0, keepdims=True)

    def softmax(s_ref, p_ref, m, mt):
        m_new = jnp.maximum(m, mt)
        p_ref[...] = _diff_probs(s_ref[...], m_new)
        return m_new, jnp.exp2(m - m_new)

    def pv(c, p_ref, alpha):
        vt = vt_ref[:, pl.ds(pl.multiple_of(c * tk, tk), tk)]
        acc_ref[...] = alpha * acc_ref[...] + _diff_pv(vt, p_ref[...])

    m, acc = _diff_ctx_step(qz, kx_ref[...], vxt_ref[...])
    acc_ref[...] = acc
    for i in (2, 3):
        p_buf[i][...] = jnp.zeros((tk, tq2), jnp.bfloat16)
    one = jnp.ones((1, tq2), jnp.float32)
    mt0 = scores(0, s_buf[0])
    mt1 = scores(1, s_buf[1])

    def body(j, carry):
        m, mts, alphas = carry
        mts, alphas = list(mts), list(alphas)
        for i in range(DIFF_STAGES):
            c = DIFF_STAGES * j + i
            pv(jnp.maximum(c - 2, 0), p_buf[(i + 2) % 4], alphas[i % 2])
            mt_next = scores(jnp.minimum(c + 2, n_chunks - 1), s_buf[(i + 2) % 4])
            m, alphas[i % 2] = softmax(s_buf[i % 4], p_buf[i % 4], m, mts[i % 2])
            mts[i % 2] = mt_next
        return m, tuple(mts), tuple(alphas)

    m, _, alphas = lax.fori_loop(0, n_chunks // DIFF_STAGES, body, (m, (mt0, mt1), (one, one)))
    pv(n_chunks - 2, p_buf[2], alphas[0])
    pv(n_chunks - 1, p_buf[3], alphas[1])
    o_ref[...] = _diff_finalize(lam_ref, g_ref, acc_ref[...], lam_init)


def _diff_attention(lamv, subln_col, q, k, vt, kx, vxt, *, has_latent, lam_init):
    n = q.shape[0]
    nx = kx.shape[0]
    tq = min(DIFF_TQ, n)
    hw = 2 * HEAD_DIM
    fixed = lambda h, i: (0, 0)
    in_specs = [pl.BlockSpec((4, HEAD_DIM), fixed), pl.BlockSpec((hw, 1), fixed),
                pl.BlockSpec((tq, hw), lambda h, i: (i, h))]
    args = [lamv, subln_col, q]
    scratch = []
    if has_latent:
        nk = k.shape[0]
        n_chunks = nk // DIFF_TK
        assert DIFF_STAGES % 4 == 0 and n_chunks % DIFF_STAGES == 0 and n_chunks * DIFF_TK == nk
        in_specs += [pl.BlockSpec((nk, hw), lambda h, i: (0, h)),
                     pl.BlockSpec((hw, nk), lambda h, i: (h, 0))]
        args += [k, vt]
        body = functools.partial(_diff_kernel, lam_init=lam_init, n_chunks=n_chunks)
        scratch = ([pltpu.VMEM((DIFF_TK, 2 * tq), jnp.float32)] * 4 +
                   [pltpu.VMEM((DIFF_TK, 2 * tq), jnp.bfloat16)] * 4 +
                   [pltpu.VMEM((hw + DIFF_SUM_ROWS, 2 * tq), jnp.float32)])
    else:
        body = functools.partial(_diff_ctx_kernel, lam_init=lam_init)
    in_specs += [pl.BlockSpec((nx, hw), lambda h, i: (0, h)),
                 pl.BlockSpec((hw, nx), lambda h, i: (h, 0))]
    args += [kx, vxt]
    return pl.pallas_call(
        body,
        out_shape=jax.ShapeDtypeStruct((n, B_WIDTH), jnp.float32),
        grid=(B_HEADS, n // tq),
        in_specs=in_specs,
        out_specs=pl.BlockSpec((tq, hw), lambda h, i: (i, h)),
        scratch_shapes=scratch,
        compiler_params=_params("parallel", "parallel"),
        name="diff_attn" if has_latent else "diff_attn_ctx",
    )(*args)


def _attn_out_kernel(oa_ref, ob_ref, gate_ref, x_ref, gt_ref, w_ref, o_ref):
    a = jnp.concatenate([oa_ref[...], ob_ref[...]], axis=-1) * _silu(gate_ref[...])
    y = jnp.dot(a.astype(jnp.bfloat16), w_ref[...], preferred_element_type=jnp.float32)
    o_ref[...] = x_ref[...] + gt_ref[...] * y


def _attn_out_proj(oa, ob, gate, x, gt, w):
    n, d = x.shape
    tm = min(ROW_TILE, n)
    row = lambda i: (i, 0)
    fixed = lambda i: (0, 0)
    return pl.pallas_call(
        _attn_out_kernel,
        out_shape=jax.ShapeDtypeStruct((n, d), jnp.float32),
        grid=(n // tm,),
        in_specs=[
            pl.BlockSpec((tm, A_WIDTH), row),
            pl.BlockSpec((tm, B_WIDTH), row),
            pl.BlockSpec((tm, ATTN_WIDTH), row),
            pl.BlockSpec((tm, d), row),
            pl.BlockSpec((1, d), fixed),
            pl.BlockSpec((ATTN_WIDTH, d), fixed),
        ],
        out_specs=pl.BlockSpec((tm, d), row),
        compiler_params=_params("parallel"),
        name="attn_out_proj",
    )(oa, ob, gate, x, gt, w)


def _rec_in_kernel(x_ref, g_ref, sc_ref, sh_ref, w_ref, xr_ref, gr_ref):
    h = _norm_mod(x_ref[...], g_ref[...], sc_ref[...], sh_ref[...]).astype(jnp.bfloat16)
    p = jnp.dot(h, w_ref[...], preferred_element_type=jnp.float32)
    xr_ref[...] = p[:, :RNN_WIDTH]
    gr_ref[...] = p[:, RNN_WIDTH:]


def _rec_in_proj(x, g, sc, sh, w):
    n, d = x.shape
    tm = min(ROW_TILE, n)
    row = lambda i: (i, 0)
    fixed = lambda i: (0, 0)
    return pl.pallas_call(
        _rec_in_kernel,
        out_shape=[jax.ShapeDtypeStruct((n, RNN_WIDTH), jnp.float32)] * 2,
        grid=(n // tm,),
        in_specs=[
            pl.BlockSpec((tm, d), row),
            pl.BlockSpec((1, d), fixed),
            pl.BlockSpec((1, d), fixed),
            pl.BlockSpec((1, d), fixed),
            pl.BlockSpec((d, 2 * RNN_WIDTH), fixed),
        ],
        out_specs=[pl.BlockSpec((tm, RNN_WIDTH), row)] * 2,
        compiler_params=_params("parallel"),
        name="rec_in_proj",
    )(x, g, sc, sh, w)


def _rec_scan_kernel(*refs, reverse, fuse_out, nt):
    if fuse_out:
        (x_ref, xp_ref, xn_ref, cw_ref, cb_ref, w_ref, bias_ref, lam_ref, h0_ref,
         yr_ref, gr_ref, xl_ref, gt_ref, wo_ref, fg_ref,
         o_ref, xe_ref, a_ref, b_ref, y_ref, h_ref) = refs
    else:
        (x_ref, xp_ref, xn_ref, cw_ref, cb_ref, w_ref, bias_ref, lam_ref, h0_ref,
         y_ref, hf_ref, xe_ref, a_ref, b_ref, h_ref) = refs
    i = pl.program_id(0)
    t = i if not reverse else nt - 1 - i
    tm = x_ref.shape[0]

    @pl.when(i == 0)
    def _():
        h_ref[...] = h0_ref[...]

    xe_ref[0:SUBLANES, :] = jnp.where(t > 0, xp_ref[...], 0.0)
    xe_ref[SUBLANES:SUBLANES + tm, :] = x_ref[...]
    xe_ref[SUBLANES + tm:, :] = jnp.where(t < nt - 1, xn_ref[...], 0.0)
    cw = cw_ref[...]
    u = cb_ref[...] + sum(cw[k:k + 1] * xe_ref[SUBLANES - 2 + k:SUBLANES - 2 + k + tm, :]
                          for k in range(CONV_WIDTH))
    z = jnp.dot(u.astype(jnp.bfloat16), w_ref[...], preferred_element_type=jnp.float32) + bias_ref[...]
    r = jax.nn.sigmoid(z[:, :RNN_WIDTH])
    ig = jax.nn.sigmoid(z[:, RNN_WIDTH:])
    nl = -lam_ref[...]
    softplus = jnp.maximum(nl, 0.0) + jnp.log1p(jnp.exp(-jnp.abs(nl)))
    log_a = -RG_C * r * softplus
    a = jnp.exp(log_a)
    a_ref[...] = a
    b_ref[...] = jnp.sqrt(1.0 - a * a) * ig * u

    def row(s, h):
        rr = (tm - 1 - s) if reverse else s
        h = a_ref[pl.ds(rr, 1), :] * h + b_ref[pl.ds(rr, 1), :]
        y_ref[pl.ds(rr, 1), :] = h
        return h

    h = lax.fori_loop(0, tm, row, h_ref[...], unroll=8)
    h_ref[...] = h

    if fuse_out:
        y = (y_ref[...] + yr_ref[...]) * _silu(gr_ref[...])
        out = jnp.dot(y.astype(jnp.bfloat16), wo_ref[...], preferred_element_type=jnp.float32)
        xl = xl_ref[...] + gt_ref[...] * out
        ms = jnp.mean(xl * xl, axis=-1, keepdims=True)
        o_ref[...] = xl * lax.rsqrt(ms + EPS) * fg_ref[...]
    else:
        hf_ref[...] = h


def _rec_scan(xr, cw, cb, w, bias, lam, h0, *, reverse, fused=None):
    n = xr.shape[0]
    tm = min(SCAN_TILE, n)
    nt = n // tm
    per = tm // SUBLANES
    nblk8 = n // SUBLANES
    pos = (lambda i: nt - 1 - i) if reverse else (lambda i: i)
    row = lambda i: (pos(i), 0)
    prev = lambda i: (jnp.maximum(pos(i) * per - 1, 0), 0)
    nxt = lambda i: (jnp.minimum((pos(i) + 1) * per, nblk8 - 1), 0)
    fixed = lambda i: (0, 0)
    vec = pl.BlockSpec((1, RNN_WIDTH), fixed)
    in_specs = [
        pl.BlockSpec((tm, RNN_WIDTH), row),
        pl.BlockSpec((SUBLANES, RNN_WIDTH), prev),
        pl.BlockSpec((SUBLANES, RNN_WIDTH), nxt),
        pl.BlockSpec((CONV_WIDTH, RNN_WIDTH), fixed),
        vec,
        pl.BlockSpec((RNN_WIDTH, 2 * RNN_WIDTH), fixed),
        pl.BlockSpec((1, 2 * RNN_WIDTH), fixed),
        vec,
        vec,
    ]
    args = [xr, xr, xr, cw, cb, w, bias, lam, h0]
    scratch = [pltpu.VMEM((tm + 2 * SUBLANES, RNN_WIDTH), jnp.float32),
               pltpu.VMEM((tm, RNN_WIDTH), jnp.float32),
               pltpu.VMEM((tm, RNN_WIDTH), jnp.float32)]
    if fused is not None:
        yr, gr, xl, gt, wo, fg = fused
        d = xl.shape[1]
        in_specs += [
            pl.BlockSpec((tm, RNN_WIDTH), row),
            pl.BlockSpec((tm, RNN_WIDTH), row),
            pl.BlockSpec((tm, d), row),
            pl.BlockSpec((1, d), fixed),
            pl.BlockSpec((RNN_WIDTH, d), fixed),
            pl.BlockSpec((1, d), fixed),
        ]
        args += [yr, gr, xl, gt, wo, fg]
        out_shape = jax.ShapeDtypeStruct((n, d), jnp.float32)
        out_specs = pl.BlockSpec((tm, d), row)
        scratch = scratch + [pltpu.VMEM((tm, RNN_WIDTH), jnp.float32)]
    else:
        out_shape = [jax.ShapeDtypeStruct((n, RNN_WIDTH), jnp.float32),
                     jax.ShapeDtypeStruct((1, RNN_WIDTH), jnp.float32)]
        out_specs = [pl.BlockSpec((tm, RNN_WIDTH), row), vec]
    scratch = scratch + [pltpu.VMEM((1, RNN_WIDTH), jnp.float32)]
    return pl.pallas_call(
        functools.partial(_rec_scan_kernel, reverse=reverse, fuse_out=fused is not None, nt=nt),
        out_shape=out_shape,
        grid=(nt,),
        in_specs=in_specs,
        out_specs=out_specs,
        scratch_shapes=scratch,
        compiler_params=_params("arbitrary"),
        name="rec_scan_out" if fused is not None else ("rec_scan_rev" if reverse else "rec_scan_fwd"),
    )(*args)


def _rope_tables(n):
    pos = jnp.arange(n, dtype=jnp.int32)
    row = (pos // GRID_W).astype(jnp.float32)
    col = (pos % GRID_W).astype(jnp.float32)
    n_freq = HEAD_DIM // 4
    inv_freq = ROPE_BASE ** (-jnp.arange(n_freq, dtype=jnp.float32) / n_freq)
    ar = row[:, None] * inv_freq
    ac = col[:, None] * inv_freq
    ang = jnp.concatenate([ar, ar, ac, ac], axis=-1)
    sign = jnp.where((jnp.arange(HEAD_DIM) % 32) < 16, -1.0, 1.0)
    cos = jnp.cos(ang)
    sin = jnp.sin(ang) * sign
    return jnp.tile(cos, (1, 2)), jnp.tile(sin, (1, 2))


def _block_diag(w):
    nb, bi, bj = w.shape
    eye = jnp.eye(nb, dtype=w.dtype)
    return jnp.einsum('hij,hg->higj', w, eye).reshape(nb * bi, nb * bj)


def kernel(x, c, ctx, c_ctx, norm_g, ada_w, ada_b, attn_w_in, attn_w_out, attn_sink, lam_q1, lam_k1,
           lam_q2, lam_k2, subln_g, rec_w_in, rec_conv_w, rec_conv_b, rec_wa, rec_ba, rec_wx, rec_bx,
           rec_lam, rec_w_out, final_g):
    assert x.shape[0] == 1 and c.shape[0] == 1 and ctx.shape[0] == 1
    bf = jnp.bfloat16
    d = D_MODEL
    xl = x[0]
    xc = ctx[0]
    n = xl.shape[0]

    cv = jnp.zeros((SUBLANES, d), jnp.float32).at[0].set(c[0]).at[1].set(c_ctx)
    mods = _adaln(cv, ada_w, ada_b)

    def mod(l, r):
        m = mods[l, r]
        return m[None, :d], m[None, d:2 * d], m[None, 2 * d:]

    lam_init = 0.8 - 0.6 * math.exp(-0.3 * 0)
    s5 = A_WIDTH + 2 * A_KV_WIDTH + 2 * B_WIDTH
    s6 = s5 + B_WIDTH
    w_in = attn_w_in[0]
    w_main = jnp.concatenate([w_in[:, :s5], w_in[:, s6:]], axis=1).astype(bf)
    w_vt = w_in[:, s5:s6].T.astype(bf)
    w_out = attn_w_out[0].astype(bf)
    g0 = norm_g[0][None]
    cos, sin = _rope_tables(n)
    sh, sc, gt = mod(0, 0)
    shc, scc, gtc = mod(0, 1)
    qa, ka, va, qb, kb, vbt, gate = _attn_in_proj(xl, g0, sc, sh, w_main, w_vt, cos, sin, rope=True)
    nc = xc.shape[0]
    qac, kac, vac, qbc, kbc, vbtc, gatec = _attn_in_proj(
        xc, g0, scc, shc, w_main, w_vt, cos[:nc], sin[:nc], rope=False)

    sink = attn_sink[0]
    lamv = jnp.stack([lam_q1[0], lam_k1[0], lam_q2[0], lam_k2[0]])
    subln_col = subln_g[0][:, None]
    oa = _gqa_attention(sink, qa, ka, va, kac, vac, has_window=True)
    ob = _diff_attention(lamv, subln_col, qb, kb, vbt, kbc, vbtc, has_latent=True, lam_init=lam_init)
    oac = _gqa_attention(sink, qac, None, None, kac, vac, has_window=False)
    obc = _diff_attention(lamv, subln_col, qbc, None, None, kbc, vbtc, has_latent=False,
                          lam_init=lam_init)
    xl = _attn_out_proj(oa, ob, gate, xl, gt, w_out)
    xc = _attn_out_proj(oac, obc, gatec, xc, gtc, w_out)

    g1 = norm_g[1][None]
    sh, sc, gt = mod(1, 0)
    shc, scc, _ = mod(1, 1)
    w_rin = rec_w_in[0].astype(bf)
    xr, gr = _rec_in_proj(xl, g1, sc, sh, w_rin)
    xrc, _ = _rec_in_proj(xc, g1, scc, shc, w_rin)
    cw = rec_conv_w[0]
    cb = rec_conv_b[0][None]
    zero_h = jnp.zeros((1, RNN_WIDTH), jnp.float32)

    def coeff(dr):
        w = jnp.concatenate([_block_diag(rec_wa[0, dr]), _block_diag(rec_wx[0, dr])], axis=1).astype(bf)
        bias = jnp.concatenate([rec_ba[0, dr], rec_bx[0, dr]])[None]
        return w, bias, rec_lam[0, dr][None]

    wf, bias_f, lam_f = coeff(0)
    wr, bias_r, lam_r = coeff(1)
    _, h0f = _rec_scan(xrc, cw, cb, wf, bias_f, lam_f, zero_h, reverse=False)
    _, h0r = _rec_scan(xrc, cw, cb, wr, bias_r, lam_r, zero_h, reverse=True)
    yr, _ = _rec_scan(xr, cw, cb, wr, bias_r, lam_r, h0r, reverse=True)
    out = _rec_scan(xr, cw, cb, wf, bias_f, lam_f, h0f, reverse=False,
                    fused=(yr, gr, xl, gt, rec_w_out[0].astype(bf), final_g[None]))
    return out[None]
```

```python
import functools
import math

import jax
import jax.numpy as jnp
from jax import lax
from jax.experimental import pallas as pl
from jax.experimental.pallas import tpu as pltpu

D_MODEL = 1024
GRID_W = 64
HEAD_DIM = 64
ROPE_BASE = 10000.0
EPS = 1e-6
NEG_INF = -1e30
A_Q_HEADS = 8
A_KV_HEADS = 2
A_GROUP = A_Q_HEADS // A_KV_HEADS
A_WIDTH = A_Q_HEADS * HEAD_DIM
A_KV_WIDTH = A_KV_HEADS * HEAD_DIM
WINDOW = 128
BLOCK = 128
B_HEADS = 4
B_WIDTH = B_HEADS * 2 * HEAD_DIM
ATTN_WIDTH = A_WIDTH + B_WIDTH
RNN_WIDTH = 1280
RNN_BLOCKS = 16
RNN_BLOCK_DIM = RNN_WIDTH // RNN_BLOCKS
CONV_WIDTH = 4
RG_C = 8.0
LOG2E = math.log2(math.e)
Q_SCALE_LOG2 = HEAD_DIM ** -0.5 * LOG2E

LANES = 128
SUBLANES = 8
VMEM_LIMIT = 48 * 1024 * 1024

ROW_TILE = 512
SCAN_TILE = 256
DIFF_TQ = 256
DIFF_TK = 512
DIFF_STAGES = 16
DIFF_SUM_ROWS = 16

_NT = (((1,), (1,)), ((), ()))


def _silu(x):
    return x * jax.nn.sigmoid(x)


def _norm_mod(x, g, sc, sh):
    ms = jnp.mean(x * x, axis=-1, keepdims=True)
    return (x * lax.rsqrt(ms + EPS) * g) * (1.0 + sc) + sh


def _params(*sem, flags=None):
    return pltpu.CompilerParams(dimension_semantics=sem, vmem_limit_bytes=VMEM_LIMIT, flags=flags)


def _adaln_kernel(cv_ref, w_ref, b_ref, o_ref):
    s = _silu(cv_ref[...])
    o_ref[0] = jnp.dot(s, w_ref[0], preferred_element_type=jnp.float32) + b_ref[0]


def _adaln(cv, ada_w, ada_b):
    depth, d, n3 = ada_w.shape
    tn = 1024
    return pl.pallas_call(
        _adaln_kernel,
        out_shape=jax.ShapeDtypeStruct((depth, SUBLANES, n3), jnp.float32),
        grid=(depth, n3 // tn),
        in_specs=[
            pl.BlockSpec((SUBLANES, d), lambda l, j: (0, 0)),
            pl.BlockSpec((1, d, tn), lambda l, j: (l, 0, j)),
            pl.BlockSpec((1, 1, tn), lambda l, j: (l, 0, j)),
        ],
        out_specs=pl.BlockSpec((1, SUBLANES, tn), lambda l, j: (l, 0, j)),
        compiler_params=_params("parallel", "parallel"),
        name="adaln",
    )(cv, ada_w, ada_b.reshape(depth, 1, n3))


def _rope(x, cos, sin_signed, first_half):
    rot = jnp.where(first_half, pltpu.roll(x, LANES - 16, 1), pltpu.roll(x, 16, 1))
    return x * cos + rot * sin_signed


def _attn_in_kernel(x_ref, g_ref, sc_ref, sh_ref, w_ref, wvt_ref, cos_ref, sin_ref,
                    qa_ref, ka_ref, qb_ref, kb_ref, vt_ref, gate_ref, *, rope):
    h = _norm_mod(x_ref[...], g_ref[...], sc_ref[...], sh_ref[...]).astype(jnp.bfloat16)
    p = jnp.dot(h, w_ref[...], preferred_element_type=jnp.float32)
    vt_ref[...] = lax.dot_general(wvt_ref[...], h, _NT,
                                  preferred_element_type=jnp.float32).astype(jnp.bfloat16)
    if rope:
        cos = cos_ref[...]
        sin = sin_ref[...]
        lane = lax.broadcasted_iota(jnp.int32, cos.shape, 1)
        first_half = (lane % 32) < 16

    def chunk(c):
        x = p[:, c * LANES:(c + 1) * LANES]
        return _rope(x, cos, sin, first_half) if rope else x

    o0 = A_WIDTH // LANES
    o1 = o0 + A_KV_WIDTH // LANES
    o2 = o1 + B_WIDTH // LANES
    o3 = o2 + B_WIDTH // LANES
    for c in range(o0):
        qa_ref[:, c * LANES:(c + 1) * LANES] = (chunk(c) * Q_SCALE_LOG2).astype(jnp.bfloat16)
    for c in range(o0, o1):
        ka_ref[:, (c - o0) * LANES:(c - o0 + 1) * LANES] = chunk(c).astype(jnp.bfloat16)
    for c in range(o1, o2):
        qb_ref[:, (c - o1) * LANES:(c - o1 + 1) * LANES] = (chunk(c) * Q_SCALE_LOG2).astype(jnp.bfloat16)
    for c in range(o2, o3):
        kb_ref[:, (c - o2) * LANES:(c - o2 + 1) * LANES] = chunk(c).astype(jnp.bfloat16)
    gate_ref[...] = p[:, o3 * LANES:]


def _attn_in_proj(x, g, sc, sh, w, wvt, cos, sin, *, rope):
    n, d = x.shape
    tm = min(ROW_TILE, n)
    nw = w.shape[1]
    nv = wvt.shape[0]
    row = lambda i: (i, 0)
    fixed = lambda i: (0, 0)
    bf = jnp.bfloat16
    return pl.pallas_call(
        functools.partial(_attn_in_kernel, rope=rope),
        out_shape=[
            jax.ShapeDtypeStruct((n, A_WIDTH), bf),
            jax.ShapeDtypeStruct((n, A_KV_WIDTH), bf),
            jax.ShapeDtypeStruct((n, B_WIDTH), bf),
            jax.ShapeDtypeStruct((n, B_WIDTH), bf),
            jax.ShapeDtypeStruct((nv, n), bf),
            jax.ShapeDtypeStruct((n, ATTN_WIDTH), jnp.float32),
        ],
        grid=(n // tm,),
        in_specs=[
            pl.BlockSpec((tm, d), row),
            pl.BlockSpec((1, d), fixed),
            pl.BlockSpec((1, d), fixed),
            pl.BlockSpec((1, d), fixed),
            pl.BlockSpec((d, nw), fixed),
            pl.BlockSpec((nv, d), fixed),
            pl.BlockSpec((tm, LANES), row),
            pl.BlockSpec((tm, LANES), row),
        ],
        out_specs=[
            pl.BlockSpec((tm, A_WIDTH), row),
            pl.BlockSpec((tm, A_KV_WIDTH), row),
            pl.BlockSpec((tm, B_WIDTH), row),
            pl.BlockSpec((tm, B_WIDTH), row),
            pl.BlockSpec((nv, tm), lambda i: (0, i)),
            pl.BlockSpec((tm, ATTN_WIDTH), row),
        ],
        compiler_params=_params("parallel"),
        name="attn_in_proj",
    )(x, g, sc, sh, w, wvt, cos, sin)


def _gqa_kernel(*refs, has_window, nb):
    if has_window:
        (sink_ref, bias_ref, q_ref, kp_ref, kc_ref, kn_ref, vp_ref, vc_ref, vn_ref,
         kx_ref, vxt_ref, o_ref) = refs
    else:
        sink_ref, q_ref, kx_ref, vxt_ref, o_ref = refs
    i = pl.program_id(0)
    q = q_ref[...]
    tq = q.shape[0]
    if has_window:
        keys = jnp.concatenate([kp_ref[...], kc_ref[...], kn_ref[...], kx_ref[...]], axis=0)
        vt = jnp.concatenate([vp_ref[...], vc_ref[...], vn_ref[...], vxt_ref[...]], axis=1)
        row = lax.broadcasted_iota(jnp.int32, (3 * BLOCK, 1), 0)
        off_end = ((row < BLOCK) & (i == 0)) | ((row >= 2 * BLOCK) & (i == nb - 1))
        bias = bias_ref[...] + jnp.where(off_end, NEG_INF, 0.0)
        bias = jnp.concatenate([bias] * A_GROUP, axis=1)
    else:
        keys = kx_ref[...]
        vt = vxt_ref[...]
    ones = jnp.ones((DIFF_SUM_ROWS, vt.shape[1]), vt.dtype)
    vt = jnp.concatenate([vt, ones], axis=0)
    lane = lax.broadcasted_iota(jnp.int32, (tq, LANES), 1)
    zero = jnp.zeros((tq, LANES), q.dtype)
    outs = [None] * A_Q_HEADS
    for g in range(A_KV_HEADS):
        in_g = (lane >= g * HEAD_DIM) & (lane < (g + 1) * HEAD_DIM)
        qg = jnp.concatenate([jnp.where(in_g, q[:, j * LANES:(j + 1) * LANES], zero)
                              for j in range(A_GROUP)], axis=0)
        s = lax.dot_general(keys, qg, _NT, preferred_element_type=jnp.float32)
        if has_window:
            s = jnp.concatenate([s[:3 * BLOCK] + bias, s[3 * BLOCK:]], axis=0)
        sk = sink_ref[g:g + 1, :]
        m = jnp.maximum(jnp.max(s, axis=0, keepdims=True), sk)
        acc = jnp.dot(vt, jnp.exp2((s - m).astype(jnp.bfloat16)),
                      preferred_element_type=jnp.float32)
        l = acc[A_KV_WIDTH:A_KV_WIDTH + 1] + jnp.exp2(sk - m)
        o = acc[g * HEAD_DIM:(g + 1) * HEAD_DIM] / l
        for j in range(A_GROUP):
            outs[2 * j + g] = o[:, j * tq:(j + 1) * tq]
    o_ref[...] = jnp.concatenate(outs, axis=0).T


def _gqa_attention(sink_cols, bias, q, k, vt, kx, vxt, *, has_window):
    n = q.shape[0]
    nx = kx.shape[0]
    fixed = lambda i: (0, 0)
    if has_window:
        nb = n // BLOCK
        tq = BLOCK
        prev = lambda i: (jnp.maximum(i - 1, 0), 0)
        cur = lambda i: (i, 0)
        nxt = lambda i: (jnp.minimum(i + 1, nb - 1), 0)
        k_specs = [pl.BlockSpec((BLOCK, A_KV_WIDTH), f) for f in (prev, cur, nxt)]
        vt_specs = [pl.BlockSpec((A_KV_WIDTH, BLOCK), lambda i, f=f: f(i)[::-1]) for f in (prev, cur, nxt)]
        in_specs = ([pl.BlockSpec((A_KV_HEADS, A_GROUP * tq), fixed),
                     pl.BlockSpec((3 * BLOCK, tq), fixed),
                     pl.BlockSpec((tq, A_WIDTH), cur)] + k_specs + vt_specs +
                    [pl.BlockSpec((nx, A_KV_WIDTH), fixed), pl.BlockSpec((A_KV_WIDTH, nx), fixed)])
        args = (sink_cols, bias, q, k, k, k, vt, vt, vt, kx, vxt)
    else:
        nb = 1
        tq = n
        in_specs = [pl.BlockSpec((A_KV_HEADS, A_GROUP * tq), fixed), pl.BlockSpec((tq, A_WIDTH), fixed),
                    pl.BlockSpec((nx, A_KV_WIDTH), fixed), pl.BlockSpec((A_KV_WIDTH, nx), fixed)]
        args = (sink_cols, q, kx, vxt)
    return pl.pallas_call(
        functools.partial(_gqa_kernel, has_window=has_window, nb=nb),
        out_shape=jax.ShapeDtypeStruct((n, A_WIDTH), jnp.float32),
        grid=(nb,),
        in_specs=in_specs,
        out_specs=pl.BlockSpec((tq, A_WIDTH), lambda i: (i, 0)),
        compiler_params=_params("parallel"),
        name="gqa_window" if has_window else "gqa_ctx",
    )(*args)


def _diff_masked_q(q_ref):
    q = q_ref[...]
    lane = lax.broadcasted_iota(jnp.int32, q.shape, 1)
    zero = jnp.zeros_like(q)
    return jnp.concatenate([jnp.where(lane < HEAD_DIM, q, zero),
                            jnp.where(lane >= HEAD_DIM, q, zero)], axis=0)


def _diff_probs(s, m):
    return jnp.exp2((s - m).astype(jnp.bfloat16))


def _diff_pv(vt, p):
    ones = jnp.ones((DIFF_SUM_ROWS, vt.shape[1]), vt.dtype)
    return jnp.dot(jnp.concatenate([vt, ones], axis=0), p, preferred_element_type=jnp.float32)


def _diff_ctx_step(qz, kx, vxt):
    s = lax.dot_general(kx, qz, _NT, preferred_element_type=jnp.float32)
    m = jnp.max(s, axis=0, keepdims=True)
    return m, _diff_pv(vxt, _diff_probs(s, m))


def _diff_finalize(lam_ref, g_ref, acc, lam_init):
    hw = 2 * HEAD_DIM
    tq = acc.shape[1] // 2
    lv = lam_ref[...]
    lam = (jnp.exp(jnp.sum(lv[0:1] * lv[1:2], keepdims=True))
           - jnp.exp(jnp.sum(lv[2:3] * lv[3:4], keepdims=True)) + lam_init)
    on = acc[:hw] / acc[hw:hw + 1]
    o = on[:, :tq] - lam * on[:, tq:]
    ms = jnp.mean(o * o, axis=0, keepdims=True)
    o = (o * lax.rsqrt(ms + EPS)) * g_ref[...] * (1.0 - lam_init)
    return o.T


def _diff_ctx_kernel(lam_ref, g_ref, q_ref, kx_ref, vxt_ref, o_ref, *, lam_init):
    _, acc = _diff_ctx_step(_diff_masked_q(q_ref), kx_ref[...], vxt_ref[...])
    o_ref[...] = _diff_finalize(lam_ref, g_ref, acc, lam_init)


def _diff_kernel(lam_ref, g_ref, q_ref, k_ref, vt_ref, kx_ref, vxt_ref, o_ref,
                 s0, s1, s2, s3, p0, p1, p2, p3, acc_ref, *, lam_init, n_chunks):
    tk = DIFF_TK
    qz = _diff_masked_q(q_ref)
    tq2 = qz.shape[0]
    s_buf = (s0, s1, s2, s3)
    p_buf = (p0, p1, p2, p3)

    def scores(c, s_ref):
        k = k_ref[pl.ds(pl.multiple_of(c * tk, tk), tk), :]
        s = lax.dot_general(k, qz, _NT, preferred_element_type=jnp.float32)
        s_ref[...] = s
        return jnp.max(s, axis=0, keepdims=True)

    def softmax(s_ref, p_ref, m, mt):
        m_new = jnp.maximum(m, mt)
        p_ref[...] = _diff_probs(s_ref[...], m_new)
        return m_new, jnp.exp2(m - m_new)

    def pv(c, p_ref, alpha):
        vt = vt_ref[:, pl.ds(pl.multiple_of(c * tk, tk), tk)]
        acc_ref[...] = alpha * acc_ref[...] + _diff_pv(vt, p_ref[...])

    m, acc = _diff_ctx_step(qz, kx_ref[...], vxt_ref[...])
    acc_ref[...] = acc
    for i in (2, 3):
        p_buf[i][...] = jnp.zeros((tk, tq2), jnp.bfloat16)
    one = jnp.ones((1, tq2), jnp.float32)
    mt0 = scores(0, s_buf[0])
    mt1 = scores(1, s_buf[1])

    def body(j, carry):
        m, mts, alphas = carry
        mts, alphas = list(mts), list(alphas)
        for i in range(DIFF_STAGES):
            c = DIFF_STAGES * j + i
            pv(jnp.maximum(c - 2, 0), p_buf[(i + 2) % 4], alphas[i % 2])
            mt_next = scores(jnp.minimum(c + 2, n_chunks - 1), s_buf[(i + 2) % 4])
            m, alphas[i % 2] = softmax(s_buf[i % 4], p_buf[i % 4], m, mts[i % 2])
            mts[i % 2] = mt_next
        return m, tuple(mts), tuple(alphas)

    m, _, alphas = lax.fori_loop(0, n_chunks // DIFF_STAGES, body, (m, (mt0, mt1), (one, one)))
    pv(n_chunks - 2, p_buf[2], alphas[0])
    pv(n_chunks - 1, p_buf[3], alphas[1])
    o_ref[...] = _diff_finalize(lam_ref, g_ref, acc_ref[...], lam_init)


def _diff_attention(lamv, subln_col, q, k, vt, kx, vxt, *, has_latent, lam_init):
    n = q.shape[0]
    nx = kx.shape[0]
    tq = min(DIFF_TQ, n)
    hw = 2 * HEAD_DIM
    fixed = lambda h, i: (0, 0)
    in_specs = [pl.BlockSpec((4, HEAD_DIM), fixed), pl.BlockSpec((hw, 1), fixed),
                pl.BlockSpec((tq, hw), lambda h, i: (i, h))]
    args = [lamv, subln_col, q]
    scratch = []
    if has_latent:
        nk = k.shape[0]
        n_chunks = nk // DIFF_TK
        assert DIFF_STAGES % 4 == 0 and n_chunks % DIFF_STAGES == 0 and n_chunks * DIFF_TK == nk
        in_specs += [pl.BlockSpec((nk, hw), lambda h, i: (0, h)),
                     pl.BlockSpec((hw, nk), lambda h, i: (h + 1, 0))]
        args += [k, vt]
        body = functools.partial(_diff_kernel, lam_init=lam_init, n_chunks=n_chunks)
        scratch = ([pltpu.VMEM((DIFF_TK, 2 * tq), jnp.float32)] * 4 +
                   [pltpu.VMEM((DIFF_TK, 2 * tq), jnp.bfloat16)] * 4 +
                   [pltpu.VMEM((hw + DIFF_SUM_ROWS, 2 * tq), jnp.float32)])
    else:
        body = functools.partial(_diff_ctx_kernel, lam_init=lam_init)
    in_specs += [pl.BlockSpec((nx, hw), lambda h, i: (0, h)),
                 pl.BlockSpec((hw, nx), lambda h, i: (h + 1, 0))]
    args += [kx, vxt]
    return pl.pallas_call(
        body,
        out_shape=jax.ShapeDtypeStruct((n, B_WIDTH), jnp.float32),
        grid=(B_HEADS, n // tq),
        in_specs=in_specs,
        out_specs=pl.BlockSpec((tq, hw), lambda h, i: (i, h)),
        scratch_shapes=scratch,
        compiler_params=_params("parallel", "parallel"),
        name="diff_attn" if has_latent else "diff_attn_ctx",
    )(*args)


def _attn_out_kernel(oa_ref, ob_ref, gate_ref, x_ref, gt_ref, w_ref, o_ref):
    a = jnp.concatenate([oa_ref[...], ob_ref[...]], axis=-1) * _silu(gate_ref[...])
    y = jnp.dot(a.astype(jnp.bfloat16), w_ref[...], preferred_element_type=jnp.float32)
    o_ref[...] = x_ref[...] + gt_ref[...] * y


def _attn_out_proj(oa, ob, gate, x, gt, w):
    n, d = x.shape
    tm = min(ROW_TILE, n)
    row = lambda i: (i, 0)
    fixed = lambda i: (0, 0)
    return pl.pallas_call(
        _attn_out_kernel,
        out_shape=jax.ShapeDtypeStruct((n, d), jnp.float32),
        grid=(n // tm,),
        in_specs=[
            pl.BlockSpec((tm, A_WIDTH), row),
            pl.BlockSpec((tm, B_WIDTH), row),
            pl.BlockSpec((tm, ATTN_WIDTH), row),
            pl.BlockSpec((tm, d), row),
            pl.BlockSpec((1, d), fixed),
            pl.BlockSpec((ATTN_WIDTH, d), fixed),
        ],
        out_specs=pl.BlockSpec((tm, d), row),
        compiler_params=_params("parallel"),
        name="attn_out_proj",
    )(oa, ob, gate, x, gt, w)


def _rec_in_kernel(x_ref, g_ref, sc_ref, sh_ref, w_ref, xr_ref, gr_ref):
    h = _norm_mod(x_ref[...], g_ref[...], sc_ref[...], sh_ref[...]).astype(jnp.bfloat16)
    p = jnp.dot(h, w_ref[...], preferred_element_type=jnp.float32)
    xr_ref[...] = p[:, :RNN_WIDTH]
    gr_ref[...] = p[:, RNN_WIDTH:]


def _rec_in_proj(x, g, sc, sh, w):
    n, d = x.shape
    tm = min(ROW_TILE, n)
    row = lambda i: (i, 0)
    fixed = lambda i: (0, 0)
    return pl.pallas_call(
        _rec_in_kernel,
        out_shape=[jax.ShapeDtypeStruct((n, RNN_WIDTH), jnp.float32)] * 2,
        grid=(n // tm,),
        in_specs=[
            pl.BlockSpec((tm, d), row),
            pl.BlockSpec((1, d), fixed),
            pl.BlockSpec((1, d), fixed),
            pl.BlockSpec((1, d), fixed),
            pl.BlockSpec((d, 2 * RNN_WIDTH), fixed),
        ],
        out_specs=[pl.BlockSpec((tm, RNN_WIDTH), row)] * 2,
        compiler_params=_params("parallel"),
        name="rec_in_proj",
    )(x, g, sc, sh, w)


def _rec_scan_kernel(*refs, reverse, fuse_out, nt):
    if fuse_out:
        (x_ref, xp_ref, xn_ref, cw_ref, cb_ref, w_ref, bias_ref, lam_ref, h0_ref,
         yr_ref, gr_ref, xl_ref, gt_ref, wo_ref, fg_ref,
         o_ref, xe_ref, a_ref, b_ref, y_ref, h_ref) = refs
    else:
        (x_ref, xp_ref, xn_ref, cw_ref, cb_ref, w_ref, bias_ref, lam_ref, h0_ref,
         y_ref, hf_ref, xe_ref, a_ref, b_ref, h_ref) = refs
    i = pl.program_id(0)
    t = i if not reverse else nt - 1 - i
    tm = x_ref.shape[0]

    @pl.when(i == 0)
    def _():
        h_ref[...] = h0_ref[...]

    xe_ref[0:SUBLANES, :] = jnp.where(t > 0, xp_ref[...], 0.0)
    xe_ref[SUBLANES:SUBLANES + tm, :] = x_ref[...]
    xe_ref[SUBLANES + tm:, :] = jnp.where(t < nt - 1, xn_ref[...], 0.0)
    cw = cw_ref[...]
    uh = cb_ref[...] + sum(cw[k:k + 1] * xe_ref[SUBLANES - 2 + k:SUBLANES - 2 + k + tm, :]
                           for k in range(CONV_WIDTH))
    ub = uh.astype(jnp.bfloat16)
    half = RNN_WIDTH // 2
    z0 = jnp.dot(ub[:, :half], w_ref[0], preferred_element_type=jnp.float32)
    z1 = jnp.dot(ub[:, half:], w_ref[1], preferred_element_type=jnp.float32)
    bias = bias_ref[...]
    ta = jnp.tanh(jnp.concatenate([z0[:, :half], z1[:, :half]], axis=1) + bias[:, :RNN_WIDTH])
    tx = jnp.tanh(jnp.concatenate([z0[:, half:], z1[:, half:]], axis=1) + bias[:, RNN_WIDTH:])
    nl = -lam_ref[...]
    softplus = jnp.maximum(nl, 0.0) + jnp.log1p(jnp.exp(-jnp.abs(nl)))
    coef = (-0.5 * RG_C * LOG2E) * softplus
    a = jnp.exp2(coef + coef * ta)
    a_ref[...] = a
    om = 1.0 - a * a
    b_ref[...] = (om * lax.rsqrt(jnp.maximum(om, 1e-30))) * ((1.0 + tx) * uh)

    def row(s, h):
        rr = (tm - 1 - s) if reverse else s
        h = a_ref[pl.ds(rr, 1), :] * h + b_ref[pl.ds(rr, 1), :]
        y_ref[pl.ds(rr, 1), :] = h
        return h

    h = lax.fori_loop(0, tm, row, h_ref[...], unroll=8)
    h_ref[...] = h

    if fuse_out:
        y = (y_ref[...] + yr_ref[...]) * _silu(gr_ref[...])
        out = jnp.dot(y.astype(jnp.bfloat16), wo_ref[...], preferred_element_type=jnp.float32)
        xl = xl_ref[...] + gt_ref[...] * out
        ms = jnp.mean(xl * xl, axis=-1, keepdims=True)
        o_ref[...] = xl * lax.rsqrt(ms + EPS) * fg_ref[...]
    else:
        hf_ref[...] = h


def _rec_scan(xr, cw, cb, w, bias, lam, h0, *, reverse, fused=None):
    n = xr.shape[0]
    tm = min(SCAN_TILE, n)
    nt = n // tm
    per = tm // SUBLANES
    nblk8 = n // SUBLANES
    pos = (lambda i: nt - 1 - i) if reverse else (lambda i: i)
    row = lambda i: (pos(i), 0)
    prev = lambda i: (jnp.maximum(pos(i) * per - 1, 0), 0)
    nxt = lambda i: (jnp.minimum((pos(i) + 1) * per, nblk8 - 1), 0)
    fixed = lambda i: (0, 0)
    vec = pl.BlockSpec((1, RNN_WIDTH), fixed)
    in_specs = [
        pl.BlockSpec((tm, RNN_WIDTH), row),
        pl.BlockSpec((SUBLANES, RNN_WIDTH), prev),
        pl.BlockSpec((SUBLANES, RNN_WIDTH), nxt),
        pl.BlockSpec((CONV_WIDTH, RNN_WIDTH), fixed),
        vec,
        pl.BlockSpec((2, RNN_WIDTH // 2, RNN_WIDTH), lambda i: (0, 0, 0)),
        pl.BlockSpec((1, 2 * RNN_WIDTH), fixed),
        vec,
        vec,
    ]
    args = [xr, xr, xr, cw, cb, w, bias, lam, h0]
    scratch = [pltpu.VMEM((tm + 2 * SUBLANES, RNN_WIDTH), jnp.float32),
               pltpu.VMEM((tm, RNN_WIDTH), jnp.float32),
               pltpu.VMEM((tm, RNN_WIDTH), jnp.float32)]
    if fused is not None:
        yr, gr, xl, gt, wo, fg = fused
        d = xl.shape[1]
        in_specs += [
            pl.BlockSpec((tm, RNN_WIDTH), row),
            pl.BlockSpec((tm, RNN_WIDTH), row),
            pl.BlockSpec((tm, d), row),
            pl.BlockSpec((1, d), fixed),
            pl.BlockSpec((RNN_WIDTH, d), fixed),
            pl.BlockSpec((1, d), fixed),
        ]
        args += [yr, gr, xl, gt, wo, fg]
        out_shape = jax.ShapeDtypeStruct((n, d), jnp.float32)
        out_specs = pl.BlockSpec((tm, d), row)
        scratch = scratch + [pltpu.VMEM((tm, RNN_WIDTH), jnp.float32)]
    else:
        out_shape = [jax.ShapeDtypeStruct((n, RNN_WIDTH), jnp.float32),
                     jax.ShapeDtypeStruct((1, RNN_WIDTH), jnp.float32)]
        out_specs = [pl.BlockSpec((tm, RNN_WIDTH), row), vec]
    scratch = scratch + [pltpu.VMEM((1, RNN_WIDTH), jnp.float32)]
    return pl.pallas_call(
        functools.partial(_rec_scan_kernel, reverse=reverse, fuse_out=fused is not None, nt=nt),
        out_shape=out_shape,
        grid=(nt,),
        in_specs=in_specs,
        out_specs=out_specs,
        scratch_shapes=scratch,
        compiler_params=_params("arbitrary"),
        name="rec_scan_out" if fused is not None else ("rec_scan_rev" if reverse else "rec_scan_fwd"),
    )(*args)


def _rope_tables(n):
    pos = jnp.arange(n, dtype=jnp.int32)
    row = (pos // GRID_W).astype(jnp.float32)
    col = (pos % GRID_W).astype(jnp.float32)
    n_freq = HEAD_DIM // 4
    inv_freq = ROPE_BASE ** (-jnp.arange(n_freq, dtype=jnp.float32) / n_freq)
    ar = row[:, None] * inv_freq
    ac = col[:, None] * inv_freq
    ang = jnp.concatenate([ar, ar, ac, ac], axis=-1)
    sign = jnp.where((jnp.arange(HEAD_DIM) % 32) < 16, -1.0, 1.0)
    cos = jnp.cos(ang)
    sin = jnp.sin(ang) * sign
    return jnp.tile(cos, (1, 2)), jnp.tile(sin, (1, 2))


def _block_diag(w):
    nb, bi, bj = w.shape
    eye = jnp.eye(nb, dtype=w.dtype)
    return jnp.einsum('hij,hg->higj', w, eye).reshape(nb * bi, nb * bj)


def kernel(x, c, ctx, c_ctx, norm_g, ada_w, ada_b, attn_w_in, attn_w_out, attn_sink, lam_q1, lam_k1,
           lam_q2, lam_k2, subln_g, rec_w_in, rec_conv_w, rec_conv_b, rec_wa, rec_ba, rec_wx, rec_bx,
           rec_lam, rec_w_out, final_g):
    assert x.shape[0] == 1 and c.shape[0] == 1 and ctx.shape[0] == 1
    bf = jnp.bfloat16
    d = D_MODEL
    xl = x[0]
    xc = ctx[0]
    n = xl.shape[0]

    cv = jnp.zeros((SUBLANES, d), jnp.float32).at[0].set(c[0]).at[1].set(c_ctx)
    mods = _adaln(cv, ada_w, ada_b)

    def mod(l, r):
        m = mods[l, r]
        return m[None, :d], m[None, d:2 * d], m[None, 2 * d:]

    lam_init = 0.8 - 0.6 * math.exp(-0.3 * 0)
    s1 = A_WIDTH
    s2 = s1 + A_KV_WIDTH
    s3 = s2 + A_KV_WIDTH
    s5 = s3 + 2 * B_WIDTH
    s6 = s5 + B_WIDTH
    head_perm = jnp.array([g * A_GROUP + j for j in range(A_GROUP) for g in range(A_KV_HEADS)])
    col_perm = (head_perm[:, None] * HEAD_DIM + jnp.arange(HEAD_DIM)).reshape(-1)
    w_in = attn_w_in[0]
    w_main = jnp.concatenate([w_in[:, :s1][:, col_perm], w_in[:, s1:s2], w_in[:, s3:s5],
                              w_in[:, s6:s6 + A_WIDTH][:, col_perm], w_in[:, s6 + A_WIDTH:]],
                             axis=1).astype(bf)
    w_vt = jnp.concatenate([w_in[:, s2:s3], w_in[:, s5:s6]], axis=1).T.astype(bf)
    w_out = attn_w_out[0]
    w_out = jnp.concatenate([w_out[:A_WIDTH][col_perm], w_out[A_WIDTH:]], axis=0).astype(bf)
    g0 = norm_g[0][None]
    cos, sin = _rope_tables(n)
    sh, sc, gt = mod(0, 0)
    shc, scc, gtc = mod(0, 1)
    qa, ka, qb, kb, vt, gate = _attn_in_proj(xl, g0, sc, sh, w_main, w_vt, cos, sin, rope=True)
    nc = xc.shape[0]
    qac, kac, qbc, kbc, vtc, gatec = _attn_in_proj(
        xc, g0, scc, shc, w_main, w_vt, cos[:nc], sin[:nc], rope=False)

    sink_gj = (attn_sink[0] * LOG2E).reshape(A_KV_HEADS, A_GROUP)
    kj = jnp.arange(3 * BLOCK)[:, None]
    qi = jnp.arange(BLOCK)[None, :]
    band_bias = jnp.where(jnp.abs(kj - BLOCK - qi) <= WINDOW, 0.0, NEG_INF).astype(jnp.float32)
    lamv = jnp.stack([lam_q1[0], lam_k1[0], lam_q2[0], lam_k2[0]])
    subln_col = subln_g[0][:, None]
    oa = _gqa_attention(jnp.repeat(sink_gj, BLOCK, axis=1), band_bias, qa, ka, vt, kac, vtc,
                        has_window=True)
    ob = _diff_attention(lamv, subln_col, qb, kb, vt, kbc, vtc, has_latent=True, lam_init=lam_init)
    oac = _gqa_attention(jnp.repeat(sink_gj, nc, axis=1), None, qac, None, None, kac, vtc,
                         has_window=False)
    obc = _diff_attention(lamv, subln_col, qbc, None, None, kbc, vtc, has_latent=False,
                          lam_init=lam_init)
    xl = _attn_out_proj(oa, ob, gate, xl, gt, w_out)
    xc = _attn_out_proj(oac, obc, gatec, xc, gtc, w_out)

    g1 = norm_g[1][None]
    sh, sc, gt = mod(1, 0)
    shc, scc, _ = mod(1, 1)
    w_rin = rec_w_in[0].astype(bf)
    xr, gr = _rec_in_proj(xl, g1, sc, sh, w_rin)
    xrc, _ = _rec_in_proj(xc, g1, scc, shc, w_rin)
    cw = 0.5 * rec_conv_w[0]
    cb = 0.5 * rec_conv_b[0][None]
    zero_h = jnp.zeros((1, RNN_WIDTH), jnp.float32)

    def coeff(dr):
        hb = RNN_BLOCKS // 2
        w = jnp.stack([jnp.concatenate([_block_diag(rec_wa[0, dr, i * hb:(i + 1) * hb]),
                                        _block_diag(rec_wx[0, dr, i * hb:(i + 1) * hb])], axis=1)
                       for i in range(2)]).astype(bf)
        bias = 0.5 * jnp.concatenate([rec_ba[0, dr], rec_bx[0, dr]])[None]
        return w, bias, rec_lam[0, dr][None]

    wf, bias_f, lam_f = coeff(0)
    wr, bias_r, lam_r = coeff(1)
    _, h0f = _rec_scan(xrc, cw, cb, wf, bias_f, lam_f, zero_h, reverse=False)
    _, h0r = _rec_scan(xrc, cw, cb, wr, bias_r, lam_r, zero_h, reverse=True)
    yr, _ = _rec_scan(xr, cw, cb, wr, bias_r, lam_r, h0r, reverse=True)
    out = _rec_scan(xr, cw, cb, wf, bias_f, lam_f, h0f, reverse=False,
                    fused=(yr, gr, xl, gt, rec_w_out[0].astype(bf), final_g[None]))
    return out[None]
```

```python
import functools
import math

import jax
import jax.numpy as jnp
from jax import lax
from jax.experimental import pallas as pl
from jax.experimental.pallas import tpu as pltpu

D_MODEL = 1024
GRID_W = 64
HEAD_DIM = 64
ROPE_BASE = 10000.0
EPS = 1e-6
NEG_INF = -1e30
A_Q_HEADS = 8
A_KV_HEADS = 2
A_GROUP = A_Q_HEADS // A_KV_HEADS
A_WIDTH = A_Q_HEADS * HEAD_DIM
A_KV_WIDTH = A_KV_HEADS * HEAD_DIM
WINDOW = 128
BLOCK = 128
B_HEADS = 4
B_WIDTH = B_HEADS * 2 * HEAD_DIM
ATTN_WIDTH = A_WIDTH + B_WIDTH
RNN_WIDTH = 1280
RNN_BLOCKS = 16
RNN_BLOCK_DIM = RNN_WIDTH // RNN_BLOCKS
CONV_WIDTH = 4
RG_C = 8.0
LOG2E = math.log2(math.e)
Q_SCALE_LOG2 = HEAD_DIM ** -0.5 * LOG2E

LANES = 128
SUBLANES = 8
VMEM_LIMIT = 48 * 1024 * 1024

ROW_TILE = 512
SCAN_TILE = 256
DIFF_TQ = 256
DIFF_TK = 512
DIFF_STAGES = 16
DIFF_SUM_ROWS = 16

_NT = (((1,), (1,)), ((), ()))


def _silu(x):
    return x * jax.nn.sigmoid(x)


def _norm_mod(x, g, sc, sh):
    ms = jnp.mean(x * x, axis=-1, keepdims=True)
    return (x * lax.rsqrt(ms + EPS) * g) * (1.0 + sc) + sh


def _params(*sem, flags=None):
    return pltpu.CompilerParams(dimension_semantics=sem, vmem_limit_bytes=VMEM_LIMIT, flags=flags)


def _adaln_kernel(cv_ref, w_ref, b_ref, o_ref):
    s = _silu(cv_ref[...])
    o_ref[0] = jnp.dot(s, w_ref[0], preferred_element_type=jnp.float32) + b_ref[0]


def _adaln(cv, ada_w, ada_b):
    depth, d, n3 = ada_w.shape
    tn = 1024
    return pl.pallas_call(
        _adaln_kernel,
        out_shape=jax.ShapeDtypeStruct((depth, SUBLANES, n3), jnp.float32),
        grid=(depth, n3 // tn),
        in_specs=[
            pl.BlockSpec((SUBLANES, d), lambda l, j: (0, 0)),
            pl.BlockSpec((1, d, tn), lambda l, j: (l, 0, j)),
            pl.BlockSpec((1, 1, tn), lambda l, j: (l, 0, j)),
        ],
        out_specs=pl.BlockSpec((1, SUBLANES, tn), lambda l, j: (l, 0, j)),
        compiler_params=_params("parallel", "parallel"),
        name="adaln",
    )(cv, ada_w, ada_b.reshape(depth, 1, n3))


def _rope(x, cos, sin_signed, first_half):
    rot = jnp.where(first_half, pltpu.roll(x, LANES - 16, 1), pltpu.roll(x, 16, 1))
    return x * cos + rot * sin_signed


def _attn_in_kernel(x_ref, g_ref, sc_ref, sh_ref, w_ref, wvt_ref, cos_ref, sin_ref,
                    qa_ref, ka_ref, qb_ref, kb_ref, vt_ref, gate_ref, *, rope):
    h = _norm_mod(x_ref[...], g_ref[...], sc_ref[...], sh_ref[...]).astype(jnp.bfloat16)
    p = jnp.dot(h, w_ref[...], preferred_element_type=jnp.float32)
    vt_ref[...] = lax.dot_general(wvt_ref[...], h, _NT,
                                  preferred_element_type=jnp.float32).astype(jnp.bfloat16)
    if rope:
        cos = cos_ref[...]
        sin = sin_ref[...]
        lane = lax.broadcasted_iota(jnp.int32, cos.shape, 1)
        first_half = (lane % 32) < 16

    def chunk(c):
        x = p[:, c * LANES:(c + 1) * LANES]
        return _rope(x, cos, sin, first_half) if rope else x

    o0 = A_WIDTH // LANES
    o1 = o0 + A_KV_WIDTH // LANES
    o2 = o1 + B_WIDTH // LANES
    o3 = o2 + B_WIDTH // LANES
    for c in range(o0):
        qa_ref[:, c * LANES:(c + 1) * LANES] = (chunk(c) * Q_SCALE_LOG2).astype(jnp.bfloat16)
    for c in range(o0, o1):
        ka_ref[:, (c - o0) * LANES:(c - o0 + 1) * LANES] = chunk(c).astype(jnp.bfloat16)
    for c in range(o1, o2):
        qb_ref[:, (c - o1) * LANES:(c - o1 + 1) * LANES] = (chunk(c) * Q_SCALE_LOG2).astype(jnp.bfloat16)
    for c in range(o2, o3):
        kb_ref[:, (c - o2) * LANES:(c - o2 + 1) * LANES] = chunk(c).astype(jnp.bfloat16)
    gate_ref[...] = p[:, o3 * LANES:].astype(jnp.bfloat16)


def _attn_in_proj(x, g, sc, sh, w, wvt, cos, sin, *, rope):
    n, d = x.shape
    tm = min(ROW_TILE, n)
    nw = w.shape[1]
    nv = wvt.shape[0]
    row = lambda i: (i, 0)
    fixed = lambda i: (0, 0)
    bf = jnp.bfloat16
    return pl.pallas_call(
        functools.partial(_attn_in_kernel, rope=rope),
        out_shape=[
            jax.ShapeDtypeStruct((n, A_WIDTH), bf),
            jax.ShapeDtypeStruct((n, A_KV_WIDTH), bf),
            jax.ShapeDtypeStruct((n, B_WIDTH), bf),
            jax.ShapeDtypeStruct((n, B_WIDTH), bf),
            jax.ShapeDtypeStruct((nv, n), bf),
            jax.ShapeDtypeStruct((n, ATTN_WIDTH), bf),
        ],
        grid=(n // tm,),
        in_specs=[
            pl.BlockSpec((tm, d), row),
            pl.BlockSpec((1, d), fixed),
            pl.BlockSpec((1, d), fixed),
            pl.BlockSpec((1, d), fixed),
            pl.BlockSpec((d, nw), fixed),
            pl.BlockSpec((nv, d), fixed),
            pl.BlockSpec((tm, LANES), row),
            pl.BlockSpec((tm, LANES), row),
        ],
        out_specs=[
            pl.BlockSpec((tm, A_WIDTH), row),
            pl.BlockSpec((tm, A_KV_WIDTH), row),
            pl.BlockSpec((tm, B_WIDTH), row),
            pl.BlockSpec((tm, B_WIDTH), row),
            pl.BlockSpec((nv, tm), lambda i: (0, i)),
            pl.BlockSpec((tm, ATTN_WIDTH), row),
        ],
        compiler_params=_params("parallel"),
        name="attn_in_proj",
    )(x, g, sc, sh, w, wvt, cos, sin)


def _gqa_kernel(*refs, has_window, nb):
    if has_window:
        (sink_ref, bias_ref, q_ref, kp_ref, kc_ref, kn_ref, vp_ref, vc_ref, vn_ref,
         kx_ref, vxt_ref, o_ref) = refs
    else:
        sink_ref, q_ref, kx_ref, vxt_ref, o_ref = refs
    i = pl.program_id(0)
    q = q_ref[...]
    tq = q.shape[0]
    if has_window:
        keys = jnp.concatenate([kp_ref[...], kc_ref[...], kn_ref[...], kx_ref[...]], axis=0)
        vt = jnp.concatenate([vp_ref[...], vc_ref[...], vn_ref[...], vxt_ref[...]], axis=1)
        row = lax.broadcasted_iota(jnp.int32, (3 * BLOCK, 1), 0)
        off_end = ((row < BLOCK) & (i == 0)) | ((row >= 2 * BLOCK) & (i == nb - 1))
        bias = bias_ref[...] + jnp.where(off_end, NEG_INF, 0.0)
        bias = jnp.concatenate([bias] * A_GROUP, axis=1)
    else:
        keys = kx_ref[...]
        vt = vxt_ref[...]
    ones = jnp.ones((DIFF_SUM_ROWS, vt.shape[1]), vt.dtype)
    vt = jnp.concatenate([vt, ones], axis=0)
    lane = lax.broadcasted_iota(jnp.int32, (tq, LANES), 1)
    zero = jnp.zeros((tq, LANES), q.dtype)
    outs = [None] * A_Q_HEADS
    scores = []
    for g in range(A_KV_HEADS):
        in_g = (lane >= g * HEAD_DIM) & (lane < (g + 1) * HEAD_DIM)
        qg = jnp.concatenate([jnp.where(in_g, q[:, j * LANES:(j + 1) * LANES], zero)
                              for j in range(A_GROUP)], axis=0)
        scores.append(lax.dot_general(keys, qg, _NT, preferred_element_type=jnp.float32))
    for g in range(A_KV_HEADS):
        s = scores[g]
        if has_window:
            s = jnp.concatenate([s[:3 * BLOCK] + bias, s[3 * BLOCK:]], axis=0)
        sk = sink_ref[g:g + 1, :]
        m = jnp.maximum(jnp.max(s, axis=0, keepdims=True), sk)
        acc = jnp.dot(vt, jnp.exp2((s - m).astype(jnp.bfloat16)),
                      preferred_element_type=jnp.float32)
        l = acc[A_KV_WIDTH:A_KV_WIDTH + 1] + jnp.exp2(sk - m)
        o = acc[g * HEAD_DIM:(g + 1) * HEAD_DIM] / l
        for j in range(A_GROUP):
            outs[2 * j + g] = o[:, j * tq:(j + 1) * tq]
    o_ref[...] = jnp.concatenate(outs, axis=0).T.astype(o_ref.dtype)


def _gqa_attention(sink_cols, bias, q, k, vt, kx, vxt, *, has_window):
    n = q.shape[0]
    nx = kx.shape[0]
    fixed = lambda i: (0, 0)
    if has_window:
        nb = n // BLOCK
        tq = BLOCK
        prev = lambda i: (jnp.maximum(i - 1, 0), 0)
        cur = lambda i: (i, 0)
        nxt = lambda i: (jnp.minimum(i + 1, nb - 1), 0)
        k_specs = [pl.BlockSpec((BLOCK, A_KV_WIDTH), f) for f in (prev, cur, nxt)]
        vt_specs = [pl.BlockSpec((A_KV_WIDTH, BLOCK), lambda i, f=f: f(i)[::-1]) for f in (prev, cur, nxt)]
        in_specs = ([pl.BlockSpec((A_KV_HEADS, A_GROUP * tq), fixed),
                     pl.BlockSpec((3 * BLOCK, tq), fixed),
                     pl.BlockSpec((tq, A_WIDTH), cur)] + k_specs + vt_specs +
                    [pl.BlockSpec((nx, A_KV_WIDTH), fixed), pl.BlockSpec((A_KV_WIDTH, nx), fixed)])
        args = (sink_cols, bias, q, k, k, k, vt, vt, vt, kx, vxt)
    else:
        nb = 1
        tq = n
        in_specs = [pl.BlockSpec((A_KV_HEADS, A_GROUP * tq), fixed), pl.BlockSpec((tq, A_WIDTH), fixed),
                    pl.BlockSpec((nx, A_KV_WIDTH), fixed), pl.BlockSpec((A_KV_WIDTH, nx), fixed)]
        args = (sink_cols, q, kx, vxt)
    return pl.pallas_call(
        functools.partial(_gqa_kernel, has_window=has_window, nb=nb),
        out_shape=jax.ShapeDtypeStruct((n, A_WIDTH), jnp.bfloat16),
        grid=(nb,),
        in_specs=in_specs,
        out_specs=pl.BlockSpec((tq, A_WIDTH), lambda i: (i, 0)),
        compiler_params=_params("parallel"),
        name="gqa_window" if has_window else "gqa_ctx",
    )(*args)


def _diff_masked_q(q_ref):
    q = q_ref[...]
    lane = lax.broadcasted_iota(jnp.int32, q.shape, 1)
    zero = jnp.zeros_like(q)
    return jnp.concatenate([jnp.where(lane < HEAD_DIM, q, zero),
                            jnp.where(lane >= HEAD_DIM, q, zero)], axis=0)


def _diff_probs(s, m):
    return jnp.exp2((s - m).astype(jnp.bfloat16))


def _diff_pv(vt, p):
    ones = jnp.ones((DIFF_SUM_ROWS, vt.shape[1]), vt.dtype)
    return jnp.dot(jnp.concatenate([vt, ones], axis=0), p, preferred_element_type=jnp.float32)


def _diff_ctx_step(qz, kx, vxt):
    s = lax.dot_general(kx, qz, _NT, preferred_element_type=jnp.float32)
    m = jnp.max(s, axis=0, keepdims=True)
    return m, _diff_pv(vxt, _diff_probs(s, m))


def _diff_finalize(lam_ref, g_ref, acc, lam_init):
    hw = 2 * HEAD_DIM
    tq = acc.shape[1] // 2
    lv = lam_ref[...]
    lam = (jnp.exp(jnp.sum(lv[0:1] * lv[1:2], keepdims=True))
           - jnp.exp(jnp.sum(lv[2:3] * lv[3:4], keepdims=True)) + lam_init)
    on = acc[:hw] / acc[hw:hw + 1]
    o = on[:, :tq] - lam * on[:, tq:]
    ms = jnp.mean(o * o, axis=0, keepdims=True)
    o = (o * lax.rsqrt(ms + EPS)) * g_ref[...] * (1.0 - lam_init)
    return o.T.astype(jnp.bfloat16)


def _diff_ctx_kernel(lam_ref, g_ref, q_ref, kx_ref, vxt_ref, o_ref, *, lam_init):
    _, acc = _diff_ctx_step(_diff_masked_q(q_ref), kx_ref[...], vxt_ref[...])
    o_ref[...] = _diff_finalize(lam_ref, g_ref, acc, lam_init)


def _diff_kernel(lam_ref, g_ref, q_ref, k_ref, vt_ref, kx_ref, vxt_ref, o_ref,
                 s0, s1, s2, s3, p0, p1, p2, p3, acc_ref, *, lam_init, n_chunks):
    tk = DIFF_TK
    qz = _diff_masked_q(q_ref)
    tq2 = qz.shape[0]
    s_buf = (s0, s1, s2, s3)
    p_buf = (p0, p1, p2, p3)

    def scores(c, s_ref):
        k = k_ref[pl.ds(pl.multiple_of(c * tk, tk), tk), :]
        s = lax.dot_general(k, qz, _NT, preferred_element_type=jnp.float32)
        s_ref[...] = s
        return jnp.max(s, axis=0, keepdims=True)

    def softmax(s_ref, p_ref, m, mt):
        m_new = jnp.maximum(m, mt)
        p_ref[...] = _diff_probs(s_ref[...], m_new)
        return m_new, jnp.exp2(m - m_new)

    def pv(c, p_ref, alpha):
        vt = vt_ref[:, pl.ds(pl.multiple_of(c * tk, tk), tk)]
        acc_ref[...] = alpha * acc_ref[...] + _diff_pv(vt, p_ref[...])

    s_ctx = lax.dot_general(kx_ref[...], qz, _NT, preferred_element_type=jnp.float32)
    mt0 = scores(0, s_buf[0])
    mt1 = scores(1, s_buf[1])
    m = jnp.max(s_ctx, axis=0, keepdims=True)
    acc_ref[...] = _diff_pv(vxt_ref[...], _diff_probs(s_ctx, m))
    for i in (2, 3):
        p_buf[i][...] = jnp.zeros((tk, tq2), jnp.bfloat16)
    one = jnp.ones((1, tq2), jnp.float32)

    def body(j, carry):
        m, mts, alphas = carry
        alphas = list(alphas)
        for i in range(0, DIFF_STAGES, 2):
            c = DIFF_STAGES * j + i
            nxt = [scores(jnp.minimum(c + d + 2, n_chunks - 1), s_buf[(i + d + 2) % 4]) for d in (0, 1)]
            for d in (0, 1):
                pv(jnp.maximum(c + d - 2, 0), p_buf[(i + d + 2) % 4], alphas[d])
            for d in (0, 1):
                m, alphas[d] = softmax(s_buf[(i + d) % 4], p_buf[(i + d) % 4], m, mts[d])
            mts = nxt
        return m, tuple(mts), tuple(alphas)

    m, _, alphas = lax.fori_loop(0, n_chunks // DIFF_STAGES, body, (m, (mt0, mt1), (one, one)))
    pv(n_chunks - 2, p_buf[2], alphas[0])
    pv(n_chunks - 1, p_buf[3], alphas[1])
    o_ref[...] = _diff_finalize(lam_ref, g_ref, acc_ref[...], lam_init)


def _diff_attention(lamv, subln_col, q, k, vt, kx, vxt, *, has_latent, lam_init):
    n = q.shape[0]
    nx = kx.shape[0]
    tq = min(DIFF_TQ, n)
    hw = 2 * HEAD_DIM
    fixed = lambda h, i: (0, 0)
    in_specs = [pl.BlockSpec((4, HEAD_DIM), fixed), pl.BlockSpec((hw, 1), fixed),
                pl.BlockSpec((tq, hw), lambda h, i: (i, h))]
    args = [lamv, subln_col, q]
    scratch = []
    if has_latent:
        nk = k.shape[0]
        n_chunks = nk // DIFF_TK
        assert DIFF_STAGES % 4 == 0 and n_chunks % DIFF_STAGES == 0 and n_chunks * DIFF_TK == nk
        in_specs += [pl.BlockSpec((nk, hw), lambda h, i: (0, h)),
                     pl.BlockSpec((hw, nk), lambda h, i: (h + 1, 0))]
        args += [k, vt]
        body = functools.partial(_diff_kernel, lam_init=lam_init, n_chunks=n_chunks)
        scratch = ([pltpu.VMEM((DIFF_TK, 2 * tq), jnp.float32)] * 4 +
                   [pltpu.VMEM((DIFF_TK, 2 * tq), jnp.bfloat16)] * 4 +
                   [pltpu.VMEM((hw + DIFF_SUM_ROWS, 2 * tq), jnp.float32)])
    else:
        body = functools.partial(_diff_ctx_kernel, lam_init=lam_init)
    in_specs += [pl.BlockSpec((nx, hw), lambda h, i: (0, h)),
                 pl.BlockSpec((hw, nx), lambda h, i: (h + 1, 0))]
    args += [kx, vxt]
    return pl.pallas_call(
        body,
        out_shape=jax.ShapeDtypeStruct((n, B_WIDTH), jnp.bfloat16),
        grid=(B_HEADS, n // tq),
        in_specs=in_specs,
        out_specs=pl.BlockSpec((tq, hw), lambda h, i: (i, h)),
        scratch_shapes=scratch,
        compiler_params=_params("parallel", "parallel"),
        name="diff_attn" if has_latent else "diff_attn_ctx",
    )(*args)


def _attn_out_kernel(oa_ref, ob_ref, gate_ref, x_ref, gt_ref, w_ref, o_ref):
    o = jnp.concatenate([oa_ref[...], ob_ref[...]], axis=-1).astype(jnp.float32)
    a = o * _silu(gate_ref[...].astype(jnp.float32))
    y = jnp.dot(a.astype(jnp.bfloat16), w_ref[...], preferred_element_type=jnp.float32)
    o_ref[...] = x_ref[...] + gt_ref[...] * y


def _attn_out_proj(oa, ob, gate, x, gt, w):
    n, d = x.shape
    tm = min(ROW_TILE, n)
    row = lambda i: (i, 0)
    fixed = lambda i: (0, 0)
    return pl.pallas_call(
        _attn_out_kernel,
        out_shape=jax.ShapeDtypeStruct((n, d), jnp.float32),
        grid=(n // tm,),
        in_specs=[
            pl.BlockSpec((tm, A_WIDTH), row),
            pl.BlockSpec((tm, B_WIDTH), row),
            pl.BlockSpec((tm, ATTN_WIDTH), row),
            pl.BlockSpec((tm, d), row),
            pl.BlockSpec((1, d), fixed),
            pl.BlockSpec((ATTN_WIDTH, d), fixed),
        ],
        out_specs=pl.BlockSpec((tm, d), row),
        compiler_params=_params("parallel"),
        name="attn_out_proj",
    )(oa, ob, gate, x, gt, w)


def _rec_in_kernel(x_ref, g_ref, sc_ref, sh_ref, w_ref, xr_ref, gr_ref):
    h = _norm_mod(x_ref[...], g_ref[...], sc_ref[...], sh_ref[...]).astype(jnp.bfloat16)
    p = jnp.dot(h, w_ref[...], preferred_element_type=jnp.float32)
    xr_ref[...] = p[:, :RNN_WIDTH]
    gr_ref[...] = p[:, RNN_WIDTH:]


def _rec_in_proj(x, g, sc, sh, w):
    n, d = x.shape
    tm = min(ROW_TILE, n)
    row = lambda i: (i, 0)
    fixed = lambda i: (0, 0)
    return pl.pallas_call(
        _rec_in_kernel,
        out_shape=[jax.ShapeDtypeStruct((n, RNN_WIDTH), jnp.float32)] * 2,
        grid=(n // tm,),
        in_specs=[
            pl.BlockSpec((tm, d), row),
            pl.BlockSpec((1, d), fixed),
            pl.BlockSpec((1, d), fixed),
            pl.BlockSpec((1, d), fixed),
            pl.BlockSpec((d, 2 * RNN_WIDTH), fixed),
        ],
        out_specs=[pl.BlockSpec((tm, RNN_WIDTH), row)] * 2,
        compiler_params=_params("parallel"),
        name="rec_in_proj",
    )(x, g, sc, sh, w)


def _rec_scan_kernel(*refs, reverse, fuse_out, nt):
    if fuse_out:
        (x_ref, xp_ref, xn_ref, cw_ref, cb_ref, w_ref, bias_ref, lam_ref, h0_ref,
         yr_ref, gr_ref, xl_ref, gt_ref, wo_ref, fg_ref,
         o_ref, xe_ref, a_ref, b_ref, y_ref, h_ref) = refs
    else:
        (x_ref, xp_ref, xn_ref, cw_ref, cb_ref, w_ref, bias_ref, lam_ref, h0_ref,
         y_ref, hf_ref, xe_ref, a_ref, b_ref, h_ref) = refs
    i = pl.program_id(0)
    t = i if not reverse else nt - 1 - i
    tm = x_ref.shape[0]

    @pl.when(i == 0)
    def _():
        h_ref[...] = h0_ref[...]

    xe_ref[0:SUBLANES, :] = jnp.where(t > 0, xp_ref[...], 0.0)
    xe_ref[SUBLANES:SUBLANES + tm, :] = x_ref[...]
    xe_ref[SUBLANES + tm:, :] = jnp.where(t < nt - 1, xn_ref[...], 0.0)
    cw = cw_ref[...]
    uh = cb_ref[...] + sum(cw[k:k + 1] * xe_ref[SUBLANES - 2 + k:SUBLANES - 2 + k + tm, :]
                           for k in range(CONV_WIDTH))
    ub = uh.astype(jnp.bfloat16)
    half = RNN_WIDTH // 2
    z0 = jnp.dot(ub[:, :half], w_ref[0], preferred_element_type=jnp.float32)
    z1 = jnp.dot(ub[:, half:], w_ref[1], preferred_element_type=jnp.float32)
    bias = bias_ref[...]
    ta = jnp.tanh(jnp.concatenate([z0[:, :half], z1[:, :half]], axis=1) + bias[:, :RNN_WIDTH])
    tx = jnp.tanh(jnp.concatenate([z0[:, half:], z1[:, half:]], axis=1) + bias[:, RNN_WIDTH:])
    nl = -lam_ref[...]
    softplus = jnp.maximum(nl, 0.0) + jnp.log1p(jnp.exp(-jnp.abs(nl)))
    coef = (-0.5 * RG_C * LOG2E) * softplus
    a = jnp.exp2(coef + coef * ta)
    a_ref[...] = a
    om = 1.0 - a * a
    b_ref[...] = (om * lax.rsqrt(jnp.maximum(om, 1e-30))) * ((1.0 + tx) * uh)

    def row(s, h):
        rr = (tm - 1 - s) if reverse else s
        h = a_ref[pl.ds(rr, 1), :] * h + b_ref[pl.ds(rr, 1), :]
        y_ref[pl.ds(rr, 1), :] = h
        return h

    h = lax.fori_loop(0, tm, row, h_ref[...], unroll=8)
    h_ref[...] = h

    if fuse_out:
        y = (y_ref[...] + yr_ref[...]) * _silu(gr_ref[...])
        out = jnp.dot(y.astype(jnp.bfloat16), wo_ref[...], preferred_element_type=jnp.float32)
        xl = xl_ref[...] + gt_ref[...] * out
        ms = jnp.mean(xl * xl, axis=-1, keepdims=True)
        o_ref[...] = xl * lax.rsqrt(ms + EPS) * fg_ref[...]
    else:
        hf_ref[...] = h


def _rec_scan(xr, cw, cb, w, bias, lam, h0, *, reverse, fused=None):
    n = xr.shape[0]
    tm = min(SCAN_TILE, n)
    nt = n // tm
    per = tm // SUBLANES
    nblk8 = n // SUBLANES
    pos = (lambda i: nt - 1 - i) if reverse else (lambda i: i)
    row = lambda i: (pos(i), 0)
    prev = lambda i: (jnp.maximum(pos(i) * per - 1, 0), 0)
    nxt = lambda i: (jnp.minimum((pos(i) + 1) * per, nblk8 - 1), 0)
    fixed = lambda i: (0, 0)
    vec = pl.BlockSpec((1, RNN_WIDTH), fixed)
    in_specs = [
        pl.BlockSpec((tm, RNN_WIDTH), row),
        pl.BlockSpec((SUBLANES, RNN_WIDTH), prev),
        pl.BlockSpec((SUBLANES, RNN_WIDTH), nxt),
        pl.BlockSpec((CONV_WIDTH, RNN_WIDTH), fixed),
        vec,
        pl.BlockSpec((2, RNN_WIDTH // 2, RNN_WIDTH), lambda i: (0, 0, 0)),
        pl.BlockSpec((1, 2 * RNN_WIDTH), fixed),
        vec,
        vec,
    ]
    args = [xr, xr, xr, cw, cb, w, bias, lam, h0]
    scratch = [pltpu.VMEM((tm + 2 * SUBLANES, RNN_WIDTH), jnp.float32),
               pltpu.VMEM((tm, RNN_WIDTH), jnp.float32),
               pltpu.VMEM((tm, RNN_WIDTH), jnp.float32)]
    if fused is not None:
        yr, gr, xl, gt, wo, fg = fused
        d = xl.shape[1]
        in_specs += [
            pl.BlockSpec((tm, RNN_WIDTH), row),
            pl.BlockSpec((tm, RNN_WIDTH), row),
            pl.BlockSpec((tm, d), row),
            pl.BlockSpec((1, d), fixed),
            pl.BlockSpec((RNN_WIDTH, d), fixed),
            pl.BlockSpec((1, d), fixed),
        ]
        args += [yr, gr, xl, gt, wo, fg]
        out_shape = jax.ShapeDtypeStruct((n, d), jnp.float32)
        out_specs = pl.BlockSpec((tm, d), row)
        scratch = scratch + [pltpu.VMEM((tm, RNN_WIDTH), jnp.float32)]
    else:
        out_shape = [jax.ShapeDtypeStruct((n, RNN_WIDTH), jnp.float32),
                     jax.ShapeDtypeStruct((1, RNN_WIDTH), jnp.float32)]
        out_specs = [pl.BlockSpec((tm, RNN_WIDTH), row), vec]
    scratch = scratch + [pltpu.VMEM((1, RNN_WIDTH), jnp.float32)]
    return pl.pallas_call(
        functools.partial(_rec_scan_kernel, reverse=reverse, fuse_out=fused is not None, nt=nt),
        out_shape=out_shape,
        grid=(nt,),
        in_specs=in_specs,
        out_specs=out_specs,
        scratch_shapes=scratch,
        compiler_params=_params("arbitrary"),
        name="rec_scan_out" if fused is not None else ("rec_scan_rev" if reverse else "rec_scan_fwd"),
    )(*args)


def _rope_tables(n):
    pos = jnp.arange(n, dtype=jnp.int32)
    row = (pos // GRID_W).astype(jnp.float32)
    col = (pos % GRID_W).astype(jnp.float32)
    n_freq = HEAD_DIM // 4
    inv_freq = ROPE_BASE ** (-jnp.arange(n_freq, dtype=jnp.float32) / n_freq)
    ar = row[:, None] * inv_freq
    ac = col[:, None] * inv_freq
    ang = jnp.concatenate([ar, ar, ac, ac], axis=-1)
    sign = jnp.where((jnp.arange(HEAD_DIM) % 32) < 16, -1.0, 1.0)
    cos = jnp.cos(ang)
    sin = jnp.sin(ang) * sign
    return jnp.tile(cos, (1, 2)), jnp.tile(sin, (1, 2))


def _block_diag(w):
    nb, bi, bj = w.shape
    eye = jnp.eye(nb, dtype=w.dtype)
    return jnp.einsum('hij,hg->higj', w, eye).reshape(nb * bi, nb * bj)


def kernel(x, c, ctx, c_ctx, norm_g, ada_w, ada_b, attn_w_in, attn_w_out, attn_sink, lam_q1, lam_k1,
           lam_q2, lam_k2, subln_g, rec_w_in, rec_conv_w, rec_conv_b, rec_wa, rec_ba, rec_wx, rec_bx,
           rec_lam, rec_w_out, final_g):
    assert x.shape[0] == 1 and c.shape[0] == 1 and ctx.shape[0] == 1
    bf = jnp.bfloat16
    d = D_MODEL
    xl = x[0]
    xc = ctx[0]
    n = xl.shape[0]

    cv = jnp.zeros((SUBLANES, d), jnp.float32).at[0].set(c[0]).at[1].set(c_ctx)
    mods = _adaln(cv, ada_w, ada_b)

    def mod(l, r):
        m = mods[l, r]
        return m[None, :d], m[None, d:2 * d], m[None, 2 * d:]

    lam_init = 0.8 - 0.6 * math.exp(-0.3 * 0)
    s1 = A_WIDTH
    s2 = s1 + A_KV_WIDTH
    s3 = s2 + A_KV_WIDTH
    s5 = s3 + 2 * B_WIDTH
    s6 = s5 + B_WIDTH
    head_perm = jnp.array([g * A_GROUP + j for j in range(A_GROUP) for g in range(A_KV_HEADS)])
    col_perm = (head_perm[:, None] * HEAD_DIM + jnp.arange(HEAD_DIM)).reshape(-1)
    w_in = attn_w_in[0]
    w_main = jnp.concatenate([w_in[:, :s1][:, col_perm], w_in[:, s1:s2], w_in[:, s3:s5],
                              w_in[:, s6:s6 + A_WIDTH][:, col_perm], w_in[:, s6 + A_WIDTH:]],
                             axis=1).astype(bf)
    w_vt = jnp.concatenate([w_in[:, s2:s3], w_in[:, s5:s6]], axis=1).T.astype(bf)
    w_out = attn_w_out[0]
    w_out = jnp.concatenate([w_out[:A_WIDTH][col_perm], w_out[A_WIDTH:]], axis=0).astype(bf)
    g0 = norm_g[0][None]
    cos, sin = _rope_tables(n)
    sh, sc, gt = mod(0, 0)
    shc, scc, gtc = mod(0, 1)
    qa, ka, qb, kb, vt, gate = _attn_in_proj(xl, g0, sc, sh, w_main, w_vt, cos, sin, rope=True)
    nc = xc.shape[0]
    qac, kac, qbc, kbc, vtc, gatec = _attn_in_proj(
        xc, g0, scc, shc, w_main, w_vt, cos[:nc], sin[:nc], rope=False)

    sink_gj = (attn_sink[0] * LOG2E).reshape(A_KV_HEADS, A_GROUP)
    kj = jnp.arange(3 * BLOCK)[:, None]
    qi = jnp.arange(BLOCK)[None, :]
    band_bias = jnp.where(jnp.abs(kj - BLOCK - qi) <= WINDOW, 0.0, NEG_INF).astype(jnp.float32)
    lamv = jnp.stack([lam_q1[0], lam_k1[0], lam_q2[0], lam_k2[0]])
    subln_col = subln_g[0][:, None]
    oa = _gqa_attention(jnp.repeat(sink_gj, BLOCK, axis=1), band_bias, qa, ka, vt, kac, vtc,
                        has_window=True)
    ob = _diff_attention(lamv, subln_col, qb, kb, vt, kbc, vtc, has_latent=True, lam_init=lam_init)
    oac = _gqa_attention(jnp.repeat(sink_gj, nc, axis=1), None, qac, None, None, kac, vtc,
                         has_window=False)
    obc = _diff_attention(lamv, subln_col, qbc, None, None, kbc, vtc, has_latent=False,
                          lam_init=lam_init)
    xl = _attn_out_proj(oa, ob, gate, xl, gt, w_out)
    xc = _attn_out_proj(oac, obc, gatec, xc, gtc, w_out)

    g1 = norm_g[1][None]
    sh, sc, gt = mod(1, 0)
    shc, scc, _ = mod(1, 1)
    w_rin = rec_w_in[0].astype(bf)
    xr, gr = _rec_in_proj(xl, g1, sc, sh, w_rin)
    xrc, _ = _rec_in_proj(xc, g1, scc, shc, w_rin)
    cw = 0.5 * rec_conv_w[0]
    cb = 0.5 * rec_conv_b[0][None]
    zero_h = jnp.zeros((1, RNN_WIDTH), jnp.float32)

    def coeff(dr):
        hb = RNN_BLOCKS // 2
        w = jnp.stack([jnp.concatenate([_block_diag(rec_wa[0, dr, i * hb:(i + 1) * hb]),
                                        _block_diag(rec_wx[0, dr, i * hb:(i + 1) * hb])], axis=1)
                       for i in range(2)]).astype(bf)
        bias = 0.5 * jnp.concatenate([rec_ba[0, dr], rec_bx[0, dr]])[None]
        return w, bias, rec_lam[0, dr][None]

    wf, bias_f, lam_f = coeff(0)
    wr, bias_r, lam_r = coeff(1)
    _, h0f = _rec_scan(xrc, cw, cb, wf, bias_f, lam_f, zero_h, reverse=False)
    _, h0r = _rec_scan(xrc, cw, cb, wr, bias_r, lam_r, zero_h, reverse=True)
    yr, _ = _rec_scan(xr, cw, cb, wr, bias_r, lam_r, h0r, reverse=True)
    out = _rec_scan(xr, cw, cb, wf, bias_f, lam_f, h0f, reverse=False,
                    fused=(yr, gr, xl, gt, rec_w_out[0].astype(bf), final_g[None]))
    return out[None]
```

```python
import functools
import math

import jax
import jax.numpy as jnp
from jax import lax
from jax.experimental import pallas as pl
from jax.experimental.pallas import tpu as pltpu

D_MODEL = 1024
GRID_W = 64
HEAD_DIM = 64
ROPE_BASE = 10000.0
EPS = 1e-6
NEG_INF = -1e30
A_Q_HEADS = 8
A_KV_HEADS = 2
A_GROUP = A_Q_HEADS // A_KV_HEADS
A_WIDTH = A_Q_HEADS * HEAD_DIM
A_KV_WIDTH = A_KV_HEADS * HEAD_DIM
WINDOW = 128
BLOCK = 128
B_HEADS = 4
B_WIDTH = B_HEADS * 2 * HEAD_DIM
ATTN_WIDTH = A_WIDTH + B_WIDTH
RNN_WIDTH = 1280
RNN_BLOCKS = 16
RNN_BLOCK_DIM = RNN_WIDTH // RNN_BLOCKS
CONV_WIDTH = 4
RG_C = 8.0
LOG2E = math.log2(math.e)
Q_SCALE_LOG2 = HEAD_DIM ** -0.5 * LOG2E

LANES = 128
SUBLANES = 8
VMEM_LIMIT = 48 * 1024 * 1024

ROW_TILE = 512
SCAN_TILE = 256
DIFF_TQ = 256
DIFF_TK = 512
DIFF_STAGES = 16
DIFF_JUMP_LIMIT = 64.0
DIFF_SUM_ROWS = 16

_NT = (((1,), (1,)), ((), ()))


def _silu(x):
    return x * jax.nn.sigmoid(x)


def _norm_mod(x, g, sc, sh):
    ms = jnp.mean(x * x, axis=-1, keepdims=True)
    return (x * lax.rsqrt(ms + EPS) * g) * (1.0 + sc) + sh


def _params(*sem, flags=None):
    return pltpu.CompilerParams(dimension_semantics=sem, vmem_limit_bytes=VMEM_LIMIT, flags=flags)


def _adaln_kernel(cv_ref, w_ref, b_ref, o_ref):
    s = _silu(cv_ref[...])
    o_ref[0] = jnp.dot(s, w_ref[0], preferred_element_type=jnp.float32) + b_ref[0]


def _adaln(cv, ada_w, ada_b):
    depth, d, n3 = ada_w.shape
    tn = 1024
    return pl.pallas_call(
        _adaln_kernel,
        out_shape=jax.ShapeDtypeStruct((depth, SUBLANES, n3), jnp.float32),
        grid=(depth, n3 // tn),
        in_specs=[
            pl.BlockSpec((SUBLANES, d), lambda l, j: (0, 0)),
            pl.BlockSpec((1, d, tn), lambda l, j: (l, 0, j)),
            pl.BlockSpec((1, 1, tn), lambda l, j: (l, 0, j)),
        ],
        out_specs=pl.BlockSpec((1, SUBLANES, tn), lambda l, j: (l, 0, j)),
        compiler_params=_params("parallel", "parallel"),
        name="adaln",
    )(cv, ada_w, ada_b.reshape(depth, 1, n3))


def _rope(x, cos, sin_signed, first_half):
    rot = jnp.where(first_half, pltpu.roll(x, LANES - 16, 1), pltpu.roll(x, 16, 1))
    return x * cos + rot * sin_signed


def _attn_in_kernel(x_ref, g_ref, sc_ref, sh_ref, w_ref, wvt_ref, cos_ref, sin_ref,
                    qa_ref, ka_ref, qb_ref, kb_ref, vt_ref, gate_ref, *, rope):
    h = _norm_mod(x_ref[...], g_ref[...], sc_ref[...], sh_ref[...]).astype(jnp.bfloat16)
    p = jnp.dot(h, w_ref[...], preferred_element_type=jnp.float32)
    vt_ref[...] = lax.dot_general(wvt_ref[...], h, _NT,
                                  preferred_element_type=jnp.float32).astype(jnp.bfloat16)
    if rope:
        cos = cos_ref[...]
        sin = sin_ref[...]
        lane = lax.broadcasted_iota(jnp.int32, cos.shape, 1)
        first_half = (lane % 32) < 16

    def chunk(c):
        x = p[:, c * LANES:(c + 1) * LANES]
        return _rope(x, cos, sin, first_half) if rope else x

    o0 = A_WIDTH // LANES
    o1 = o0 + A_KV_WIDTH // LANES
    o2 = o1 + B_WIDTH // LANES
    o3 = o2 + B_WIDTH // LANES
    for c in range(o0):
        qa_ref[:, c * LANES:(c + 1) * LANES] = (chunk(c) * Q_SCALE_LOG2).astype(jnp.bfloat16)
    for c in range(o0, o1):
        ka_ref[:, (c - o0) * LANES:(c - o0 + 1) * LANES] = chunk(c).astype(jnp.bfloat16)
    for c in range(o1, o2):
        qb_ref[:, (c - o1) * LANES:(c - o1 + 1) * LANES] = (chunk(c) * Q_SCALE_LOG2).astype(jnp.bfloat16)
    for c in range(o2, o3):
        kb_ref[:, (c - o2) * LANES:(c - o2 + 1) * LANES] = chunk(c).astype(jnp.bfloat16)
    gate_ref[...] = p[:, o3 * LANES:].astype(jnp.bfloat16)


def _attn_in_proj(x, g, sc, sh, w, wvt, cos, sin, *, rope):
    n, d = x.shape
    tm = min(ROW_TILE, n)
    nw = w.shape[1]
    nv = wvt.shape[0]
    row = lambda i: (i, 0)
    fixed = lambda i: (0, 0)
    bf = jnp.bfloat16
    return pl.pallas_call(
        functools.partial(_attn_in_kernel, rope=rope),
        out_shape=[
            jax.ShapeDtypeStruct((n, A_WIDTH), bf),
            jax.ShapeDtypeStruct((n, A_KV_WIDTH), bf),
            jax.ShapeDtypeStruct((n, B_WIDTH), bf),
            jax.ShapeDtypeStruct((n, B_WIDTH), bf),
            jax.ShapeDtypeStruct((nv, n), bf),
            jax.ShapeDtypeStruct((n, ATTN_WIDTH), bf),
        ],
        grid=(n // tm,),
        in_specs=[
            pl.BlockSpec((tm, d), row),
            pl.BlockSpec((1, d), fixed),
            pl.BlockSpec((1, d), fixed),
            pl.BlockSpec((1, d), fixed),
            pl.BlockSpec((d, nw), fixed),
            pl.BlockSpec((nv, d), fixed),
            pl.BlockSpec((tm, LANES), row),
            pl.BlockSpec((tm, LANES), row),
        ],
        out_specs=[
            pl.BlockSpec((tm, A_WIDTH), row),
            pl.BlockSpec((tm, A_KV_WIDTH), row),
            pl.BlockSpec((tm, B_WIDTH), row),
            pl.BlockSpec((tm, B_WIDTH), row),
            pl.BlockSpec((nv, tm), lambda i: (0, i)),
            pl.BlockSpec((tm, ATTN_WIDTH), row),
        ],
        compiler_params=_params("parallel"),
        name="attn_in_proj",
    )(x, g, sc, sh, w, wvt, cos, sin)


def _gqa_kernel(*refs, has_window, nb):
    if has_window:
        (sink_ref, bias_ref, q_ref, kp_ref, kc_ref, kn_ref, vp_ref, vc_ref, vn_ref,
         kx_ref, vxt_ref, o_ref) = refs
    else:
        sink_ref, q_ref, kx_ref, vxt_ref, o_ref = refs
    i = pl.program_id(0)
    q = q_ref[...]
    tq = q.shape[0]
    if has_window:
        keys = jnp.concatenate([kp_ref[...], kc_ref[...], kn_ref[...], kx_ref[...]], axis=0)
        vt = jnp.concatenate([vp_ref[...], vc_ref[...], vn_ref[...], vxt_ref[...]], axis=1)
        row = lax.broadcasted_iota(jnp.int32, (3 * BLOCK, 1), 0)
        off_end = ((row < BLOCK) & (i == 0)) | ((row >= 2 * BLOCK) & (i == nb - 1))
        bias = bias_ref[...] + jnp.where(off_end, NEG_INF, 0.0)
        bias = jnp.concatenate([bias] * A_GROUP, axis=1)
    else:
        keys = kx_ref[...]
        vt = vxt_ref[...]
    ones = jnp.ones((DIFF_SUM_ROWS, vt.shape[1]), vt.dtype)
    vt = jnp.concatenate([vt, ones], axis=0)
    lane = lax.broadcasted_iota(jnp.int32, (tq, LANES), 1)
    zero = jnp.zeros((tq, LANES), q.dtype)
    outs = [None] * A_Q_HEADS
    scores = []
    for g in range(A_KV_HEADS):
        in_g = (lane >= g * HEAD_DIM) & (lane < (g + 1) * HEAD_DIM)
        qg = jnp.concatenate([jnp.where(in_g, q[:, j * LANES:(j + 1) * LANES], zero)
                              for j in range(A_GROUP)], axis=0)
        scores.append(lax.dot_general(keys, qg, _NT, preferred_element_type=jnp.float32))
    for g in range(A_KV_HEADS):
        s = scores[g]
        if has_window:
            s = jnp.concatenate([s[:3 * BLOCK] + bias, s[3 * BLOCK:]], axis=0)
        sk = sink_ref[g:g + 1, :]
        m = jnp.maximum(jnp.max(s, axis=0, keepdims=True), sk)
        acc = jnp.dot(vt, jnp.exp2((s - m).astype(jnp.bfloat16)),
                      preferred_element_type=jnp.float32)
        l = acc[A_KV_WIDTH:A_KV_WIDTH + 1] + jnp.exp2(sk - m)
        o = acc[g * HEAD_DIM:(g + 1) * HEAD_DIM] / l
        for j in range(A_GROUP):
            outs[2 * j + g] = o[:, j * tq:(j + 1) * tq]
    o_ref[...] = jnp.concatenate(outs, axis=0).T.astype(o_ref.dtype)


def _gqa_attention(sink_cols, bias, q, k, vt, kx, vxt, *, has_window):
    n = q.shape[0]
    nx = kx.shape[0]
    fixed = lambda i: (0, 0)
    if has_window:
        nb = n // BLOCK
        tq = BLOCK
        prev = lambda i: (jnp.maximum(i - 1, 0), 0)
        cur = lambda i: (i, 0)
        nxt = lambda i: (jnp.minimum(i + 1, nb - 1), 0)
        k_specs = [pl.BlockSpec((BLOCK, A_KV_WIDTH), f) for f in (prev, cur, nxt)]
        vt_specs = [pl.BlockSpec((A_KV_WIDTH, BLOCK), lambda i, f=f: f(i)[::-1]) for f in (prev, cur, nxt)]
        in_specs = ([pl.BlockSpec((A_KV_HEADS, A_GROUP * tq), fixed),
                     pl.BlockSpec((3 * BLOCK, tq), fixed),
                     pl.BlockSpec((tq, A_WIDTH), cur)] + k_specs + vt_specs +
                    [pl.BlockSpec((nx, A_KV_WIDTH), fixed), pl.BlockSpec((A_KV_WIDTH, nx), fixed)])
        args = (sink_cols, bias, q, k, k, k, vt, vt, vt, kx, vxt)
    else:
        nb = 1
        tq = n
        in_specs = [pl.BlockSpec((A_KV_HEADS, A_GROUP * tq), fixed), pl.BlockSpec((tq, A_WIDTH), fixed),
                    pl.BlockSpec((nx, A_KV_WIDTH), fixed), pl.BlockSpec((A_KV_WIDTH, nx), fixed)]
        args = (sink_cols, q, kx, vxt)
    return pl.pallas_call(
        functools.partial(_gqa_kernel, has_window=has_window, nb=nb),
        out_shape=jax.ShapeDtypeStruct((n, A_WIDTH), jnp.bfloat16),
        grid=(nb,),
        in_specs=in_specs,
        out_specs=pl.BlockSpec((tq, A_WIDTH), lambda i: (i, 0)),
        compiler_params=_params("parallel"),
        name="gqa_window" if has_window else "gqa_ctx",
    )(*args)


def _diff_masked_q(q_ref):
    q = q_ref[...]
    lane = lax.broadcasted_iota(jnp.int32, q.shape, 1)
    zero = jnp.zeros_like(q)
    return jnp.concatenate([jnp.where(lane < HEAD_DIM, q, zero),
                            jnp.where(lane >= HEAD_DIM, q, zero)], axis=0)


def _diff_probs(s_rel):
    return jnp.exp2(s_rel.astype(jnp.bfloat16))


def _diff_pv(vt, p):
    ones = jnp.ones((DIFF_SUM_ROWS, vt.shape[1]), vt.dtype)
    return jnp.dot(jnp.concatenate([vt, ones], axis=0), p, preferred_element_type=jnp.float32)


def _diff_ctx_step(qz, kx, vxt):
    s = lax.dot_general(kx, qz, _NT, preferred_element_type=jnp.float32)
    m = jnp.max(s, axis=0, keepdims=True)
    return m, _diff_pv(vxt, _diff_probs(s - m))


def _diff_finalize(lam_ref, g_ref, acc, lam_init):
    hw = 2 * HEAD_DIM
    tq = acc.shape[1] // 2
    lv = lam_ref[...]
    lam = (jnp.exp(jnp.sum(lv[0:1] * lv[1:2], keepdims=True))
           - jnp.exp(jnp.sum(lv[2:3] * lv[3:4], keepdims=True)) + lam_init)
    on = acc[:hw] / acc[hw:hw + 1]
    o = on[:, :tq] - lam * on[:, tq:]
    ms = jnp.mean(o * o, axis=0, keepdims=True)
    o = (o * lax.rsqrt(ms + EPS)) * g_ref[...] * (1.0 - lam_init)
    return o.T.astype(jnp.bfloat16)


def _diff_ctx_kernel(lam_ref, g_ref, q_ref, kx_ref, vxt_ref, o_ref, *, lam_init):
    _, acc = _diff_ctx_step(_diff_masked_q(q_ref), kx_ref[...], vxt_ref[...])
    o_ref[...] = _diff_finalize(lam_ref, g_ref, acc, lam_init)


def _bf16_round(x):
    return x.astype(jnp.bfloat16).astype(jnp.float32)


def _diff_kernel(lam_ref, g_ref, q_ref, k_ref, vt_ref, kx_ref, vxt_ref, o_ref,
                 p0, p1, p2, p3, acc_ref, *, lam_init, n_chunks):
    tk = DIFF_TK
    qz = _diff_masked_q(q_ref)
    cols = qz.shape[0]
    qzt = qz.astype(jnp.float32).T.astype(jnp.bfloat16)
    ones_k = jnp.ones((tk, LANES), jnp.bfloat16)
    first_row = lax.broadcasted_iota(jnp.int32, (2 * SUBLANES, cols), 0) == 0
    zero_rows = jnp.zeros((LANES - 2 * SUBLANES, cols), jnp.bfloat16)
    p_buf = (p0, p1, p2, p3)

    def rhs_for(r):
        neg_r = jnp.where(first_row, -r, 0.0).astype(jnp.bfloat16)
        return jnp.concatenate([qzt, neg_r, zero_rows], axis=0)

    def rel_scores(c, rhs):
        k = k_ref[pl.ds(pl.multiple_of(c * tk, tk), tk), :]
        return jnp.dot(jnp.concatenate([k, ones_k], axis=1), rhs, preferred_element_type=jnp.float32)

    def pv(c, p_ref):
        return _diff_pv(vt_ref[:, pl.ds(pl.multiple_of(c * tk, tk), tk)], p_ref[...])

    def col_max(sa, sb):
        return jnp.maximum(jnp.max(sa, axis=0, keepdims=True), jnp.max(sb, axis=0, keepdims=True))

    s_ctx = lax.dot_general(kx_ref[...], qz, _NT, preferred_element_type=jnp.float32)
    m_ctx = jnp.max(s_ctx, axis=0, keepdims=True)
    r0 = _bf16_round(m_ctx)
    rhs0 = rhs_for(r0)
    sa, sb = rel_scores(0, rhs0), rel_scores(1, rhs0)
    acc_ref[...] = _diff_pv(vxt_ref[...], _diff_probs(s_ctx - r0))
    worst = col_max(sa, sb)
    p_buf[0][...] = _diff_probs(sa)
    p_buf[1][...] = _diff_probs(sb)

    def body(j, carry):
        run_max, pend, r_prev, r_cur, worst = carry
        for i in range(0, DIFF_STAGES, 2):
            c = DIFF_STAGES * j + i
            r_next = _bf16_round(run_max)
            rhs = rhs_for(r_next)
            sa = rel_scores(jnp.minimum(c + 2, n_chunks - 1), rhs)
            sb = rel_scores(jnp.minimum(c + 3, n_chunks - 1), rhs)
            acc_ref[...] = (jnp.exp2(r_prev - r_cur) * acc_ref[...] + pv(c, p_buf[i % 4])
                            + pv(c + 1, p_buf[(i + 1) % 4]))
            top = col_max(sa, sb)
            p_buf[(i + 2) % 4][...] = _diff_probs(sa)
            p_buf[(i + 3) % 4][...] = _diff_probs(sb)
            worst = jnp.maximum(worst, top)
            run_max = jnp.maximum(run_max, pend)
            pend = top + r_next
            r_prev, r_cur = r_cur, r_next
        return run_max, pend, r_prev, r_cur, worst

    carry = lax.fori_loop(0, n_chunks // DIFF_STAGES, body, (m_ctx, worst + r0, r0, r0, worst))
    o_ref[...] = _diff_finalize(lam_ref, g_ref, acc_ref[...], lam_init)

    @pl.when(jnp.max(carry[4]) > DIFF_JUMP_LIMIT)
    def _():
        m0, acc0 = _diff_ctx_step(qz, kx_ref[...], vxt_ref[...])

        def exact(c, carry):
            m, acc = carry
            off = pl.multiple_of(c * tk, tk)
            s = lax.dot_general(k_ref[pl.ds(off, tk), :], qz, _NT, preferred_element_type=jnp.float32)
            m_new = jnp.maximum(m, jnp.max(s, axis=0, keepdims=True))
            acc = jnp.exp2(m - m_new) * acc + _diff_pv(vt_ref[:, pl.ds(off, tk)], _diff_probs(s - m_new))
            return m_new, acc

        _, acc = lax.fori_loop(0, n_chunks, exact, (m0, acc0))
        o_ref[...] = _diff_finalize(lam_ref, g_ref, acc, lam_init)


def _diff_attention(lamv, subln_col, q, k, vt, kx, vxt, *, has_latent, lam_init):
    n = q.shape[0]
    nx = kx.shape[0]
    tq = min(DIFF_TQ, n)
    hw = 2 * HEAD_DIM
    fixed = lambda h, i: (0, 0)
    in_specs = [pl.BlockSpec((4, HEAD_DIM), fixed), pl.BlockSpec((hw, 1), fixed),
                pl.BlockSpec((tq, hw), lambda h, i: (i, h))]
    args = [lamv, subln_col, q]
    scratch = []
    if has_latent:
        nk = k.shape[0]
        n_chunks = nk // DIFF_TK
        assert DIFF_STAGES % 4 == 0 and n_chunks % DIFF_STAGES == 0 and n_chunks * DIFF_TK == nk
        in_specs += [pl.BlockSpec((nk, hw), lambda h, i: (0, h)),
                     pl.BlockSpec((hw, nk), lambda h, i: (h + 1, 0))]
        args += [k, vt]
        body = functools.partial(_diff_kernel, lam_init=lam_init, n_chunks=n_chunks)
        scratch = ([pltpu.VMEM((DIFF_TK, 2 * tq), jnp.bfloat16)] * 4 +
                   [pltpu.VMEM((hw + DIFF_SUM_ROWS, 2 * tq), jnp.float32)])
    else:
        body = functools.partial(_diff_ctx_kernel, lam_init=lam_init)
    in_specs += [pl.BlockSpec((nx, hw), lambda h, i: (0, h)),
                 pl.BlockSpec((hw, nx), lambda h, i: (h + 1, 0))]
    args += [kx, vxt]
    return pl.pallas_call(
        body,
        out_shape=jax.ShapeDtypeStruct((n, B_WIDTH), jnp.bfloat16),
        grid=(B_HEADS, n // tq),
        in_specs=in_specs,
        out_specs=pl.BlockSpec((tq, hw), lambda h, i: (i, h)),
        scratch_shapes=scratch,
        compiler_params=_params("parallel", "parallel"),
        name="diff_attn" if has_latent else "diff_attn_ctx",
    )(*args)


def _attn_out_kernel(oa_ref, ob_ref, gate_ref, x_ref, gt_ref, w_ref, o_ref):
    o = jnp.concatenate([oa_ref[...], ob_ref[...]], axis=-1).astype(jnp.float32)
    a = o * _silu(gate_ref[...].astype(jnp.float32))
    y = jnp.dot(a.astype(jnp.bfloat16), w_ref[...], preferred_element_type=jnp.float32)
    o_ref[...] = x_ref[...] + gt_ref[...] * y


def _attn_out_proj(oa, ob, gate, x, gt, w):
    n, d = x.shape
    tm = min(ROW_TILE, n)
    row = lambda i: (i, 0)
    fixed = lambda i: (0, 0)
    return pl.pallas_call(
        _attn_out_kernel,
        out_shape=jax.ShapeDtypeStruct((n, d), jnp.float32),
        grid=(n // tm,),
        in_specs=[
            pl.BlockSpec((tm, A_WIDTH), row),
            pl.BlockSpec((tm, B_WIDTH), row),
            pl.BlockSpec((tm, ATTN_WIDTH), row),
            pl.BlockSpec((tm, d), row),
            pl.BlockSpec((1, d), fixed),
            pl.BlockSpec((ATTN_WIDTH, d), fixed),
        ],
        out_specs=pl.BlockSpec((tm, d), row),
        compiler_params=_params("parallel"),
        name="attn_out_proj",
    )(oa, ob, gate, x, gt, w)


def _rec_in_kernel(x_ref, g_ref, sc_ref, sh_ref, w_ref, xr_ref, gr_ref):
    h = _norm_mod(x_ref[...], g_ref[...], sc_ref[...], sh_ref[...]).astype(jnp.bfloat16)
    p = jnp.dot(h, w_ref[...], preferred_element_type=jnp.float32)
    xr_ref[...] = p[:, :RNN_WIDTH]
    gr_ref[...] = p[:, RNN_WIDTH:]


def _rec_in_proj(x, g, sc, sh, w):
    n, d = x.shape
    tm = min(ROW_TILE, n)
    row = lambda i: (i, 0)
    fixed = lambda i: (0, 0)
    return pl.pallas_call(
        _rec_in_kernel,
        out_shape=[jax.ShapeDtypeStruct((n, RNN_WIDTH), jnp.float32)] * 2,
        grid=(n // tm,),
        in_specs=[
            pl.BlockSpec((tm, d), row),
            pl.BlockSpec((1, d), fixed),
            pl.BlockSpec((1, d), fixed),
            pl.BlockSpec((1, d), fixed),
            pl.BlockSpec((d, 2 * RNN_WIDTH), fixed),
        ],
        out_specs=[pl.BlockSpec((tm, RNN_WIDTH), row)] * 2,
        compiler_params=_params("parallel"),
        name="rec_in_proj",
    )(x, g, sc, sh, w)


def _rec_scan_kernel(*refs, reverse, fuse_out, nt):
    if fuse_out:
        (x_ref, xp_ref, xn_ref, cw_ref, cb_ref, w_ref, bias_ref, lam_ref, h0_ref,
         yr_ref, gr_ref, xl_ref, gt_ref, wo_ref, fg_ref,
         o_ref, xe_ref, a_ref, b_ref, y_ref, h_ref) = refs
    else:
        (x_ref, xp_ref, xn_ref, cw_ref, cb_ref, w_ref, bias_ref, lam_ref, h0_ref,
         y_ref, hf_ref, xe_ref, a_ref, b_ref, h_ref) = refs
    i = pl.program_id(0)
    t = i if not reverse else nt - 1 - i
    tm = x_ref.shape[0]

    @pl.when(i == 0)
    def _():
        h_ref[...] = h0_ref[...]

    xe_ref[0:SUBLANES, :] = jnp.where(t > 0, xp_ref[...], 0.0)
    xe_ref[SUBLANES:SUBLANES + tm, :] = x_ref[...]
    xe_ref[SUBLANES + tm:, :] = jnp.where(t < nt - 1, xn_ref[...], 0.0)
    cw = cw_ref[...]
    uh = cb_ref[...] + sum(cw[k:k + 1] * xe_ref[SUBLANES - 2 + k:SUBLANES - 2 + k + tm, :]
                           for k in range(CONV_WIDTH))
    ub = uh.astype(jnp.bfloat16)
    half = RNN_WIDTH // 2
    z0 = jnp.dot(ub[:, :half], w_ref[0], preferred_element_type=jnp.float32)
    z1 = jnp.dot(ub[:, half:], w_ref[1], preferred_element_type=jnp.float32)
    bias = bias_ref[...]
    ta = jnp.tanh(jnp.concatenate([z0[:, :half], z1[:, :half]], axis=1) + bias[:, :RNN_WIDTH])
    tx = jnp.tanh(jnp.concatenate([z0[:, half:], z1[:, half:]], axis=1) + bias[:, RNN_WIDTH:])
    nl = -lam_ref[...]
    softplus = jnp.maximum(nl, 0.0) + jnp.log1p(jnp.exp(-jnp.abs(nl)))
    coef = (-0.5 * RG_C * LOG2E) * softplus
    a = jnp.exp2(coef + coef * ta)
    a_ref[...] = a
    om = 1.0 - a * a
    b_ref[...] = (om * lax.rsqrt(jnp.maximum(om, 1e-30))) * ((1.0 + tx) * uh)

    def row(s, h):
        rr = (tm - 1 - s) if reverse else s
        h = a_ref[pl.ds(rr, 1), :] * h + b_ref[pl.ds(rr, 1), :]
        y_ref[pl.ds(rr, 1), :] = h
        return h

    h = lax.fori_loop(0, tm, row, h_ref[...], unroll=8)
    h_ref[...] = h

    if fuse_out:
        y = (y_ref[...] + yr_ref[...]) * _silu(gr_ref[...])
        out = jnp.dot(y.astype(jnp.bfloat16), wo_ref[...], preferred_element_type=jnp.float32)
        xl = xl_ref[...] + gt_ref[...] * out
        ms = jnp.mean(xl * xl, axis=-1, keepdims=True)
        o_ref[...] = xl * lax.rsqrt(ms + EPS) * fg_ref[...]
    else:
        hf_ref[...] = h


def _rec_scan(xr, cw, cb, w, bias, lam, h0, *, reverse, fused=None):
    n = xr.shape[0]
    tm = min(SCAN_TILE, n)
    nt = n // tm
    per = tm // SUBLANES
    nblk8 = n // SUBLANES
    pos = (lambda i: nt - 1 - i) if reverse else (lambda i: i)
    row = lambda i: (pos(i), 0)
    prev = lambda i: (jnp.maximum(pos(i) * per - 1, 0), 0)
    nxt = lambda i: (jnp.minimum((pos(i) + 1) * per, nblk8 - 1), 0)
    fixed = lambda i: (0, 0)
    vec = pl.BlockSpec((1, RNN_WIDTH), fixed)
    in_specs = [
        pl.BlockSpec((tm, RNN_WIDTH), row),
        pl.BlockSpec((SUBLANES, RNN_WIDTH), prev),
        pl.BlockSpec((SUBLANES, RNN_WIDTH), nxt),
        pl.BlockSpec((CONV_WIDTH, RNN_WIDTH), fixed),
        vec,
        pl.BlockSpec((2, RNN_WIDTH // 2, RNN_WIDTH), lambda i: (0, 0, 0)),
        pl.BlockSpec((1, 2 * RNN_WIDTH), fixed),
        vec,
        vec,
    ]
    args = [xr, xr, xr, cw, cb, w, bias, lam, h0]
    scratch = [pltpu.VMEM((tm + 2 * SUBLANES, RNN_WIDTH), jnp.float32),
               pltpu.VMEM((tm, RNN_WIDTH), jnp.float32),
               pltpu.VMEM((tm, RNN_WIDTH), jnp.float32)]
    if fused is not None:
        yr, gr, xl, gt, wo, fg = fused
        d = xl.shape[1]
        in_specs += [
            pl.BlockSpec((tm, RNN_WIDTH), row),
            pl.BlockSpec((tm, RNN_WIDTH), row),
            pl.BlockSpec((tm, d), row),
            pl.BlockSpec((1, d), fixed),
            pl.BlockSpec((RNN_WIDTH, d), fixed),
            pl.BlockSpec((1, d), fixed),
        ]
        args += [yr, gr, xl, gt, wo, fg]
        out_shape = jax.ShapeDtypeStruct((n, d), jnp.float32)
        out_specs = pl.BlockSpec((tm, d), row)
        scratch = scratch + [pltpu.VMEM((tm, RNN_WIDTH), jnp.float32)]
    else:
        out_shape = [jax.ShapeDtypeStruct((n, RNN_WIDTH), jnp.float32),
                     jax.ShapeDtypeStruct((1, RNN_WIDTH), jnp.float32)]
        out_specs = [pl.BlockSpec((tm, RNN_WIDTH), row), vec]
    scratch = scratch + [pltpu.VMEM((1, RNN_WIDTH), jnp.float32)]
    return pl.pallas_call(
        functools.partial(_rec_scan_kernel, reverse=reverse, fuse_out=fused is not None, nt=nt),
        out_shape=out_shape,
        grid=(nt,),
        in_specs=in_specs,
        out_specs=out_specs,
        scratch_shapes=scratch,
        compiler_params=_params("arbitrary"),
        name="rec_scan_out" if fused is not None else ("rec_scan_rev" if reverse else "rec_scan_fwd"),
    )(*args)


def _rope_tables(n):
    pos = jnp.arange(n, dtype=jnp.int32)
    row = (pos // GRID_W).astype(jnp.float32)
    col = (pos % GRID_W).astype(jnp.float32)
    n_freq = HEAD_DIM // 4
    inv_freq = ROPE_BASE ** (-jnp.arange(n_freq, dtype=jnp.float32) / n_freq)
    ar = row[:, None] * inv_freq
    ac = col[:, None] * inv_freq
    ang = jnp.concatenate([ar, ar, ac, ac], axis=-1)
    sign = jnp.where((jnp.arange(HEAD_DIM) % 32) < 16, -1.0, 1.0)
    cos = jnp.cos(ang)
    sin = jnp.sin(ang) * sign
    return jnp.tile(cos, (1, 2)), jnp.tile(sin, (1, 2))


def _block_diag(w):
    nb, bi, bj = w.shape
    eye = jnp.eye(nb, dtype=w.dtype)
    return jnp.einsum('hij,hg->higj', w, eye).reshape(nb * bi, nb * bj)


def kernel(x, c, ctx, c_ctx, norm_g, ada_w, ada_b, attn_w_in, attn_w_out, attn_sink, lam_q1, lam_k1,
           lam_q2, lam_k2, subln_g, rec_w_in, rec_conv_w, rec_conv_b, rec_wa, rec_ba, rec_wx, rec_bx,
           rec_lam, rec_w_out, final_g):
    assert x.shape[0] == 1 and c.shape[0] == 1 and ctx.shape[0] == 1
    bf = jnp.bfloat16
    d = D_MODEL
    xl = x[0]
    xc = ctx[0]
    n = xl.shape[0]

    cv = jnp.zeros((SUBLANES, d), jnp.float32).at[0].set(c[0]).at[1].set(c_ctx)
    mods = _adaln(cv, ada_w, ada_b)

    def mod(l, r):
        m = mods[l, r]
        return m[None, :d], m[None, d:2 * d], m[None, 2 * d:]

    lam_init = 0.8 - 0.6 * math.exp(-0.3 * 0)
    s1 = A_WIDTH
    s2 = s1 + A_KV_WIDTH
    s3 = s2 + A_KV_WIDTH
    s5 = s3 + 2 * B_WIDTH
    s6 = s5 + B_WIDTH
    head_perm = jnp.array([g * A_GROUP + j for j in range(A_GROUP) for g in range(A_KV_HEADS)])
    col_perm = (head_perm[:, None] * HEAD_DIM + jnp.arange(HEAD_DIM)).reshape(-1)
    w_in = attn_w_in[0]
    w_main = jnp.concatenate([w_in[:, :s1][:, col_perm], w_in[:, s1:s2], w_in[:, s3:s5],
                              w_in[:, s6:s6 + A_WIDTH][:, col_perm], w_in[:, s6 + A_WIDTH:]],
                             axis=1).astype(bf)
    w_vt = jnp.concatenate([w_in[:, s2:s3], w_in[:, s5:s6]], axis=1).T.astype(bf)
    w_out = attn_w_out[0]
    w_out = jnp.concatenate([w_out[:A_WIDTH][col_perm], w_out[A_WIDTH:]], axis=0).astype(bf)
    g0 = norm_g[0][None]
    cos, sin = _rope_tables(n)
    sh, sc, gt = mod(0, 0)
    shc, scc, gtc = mod(0, 1)
    qa, ka, qb, kb, vt, gate = _attn_in_proj(xl, g0, sc, sh, w_main, w_vt, cos, sin, rope=True)
    nc = xc.shape[0]
    qac, kac, qbc, kbc, vtc, gatec = _attn_in_proj(
        xc, g0, scc, shc, w_main, w_vt, cos[:nc], sin[:nc], rope=False)

    sink_gj = (attn_sink[0] * LOG2E).reshape(A_KV_HEADS, A_GROUP)
    kj = jnp.arange(3 * BLOCK)[:, None]
    qi = jnp.arange(BLOCK)[None, :]
    band_bias = jnp.where(jnp.abs(kj - BLOCK - qi) <= WINDOW, 0.0, NEG_INF).astype(jnp.float32)
    lamv = jnp.stack([lam_q1[0], lam_k1[0], lam_q2[0], lam_k2[0]])
    subln_col = subln_g[0][:, None]
    oa = _gqa_attention(jnp.repeat(sink_gj, BLOCK, axis=1), band_bias, qa, ka, vt, kac, vtc,
                        has_window=True)
    ob = _diff_attention(lamv, subln_col, qb, kb, vt, kbc, vtc, has_latent=True, lam_init=lam_init)
    oac = _gqa_attention(jnp.repeat(sink_gj, nc, axis=1), None, qac, None, None, kac, vtc,
                         has_window=False)
    obc = _diff_attention(lamv, subln_col, qbc, None, None, kbc, vtc, has_latent=False,
                          lam_init=lam_init)
    xl = _attn_out_proj(oa, ob, gate, xl, gt, w_out)
    xc = _attn_out_proj(oac, obc, gatec, xc, gtc, w_out)

    g1 = norm_g[1][None]
    sh, sc, gt = mod(1, 0)
    shc, scc, _ = mod(1, 1)
    w_rin = rec_w_in[0].astype(bf)
    xr, gr = _rec_in_proj(xl, g1, sc, sh, w_rin)
    xrc, _ = _rec_in_proj(xc, g1, scc, shc, w_rin)
    cw = 0.5 * rec_conv_w[0]
    cb = 0.5 * rec_conv_b[0][None]
    zero_h = jnp.zeros((1, RNN_WIDTH), jnp.float32)

    def coeff(dr):
        hb = RNN_BLOCKS // 2
        w = jnp.stack([jnp.concatenate([_block_diag(rec_wa[0, dr, i * hb:(i + 1) * hb]),
                                        _block_diag(rec_wx[0, dr, i * hb:(i + 1) * hb])], axis=1)
                       for i in range(2)]).astype(bf)
        bias = 0.5 * jnp.concatenate([rec_ba[0, dr], rec_bx[0, dr]])[None]
        return w, bias, rec_lam[0, dr][None]

    wf, bias_f, lam_f = coeff(0)
    wr, bias_r, lam_r = coeff(1)
    _, h0f = _rec_scan(xrc, cw, cb, wf, bias_f, lam_f, zero_h, reverse=False)
    _, h0r = _rec_scan(xrc, cw, cb, wr, bias_r, lam_r, zero_h, reverse=True)
    yr, _ = _rec_scan(xr, cw, cb, wr, bias_r, lam_r, h0r, reverse=True)
    out = _rec_scan(xr, cw, cb, wf, bias_f, lam_f, h0f, reverse=False,
                    fused=(yr, gr, xl, gt, rec_w_out[0].astype(bf), final_g[None]))
    return out[None]
```

```python
import functools
import math

import jax
import jax.numpy as jnp
from jax import lax
from jax.experimental import pallas as pl
from jax.experimental.pallas import tpu as pltpu

D_MODEL = 1024
GRID_W = 64
HEAD_DIM = 64
ROPE_BASE = 10000.0
EPS = 1e-6
NEG_INF = -1e30
A_Q_HEADS = 8
A_KV_HEADS = 2
A_GROUP = A_Q_HEADS // A_KV_HEADS
A_WIDTH = A_Q_HEADS * HEAD_DIM
A_KV_WIDTH = A_KV_HEADS * HEAD_DIM
WINDOW = 128
BLOCK = 128
B_HEADS = 4
B_WIDTH = B_HEADS * 2 * HEAD_DIM
ATTN_WIDTH = A_WIDTH + B_WIDTH
RNN_WIDTH = 1280
RNN_BLOCKS = 16
RNN_BLOCK_DIM = RNN_WIDTH // RNN_BLOCKS
CONV_WIDTH = 4
RG_C = 8.0
LOG2E = math.log2(math.e)
Q_SCALE_LOG2 = HEAD_DIM ** -0.5 * LOG2E

LANES = 128
SUBLANES = 8
VMEM_LIMIT = 48 * 1024 * 1024

ROW_TILE = 512
SCAN_TILE = 256
GQA_BLOCKS = 4
DIFF_TQ = 256
DIFF_TK = 512
DIFF_STAGES = 16
DIFF_JUMP_LIMIT = 64.0
DIFF_SUM_ROWS = 16

_NT = (((1,), (1,)), ((), ()))


def _silu(x):
    return x * jax.nn.sigmoid(x)


def _norm_mod(x, g, sc, sh):
    ms = jnp.mean(x * x, axis=-1, keepdims=True)
    return (x * lax.rsqrt(ms + EPS) * g) * (1.0 + sc) + sh


def _params(*sem, flags=None):
    return pltpu.CompilerParams(dimension_semantics=sem, vmem_limit_bytes=VMEM_LIMIT, flags=flags)


def _adaln_kernel(cv_ref, w_ref, b_ref, o_ref):
    s = _silu(cv_ref[...])
    o_ref[0] = jnp.dot(s, w_ref[0], preferred_element_type=jnp.float32) + b_ref[0]


def _adaln(cv, ada_w, ada_b):
    depth, d, n3 = ada_w.shape
    tn = 1024
    return pl.pallas_call(
        _adaln_kernel,
        out_shape=jax.ShapeDtypeStruct((depth, SUBLANES, n3), jnp.float32),
        grid=(depth, n3 // tn),
        in_specs=[
            pl.BlockSpec((SUBLANES, d), lambda l, j: (0, 0)),
            pl.BlockSpec((1, d, tn), lambda l, j: (l, 0, j)),
            pl.BlockSpec((1, 1, tn), lambda l, j: (l, 0, j)),
        ],
        out_specs=pl.BlockSpec((1, SUBLANES, tn), lambda l, j: (l, 0, j)),
        compiler_params=_params("parallel", "parallel"),
        name="adaln",
    )(cv, ada_w, ada_b.reshape(depth, 1, n3))


def _rope(x, cos, sin_signed, first_half):
    rot = jnp.where(first_half, pltpu.roll(x, LANES - 16, 1), pltpu.roll(x, 16, 1))
    return x * cos + rot * sin_signed


def _attn_in_kernel(x_ref, g_ref, sc_ref, sh_ref, w_ref, wvt_ref, cos_ref, sin_ref,
                    qa_ref, ka_ref, qb_ref, kb_ref, vt_ref, gate_ref, *, rope):
    h = _norm_mod(x_ref[...], g_ref[...], sc_ref[...], sh_ref[...]).astype(jnp.bfloat16)
    p = jnp.dot(h, w_ref[...], preferred_element_type=jnp.float32)
    vt_ref[...] = lax.dot_general(wvt_ref[...], h, _NT,
                                  preferred_element_type=jnp.float32).astype(jnp.bfloat16)
    if rope:
        cos = cos_ref[...]
        sin = sin_ref[...]
        lane = lax.broadcasted_iota(jnp.int32, cos.shape, 1)
        first_half = (lane % 32) < 16

    def chunk(c):
        x = p[:, c * LANES:(c + 1) * LANES]
        return _rope(x, cos, sin, first_half) if rope else x

    o0 = A_WIDTH // LANES
    o1 = o0 + A_KV_WIDTH // LANES
    o2 = o1 + B_WIDTH // LANES
    o3 = o2 + B_WIDTH // LANES
    for c in range(o0):
        qa_ref[:, c * LANES:(c + 1) * LANES] = (chunk(c) * Q_SCALE_LOG2).astype(jnp.bfloat16)
    for c in range(o0, o1):
        ka_ref[:, (c - o0) * LANES:(c - o0 + 1) * LANES] = chunk(c).astype(jnp.bfloat16)
    for c in range(o1, o2):
        qb_ref[:, (c - o1) * LANES:(c - o1 + 1) * LANES] = (chunk(c) * Q_SCALE_LOG2).astype(jnp.bfloat16)
    for c in range(o2, o3):
        kb_ref[:, (c - o2) * LANES:(c - o2 + 1) * LANES] = chunk(c).astype(jnp.bfloat16)
    gate_ref[...] = p[:, o3 * LANES:].astype(jnp.bfloat16)


def _attn_in_proj(x, g, sc, sh, w, wvt, cos, sin, *, rope):
    n, d = x.shape
    tm = min(ROW_TILE, n)
    nw = w.shape[1]
    nv = wvt.shape[0]
    row = lambda i: (i, 0)
    fixed = lambda i: (0, 0)
    bf = jnp.bfloat16
    return pl.pallas_call(
        functools.partial(_attn_in_kernel, rope=rope),
        out_shape=[
            jax.ShapeDtypeStruct((n, A_WIDTH), bf),
            jax.ShapeDtypeStruct((n, A_KV_WIDTH), bf),
            jax.ShapeDtypeStruct((n, B_WIDTH), bf),
            jax.ShapeDtypeStruct((n, B_WIDTH), bf),
            jax.ShapeDtypeStruct((nv, n), bf),
            jax.ShapeDtypeStruct((n, ATTN_WIDTH), bf),
        ],
        grid=(n // tm,),
        in_specs=[
            pl.BlockSpec((tm, d), row),
            pl.BlockSpec((1, d), fixed),
            pl.BlockSpec((1, d), fixed),
            pl.BlockSpec((1, d), fixed),
            pl.BlockSpec((d, nw), fixed),
            pl.BlockSpec((nv, d), fixed),
            pl.BlockSpec((tm, LANES), row),
            pl.BlockSpec((tm, LANES), row),
        ],
        out_specs=[
            pl.BlockSpec((tm, A_WIDTH), row),
            pl.BlockSpec((tm, A_KV_WIDTH), row),
            pl.BlockSpec((tm, B_WIDTH), row),
            pl.BlockSpec((tm, B_WIDTH), row),
            pl.BlockSpec((nv, tm), lambda i: (0, i)),
            pl.BlockSpec((tm, ATTN_WIDTH), row),
        ],
        compiler_params=_params("parallel"),
        name="attn_in_proj",
    )(x, g, sc, sh, w, wvt, cos, sin)


def _gqa_scores(q, keys):
    tq = q.shape[0]
    lane = lax.broadcasted_iota(jnp.int32, (tq, LANES), 1)
    zero = jnp.zeros((tq, LANES), q.dtype)
    out = []
    for g in range(A_KV_HEADS):
        in_g = (lane >= g * HEAD_DIM) & (lane < (g + 1) * HEAD_DIM)
        qg = jnp.concatenate([jnp.where(in_g, q[:, j * LANES:(j + 1) * LANES], zero)
                              for j in range(A_GROUP)], axis=0)
        out.append(lax.dot_general(keys, qg, _NT, preferred_element_type=jnp.float32))
    return out


def _gqa_outputs(scores, vt, bias, sink_ref, tq):
    outs = [None] * A_Q_HEADS
    for g in range(A_KV_HEADS):
        s = scores[g]
        if bias is not None:
            s = jnp.concatenate([s[:3 * BLOCK] + bias, s[3 * BLOCK:]], axis=0)
        sk = sink_ref[g:g + 1, :]
        m = jnp.maximum(jnp.max(s, axis=0, keepdims=True), sk)
        acc = jnp.dot(vt, jnp.exp2((s - m).astype(jnp.bfloat16)),
                      preferred_element_type=jnp.float32)
        l = acc[A_KV_WIDTH:A_KV_WIDTH + 1] + jnp.exp2(sk - m)
        o = acc[g * HEAD_DIM:(g + 1) * HEAD_DIM] / l
        for j in range(A_GROUP):
            outs[2 * j + g] = o[:, j * tq:(j + 1) * tq]
    return jnp.concatenate(outs, axis=0).T


def _gqa_window_kernel(sink_ref, bias_ref, q_ref, *refs, nb):
    nk = GQA_BLOCKS + 2
    k_refs, vt_refs = refs[:nk], refs[nk:2 * nk]
    kx_ref, vxt_ref, o_ref = refs[2 * nk:]
    i = pl.program_id(0)
    ones = jnp.ones((DIFF_SUM_ROWS, 3 * BLOCK + kx_ref.shape[0]), jnp.bfloat16)
    row = lax.broadcasted_iota(jnp.int32, (3 * BLOCK, 1), 0)
    all_scores, vts, biases = [], [], []
    for t in range(GQA_BLOCKS):
        keys = jnp.concatenate([r[...] for r in k_refs[t:t + 3]] + [kx_ref[...]], axis=0)
        vt = jnp.concatenate([r[...] for r in vt_refs[t:t + 3]] + [vxt_ref[...]], axis=1)
        vts.append(jnp.concatenate([vt, ones], axis=0))
        all_scores.append(_gqa_scores(q_ref[t * BLOCK:(t + 1) * BLOCK, :], keys))
        b = GQA_BLOCKS * i + t
        off_end = ((row < BLOCK) & (b == 0)) | ((row >= 2 * BLOCK) & (b == nb - 1))
        bias = bias_ref[...] + jnp.where(off_end, NEG_INF, 0.0)
        biases.append(jnp.concatenate([bias] * A_GROUP, axis=1))
    for t in range(GQA_BLOCKS):
        o = _gqa_outputs(all_scores[t], vts[t], biases[t], sink_ref, BLOCK)
        o_ref[t * BLOCK:(t + 1) * BLOCK, :] = o.astype(o_ref.dtype)


def _gqa_ctx_kernel(sink_ref, q_ref, kx_ref, vxt_ref, o_ref):
    vt = vxt_ref[...]
    vt = jnp.concatenate([vt, jnp.ones((DIFF_SUM_ROWS, vt.shape[1]), vt.dtype)], axis=0)
    q = q_ref[...]
    o = _gqa_outputs(_gqa_scores(q, kx_ref[...]), vt, None, sink_ref, q.shape[0])
    o_ref[...] = o.astype(o_ref.dtype)


def _gqa_attention(sink_cols, bias, q, k, vt, kx, vxt, *, has_window):
    n = q.shape[0]
    nx = kx.shape[0]
    fixed = lambda i: (0, 0)
    if has_window:
        nb = n // BLOCK
        tq = GQA_BLOCKS * BLOCK
        steps = nb // GQA_BLOCKS
        assert steps * GQA_BLOCKS == nb
        offs = range(-1, GQA_BLOCKS + 1)
        blk = lambda i, o: jnp.clip(GQA_BLOCKS * i + o, 0, nb - 1)
        k_specs = [pl.BlockSpec((BLOCK, A_KV_WIDTH), lambda i, o=o: (blk(i, o), 0)) for o in offs]
        vt_specs = [pl.BlockSpec((A_KV_WIDTH, BLOCK), lambda i, o=o: (0, blk(i, o))) for o in offs]
        in_specs = ([pl.BlockSpec((A_KV_HEADS, A_GROUP * BLOCK), fixed),
                     pl.BlockSpec((3 * BLOCK, BLOCK), fixed),
                     pl.BlockSpec((tq, A_WIDTH), lambda i: (i, 0))] + k_specs + vt_specs +
                    [pl.BlockSpec((nx, A_KV_WIDTH), fixed), pl.BlockSpec((A_KV_WIDTH, nx), fixed)])
        args = (sink_cols, bias, q) + (k,) * len(offs) + (vt,) * len(offs) + (kx, vxt)
        body = functools.partial(_gqa_window_kernel, nb=nb)
    else:
        steps = 1
        tq = n
        in_specs = [pl.BlockSpec((A_KV_HEADS, A_GROUP * tq), fixed), pl.BlockSpec((tq, A_WIDTH), fixed),
                    pl.BlockSpec((nx, A_KV_WIDTH), fixed), pl.BlockSpec((A_KV_WIDTH, nx), fixed)]
        args = (sink_cols, q, kx, vxt)
        body = _gqa_ctx_kernel
    return pl.pallas_call(
        body,
        out_shape=jax.ShapeDtypeStruct((n, A_WIDTH), jnp.bfloat16),
        grid=(steps,),
        in_specs=in_specs,
        out_specs=pl.BlockSpec((tq, A_WIDTH), lambda i: (i, 0)),
        compiler_params=_params("parallel"),
        name="gqa_window" if has_window else "gqa_ctx",
    )(*args)


def _diff_masked_q(q_ref):
    q = q_ref[...]
    lane = lax.broadcasted_iota(jnp.int32, q.shape, 1)
    zero = jnp.zeros_like(q)
    return jnp.concatenate([jnp.where(lane < HEAD_DIM, q, zero),
                            jnp.where(lane >= HEAD_DIM, q, zero)], axis=0)


def _diff_probs(s_rel):
    return jnp.exp2(s_rel.astype(jnp.bfloat16))


def _diff_pv(vt, p):
    ones = jnp.ones((DIFF_SUM_ROWS, vt.shape[1]), vt.dtype)
    return jnp.dot(jnp.concatenate([vt, ones], axis=0), p, preferred_element_type=jnp.float32)


def _diff_ctx_step(qz, kx, vxt):
    s = lax.dot_general(kx, qz, _NT, preferred_element_type=jnp.float32)
    m = jnp.max(s, axis=0, keepdims=True)
    return m, _diff_pv(vxt, _diff_probs(s - m))


def _diff_finalize(lam_ref, g_ref, acc, lam_init):
    hw = 2 * HEAD_DIM
    tq = acc.shape[1] // 2
    lv = lam_ref[...]
    lam = (jnp.exp(jnp.sum(lv[0:1] * lv[1:2], keepdims=True))
           - jnp.exp(jnp.sum(lv[2:3] * lv[3:4], keepdims=True)) + lam_init)
    on = acc[:hw] / acc[hw:hw + 1]
    o = on[:, :tq] - lam * on[:, tq:]
    ms = jnp.mean(o * o, axis=0, keepdims=True)
    o = (o * lax.rsqrt(ms + EPS)) * g_ref[...] * (1.0 - lam_init)
    return o.T.astype(jnp.bfloat16)


def _diff_ctx_kernel(lam_ref, g_ref, q_ref, kx_ref, vxt_ref, o_ref, *, lam_init):
    _, acc = _diff_ctx_step(_diff_masked_q(q_ref), kx_ref[...], vxt_ref[...])
    o_ref[...] = _diff_finalize(lam_ref, g_ref, acc, lam_init)


def _bf16_round(x):
    return x.astype(jnp.bfloat16).astype(jnp.float32)


def _diff_kernel(lam_ref, g_ref, q_ref, k_ref, vt_ref, kx_ref, vxt_ref, o_ref,
                 p0, p1, p2, p3, acc_ref, *, lam_init, n_chunks):
    tk = DIFF_TK
    qz = _diff_masked_q(q_ref)
    cols = qz.shape[0]
    qzt = qz.astype(jnp.float32).T.astype(jnp.bfloat16)
    ones_k = jnp.ones((tk, LANES), jnp.bfloat16)
    first_row = lax.broadcasted_iota(jnp.int32, (2 * SUBLANES, cols), 0) == 0
    zero_rows = jnp.zeros((LANES - 2 * SUBLANES, cols), jnp.bfloat16)
    p_buf = (p0, p1, p2, p3)

    def rhs_for(r):
        neg_r = jnp.where(first_row, -r, 0.0).astype(jnp.bfloat16)
        return jnp.concatenate([qzt, neg_r, zero_rows], axis=0)

    def rel_scores(c, rhs):
        k = k_ref[pl.ds(pl.multiple_of(c * tk, tk), tk), :]
        return jnp.dot(jnp.concatenate([k, ones_k], axis=1), rhs, preferred_element_type=jnp.float32)

    def pv(c, p_ref):
        return _diff_pv(vt_ref[:, pl.ds(pl.multiple_of(c * tk, tk), tk)], p_ref[...])

    def col_max(sa, sb):
        return jnp.maximum(jnp.max(sa, axis=0, keepdims=True), jnp.max(sb, axis=0, keepdims=True))

    s_ctx = lax.dot_general(kx_ref[...], qz, _NT, preferred_element_type=jnp.float32)
    m_ctx = jnp.max(s_ctx, axis=0, keepdims=True)
    r0 = _bf16_round(m_ctx)
    rhs0 = rhs_for(r0)
    sa, sb = rel_scores(0, rhs0), rel_scores(1, rhs0)
    acc_ref[...] = _diff_pv(vxt_ref[...], _diff_probs(s_ctx - r0))
    worst = col_max(sa, sb)
    p_buf[0][...] = _diff_probs(sa)
    p_buf[1][...] = _diff_probs(sb)

    def body(j, carry):
        run_max, pend, r_prev, r_cur, worst = carry
        for i in range(0, DIFF_STAGES, 2):
            c = DIFF_STAGES * j + i
            r_next = _bf16_round(run_max)
            rhs = rhs_for(r_next)
            pva = pv(c, p_buf[i % 4])
            sa = rel_scores(jnp.minimum(c + 2, n_chunks - 1), rhs)
            pvb = pv(c + 1, p_buf[(i + 1) % 4])
            sb = rel_scores(jnp.minimum(c + 3, n_chunks - 1), rhs)
            acc_ref[...] = jnp.exp2(r_prev - r_cur) * acc_ref[...] + pva + pvb
            top = col_max(sa, sb)
            p_buf[(i + 2) % 4][...] = _diff_probs(sa)
            p_buf[(i + 3) % 4][...] = _diff_probs(sb)
            worst = jnp.maximum(worst, top)
            run_max = jnp.maximum(run_max, pend)
            pend = top + r_next
            r_prev, r_cur = r_cur, r_next
        return run_max, pend, r_prev, r_cur, worst

    carry = lax.fori_loop(0, n_chunks // DIFF_STAGES, body, (m_ctx, worst + r0, r0, r0, worst))
    o_ref[...] = _diff_finalize(lam_ref, g_ref, acc_ref[...], lam_init)

    @pl.when(jnp.max(carry[4]) > DIFF_JUMP_LIMIT)
    def _():
        m0, acc0 = _diff_ctx_step(qz, kx_ref[...], vxt_ref[...])

        def exact(c, carry):
            m, acc = carry
            off = pl.multiple_of(c * tk, tk)
            s = lax.dot_general(k_ref[pl.ds(off, tk), :], qz, _NT, preferred_element_type=jnp.float32)
            m_new = jnp.maximum(m, jnp.max(s, axis=0, keepdims=True))
            acc = jnp.exp2(m - m_new) * acc + _diff_pv(vt_ref[:, pl.ds(off, tk)], _diff_probs(s - m_new))
            return m_new, acc

        _, acc = lax.fori_loop(0, n_chunks, exact, (m0, acc0))
        o_ref[...] = _diff_finalize(lam_ref, g_ref, acc, lam_init)


def _diff_attention(lamv, subln_col, q, k, vt, kx, vxt, *, has_latent, lam_init):
    n = q.shape[0]
    nx = kx.shape[0]
    tq = min(DIFF_TQ, n)
    hw = 2 * HEAD_DIM
    fixed = lambda h, i: (0, 0)
    in_specs = [pl.BlockSpec((4, HEAD_DIM), fixed), pl.BlockSpec((hw, 1), fixed),
                pl.BlockSpec((tq, hw), lambda h, i: (i, h))]
    args = [lamv, subln_col, q]
    scratch = []
    if has_latent:
        nk = k.shape[0]
        n_chunks = nk // DIFF_TK
        assert DIFF_STAGES % 4 == 0 and n_chunks % DIFF_STAGES == 0 and n_chunks * DIFF_TK == nk
        in_specs += [pl.BlockSpec((nk, hw), lambda h, i: (0, h)),
                     pl.BlockSpec((hw, nk), lambda h, i: (h + 1, 0))]
        args += [k, vt]
        body = functools.partial(_diff_kernel, lam_init=lam_init, n_chunks=n_chunks)
        scratch = ([pltpu.VMEM((DIFF_TK, 2 * tq), jnp.bfloat16)] * 4 +
                   [pltpu.VMEM((hw + DIFF_SUM_ROWS, 2 * tq), jnp.float32)])
    else:
        body = functools.partial(_diff_ctx_kernel, lam_init=lam_init)
    in_specs += [pl.BlockSpec((nx, hw), lambda h, i: (0, h)),
                 pl.BlockSpec((hw, nx), lambda h, i: (h + 1, 0))]
    args += [kx, vxt]
    return pl.pallas_call(
        body,
        out_shape=jax.ShapeDtypeStruct((n, B_WIDTH), jnp.bfloat16),
        grid=(B_HEADS, n // tq),
        in_specs=in_specs,
        out_specs=pl.BlockSpec((tq, hw), lambda h, i: (i, h)),
        scratch_shapes=scratch,
        compiler_params=_params("parallel", "parallel"),
        name="diff_attn" if has_latent else "diff_attn_ctx",
    )(*args)


def _attn_out_kernel(oa_ref, ob_ref, gate_ref, x_ref, gt_ref, w_ref, o_ref):
    o = jnp.concatenate([oa_ref[...], ob_ref[...]], axis=-1).astype(jnp.float32)
    a = o * _silu(gate_ref[...].astype(jnp.float32))
    y = jnp.dot(a.astype(jnp.bfloat16), w_ref[...], preferred_element_type=jnp.float32)
    o_ref[...] = x_ref[...] + gt_ref[...] * y


def _attn_out_proj(oa, ob, gate, x, gt, w):
    n, d = x.shape
    tm = min(ROW_TILE, n)
    row = lambda i: (i, 0)
    fixed = lambda i: (0, 0)
    return pl.pallas_call(
        _attn_out_kernel,
        out_shape=jax.ShapeDtypeStruct((n, d), jnp.float32),
        grid=(n // tm,),
        in_specs=[
            pl.BlockSpec((tm, A_WIDTH), row),
            pl.BlockSpec((tm, B_WIDTH), row),
            pl.BlockSpec((tm, ATTN_WIDTH), row),
            pl.BlockSpec((tm, d), row),
            pl.BlockSpec((1, d), fixed),
            pl.BlockSpec((ATTN_WIDTH, d), fixed),
        ],
        out_specs=pl.BlockSpec((tm, d), row),
        compiler_params=_params("parallel"),
        name="attn_out_proj",
    )(oa, ob, gate, x, gt, w)


def _rec_in_kernel(x_ref, g_ref, sc_ref, sh_ref, w_ref, xr_ref, gr_ref):
    h = _norm_mod(x_ref[...], g_ref[...], sc_ref[...], sh_ref[...]).astype(jnp.bfloat16)
    p = jnp.dot(h, w_ref[...], preferred_element_type=jnp.float32)
    xr_ref[...] = p[:, :RNN_WIDTH]
    gr_ref[...] = p[:, RNN_WIDTH:]


def _rec_in_proj(x, g, sc, sh, w):
    n, d = x.shape
    tm = min(ROW_TILE, n)
    row = lambda i: (i, 0)
    fixed = lambda i: (0, 0)
    return pl.pallas_call(
        _rec_in_kernel,
        out_shape=[jax.ShapeDtypeStruct((n, RNN_WIDTH), jnp.float32)] * 2,
        grid=(n // tm,),
        in_specs=[
            pl.BlockSpec((tm, d), row),
            pl.BlockSpec((1, d), fixed),
            pl.BlockSpec((1, d), fixed),
            pl.BlockSpec((1, d), fixed),
            pl.BlockSpec((d, 2 * RNN_WIDTH), fixed),
        ],
        out_specs=[pl.BlockSpec((tm, RNN_WIDTH), row)] * 2,
        compiler_params=_params("parallel"),
        name="rec_in_proj",
    )(x, g, sc, sh, w)


def _rec_scan_kernel(*refs, reverse, fuse_out, nt):
    if fuse_out:
        (x_ref, xp_ref, xn_ref, cw_ref, cb_ref, w_ref, bias_ref, lam_ref, h0_ref,
         yr_ref, gr_ref, xl_ref, gt_ref, wo_ref, fg_ref,
         o_ref, xe_ref, a_ref, b_ref, y_ref, h_ref) = refs
    else:
        (x_ref, xp_ref, xn_ref, cw_ref, cb_ref, w_ref, bias_ref, lam_ref, h0_ref,
         y_ref, hf_ref, xe_ref, a_ref, b_ref, h_ref) = refs
    i = pl.program_id(0)
    t = i if not reverse else nt - 1 - i
    tm = x_ref.shape[0]

    @pl.when(i == 0)
    def _():
        h_ref[...] = h0_ref[...]

    xe_ref[0:SUBLANES, :] = jnp.where(t > 0, xp_ref[...], 0.0)
    xe_ref[SUBLANES:SUBLANES + tm, :] = x_ref[...]
    xe_ref[SUBLANES + tm:, :] = jnp.where(t < nt - 1, xn_ref[...], 0.0)
    cw = cw_ref[...]
    xe = xe_ref[...]
    t = pltpu.roll(xe * cw[0:1], 1, 0)
    t = pltpu.roll(t + xe * cw[1:2], 1, 0)
    t = t + xe * cw[2:3] + pltpu.roll(xe * cw[3:4], tm + 2 * SUBLANES - 1, 0)
    uh = t[SUBLANES:SUBLANES + tm] + cb_ref[...]
    ub = uh.astype(jnp.bfloat16)
    half = RNN_WIDTH // 2
    z0 = jnp.dot(ub[:, :half], w_ref[0], preferred_element_type=jnp.float32)
    z1 = jnp.dot(ub[:, half:], w_ref[1], preferred_element_type=jnp.float32)
    bias = bias_ref[...]
    ta = jnp.tanh(jnp.concatenate([z0[:, :half], z1[:, :half]], axis=1) + bias[:, :RNN_WIDTH])
    tx = jnp.tanh(jnp.concatenate([z0[:, half:], z1[:, half:]], axis=1) + bias[:, RNN_WIDTH:])
    nl = -lam_ref[...]
    softplus = jnp.maximum(nl, 0.0) + jnp.log1p(jnp.exp(-jnp.abs(nl)))
    coef = (-0.5 * RG_C * LOG2E) * softplus
    a = jnp.exp2(coef + coef * ta)
    a_ref[...] = a
    om = 1.0 - a * a
    b_ref[...] = (om * lax.rsqrt(jnp.maximum(om, 1e-30))) * ((1.0 + tx) * uh)

    def row(s, h):
        rr = (tm - 1 - s) if reverse else s
        h = a_ref[pl.ds(rr, 1), :] * h + b_ref[pl.ds(rr, 1), :]
        y_ref[pl.ds(rr, 1), :] = h
        return h

    h = lax.fori_loop(0, tm, row, h_ref[...], unroll=8)
    h_ref[...] = h

    if fuse_out:
        y = (y_ref[...] + yr_ref[...]) * _silu(gr_ref[...])
        out = jnp.dot(y.astype(jnp.bfloat16), wo_ref[...], preferred_element_type=jnp.float32)
        xl = xl_ref[...] + gt_ref[...] * out
        ms = jnp.mean(xl * xl, axis=-1, keepdims=True)
        o_ref[...] = xl * lax.rsqrt(ms + EPS) * fg_ref[...]
    else:
        hf_ref[...] = h


def _rec_scan(xr, cw, cb, w, bias, lam, h0, *, reverse, fused=None):
    n = xr.shape[0]
    tm = min(SCAN_TILE, n)
    nt = n // tm
    per = tm // SUBLANES
    nblk8 = n // SUBLANES
    pos = (lambda i: nt - 1 - i) if reverse else (lambda i: i)
    row = lambda i: (pos(i), 0)
    prev = lambda i: (jnp.maximum(pos(i) * per - 1, 0), 0)
    nxt = lambda i: (jnp.minimum((pos(i) + 1) * per, nblk8 - 1), 0)
    fixed = lambda i: (0, 0)
    vec = pl.BlockSpec((1, RNN_WIDTH), fixed)
    in_specs = [
        pl.BlockSpec((tm, RNN_WIDTH), row),
        pl.BlockSpec((SUBLANES, RNN_WIDTH), prev),
        pl.BlockSpec((SUBLANES, RNN_WIDTH), nxt),
        pl.BlockSpec((CONV_WIDTH, RNN_WIDTH), fixed),
        vec,
        pl.BlockSpec((2, RNN_WIDTH // 2, RNN_WIDTH), lambda i: (0, 0, 0)),
        pl.BlockSpec((1, 2 * RNN_WIDTH), fixed),
        vec,
        vec,
    ]
    args = [xr, xr, xr, cw, cb, w, bias, lam, h0]
    scratch = [pltpu.VMEM((tm + 2 * SUBLANES, RNN_WIDTH), jnp.float32),
               pltpu.VMEM((tm, RNN_WIDTH), jnp.float32),
               pltpu.VMEM((tm, RNN_WIDTH), jnp.float32)]
    if fused is not None:
        yr, gr, xl, gt, wo, fg = fused
        d = xl.shape[1]
        in_specs += [
            pl.BlockSpec((tm, RNN_WIDTH), row),
            pl.BlockSpec((tm, RNN_WIDTH), row),
            pl.BlockSpec((tm, d), row),
            pl.BlockSpec((1, d), fixed),
            pl.BlockSpec((RNN_WIDTH, d), fixed),
            pl.BlockSpec((1, d), fixed),
        ]
        args += [yr, gr, xl, gt, wo, fg]
        out_shape = jax.ShapeDtypeStruct((n, d), jnp.float32)
        out_specs = pl.BlockSpec((tm, d), row)
        scratch = scratch + [pltpu.VMEM((tm, RNN_WIDTH), jnp.float32)]
    else:
        out_shape = [jax.ShapeDtypeStruct((n, RNN_WIDTH), jnp.float32),
                     jax.ShapeDtypeStruct((1, RNN_WIDTH), jnp.float32)]
        out_specs = [pl.BlockSpec((tm, RNN_WIDTH), row), vec]
    scratch = scratch + [pltpu.VMEM((1, RNN_WIDTH), jnp.float32)]
    return pl.pallas_call(
        functools.partial(_rec_scan_kernel, reverse=reverse, fuse_out=fused is not None, nt=nt),
        out_shape=out_shape,
        grid=(nt,),
        in_specs=in_specs,
        out_specs=out_specs,
        scratch_shapes=scratch,
        compiler_params=_params("arbitrary"),
        name="rec_scan_out" if fused is not None else ("rec_scan_rev" if reverse else "rec_scan_fwd"),
    )(*args)


def _rope_tables(n):
    pos = jnp.arange(n, dtype=jnp.int32)
    row = (pos // GRID_W).astype(jnp.float32)
    col = (pos % GRID_W).astype(jnp.float32)
    n_freq = HEAD_DIM // 4
    inv_freq = ROPE_BASE ** (-jnp.arange(n_freq, dtype=jnp.float32) / n_freq)
    ar = row[:, None] * inv_freq
    ac = col[:, None] * inv_freq
    ang = jnp.concatenate([ar, ar, ac, ac], axis=-1)
    sign = jnp.where((jnp.arange(HEAD_DIM) % 32) < 16, -1.0, 1.0)
    cos = jnp.cos(ang)
    sin = jnp.sin(ang) * sign
    return jnp.tile(cos, (1, 2)), jnp.tile(sin, (1, 2))


def _block_diag(w):
    nb, bi, bj = w.shape
    eye = jnp.eye(nb, dtype=w.dtype)
    return jnp.einsum('hij,hg->higj', w, eye).reshape(nb * bi, nb * bj)


def kernel(x, c, ctx, c_ctx, norm_g, ada_w, ada_b, attn_w_in, attn_w_out, attn_sink, lam_q1, lam_k1,
           lam_q2, lam_k2, subln_g, rec_w_in, rec_conv_w, rec_conv_b, rec_wa, rec_ba, rec_wx, rec_bx,
           rec_lam, rec_w_out, final_g):
    assert x.shape[0] == 1 and c.shape[0] == 1 and ctx.shape[0] == 1
    bf = jnp.bfloat16
    d = D_MODEL
    xl = x[0]
    xc = ctx[0]
    n = xl.shape[0]

    cv = jnp.zeros((SUBLANES, d), jnp.float32).at[0].set(c[0]).at[1].set(c_ctx)
    mods = _adaln(cv, ada_w, ada_b)

    def mod(l, r):
        m = mods[l, r]
        return m[None, :d], m[None, d:2 * d], m[None, 2 * d:]

    lam_init = 0.8 - 0.6 * math.exp(-0.3 * 0)
    s1 = A_WIDTH
    s2 = s1 + A_KV_WIDTH
    s3 = s2 + A_KV_WIDTH
    s5 = s3 + 2 * B_WIDTH
    s6 = s5 + B_WIDTH
    head_perm = jnp.array([g * A_GROUP + j for j in range(A_GROUP) for g in range(A_KV_HEADS)])
    col_perm = (head_perm[:, None] * HEAD_DIM + jnp.arange(HEAD_DIM)).reshape(-1)
    w_in = attn_w_in[0]
    w_main = jnp.concatenate([w_in[:, :s1][:, col_perm], w_in[:, s1:s2], w_in[:, s3:s5],
                              w_in[:, s6:s6 + A_WIDTH][:, col_perm], w_in[:, s6 + A_WIDTH:]],
                             axis=1).astype(bf)
    w_vt = jnp.concatenate([w_in[:, s2:s3], w_in[:, s5:s6]], axis=1).T.astype(bf)
    w_out = attn_w_out[0]
    w_out = jnp.concatenate([w_out[:A_WIDTH][col_perm], w_out[A_WIDTH:]], axis=0).astype(bf)
    g0 = norm_g[0][None]
    cos, sin = _rope_tables(n)
    sh, sc, gt = mod(0, 0)
    shc, scc, gtc = mod(0, 1)
    qa, ka, qb, kb, vt, gate = _attn_in_proj(xl, g0, sc, sh, w_main, w_vt, cos, sin, rope=True)
    nc = xc.shape[0]
    qac, kac, qbc, kbc, vtc, gatec = _attn_in_proj(
        xc, g0, scc, shc, w_main, w_vt, cos[:nc], sin[:nc], rope=False)

    sink_gj = (attn_sink[0] * LOG2E).reshape(A_KV_HEADS, A_GROUP)
    kj = jnp.arange(3 * BLOCK)[:, None]
    qi = jnp.arange(BLOCK)[None, :]
    band_bias = jnp.where(jnp.abs(kj - BLOCK - qi) <= WINDOW, 0.0, NEG_INF).astype(jnp.float32)
    lamv = jnp.stack([lam_q1[0], lam_k1[0], lam_q2[0], lam_k2[0]])
    subln_col = subln_g[0][:, None]
    oa = _gqa_attention(jnp.repeat(sink_gj, BLOCK, axis=1), band_bias, qa, ka, vt, kac, vtc,
                        has_window=True)
    ob = _diff_attention(lamv, subln_col, qb, kb, vt, kbc, vtc, has_latent=True, lam_init=lam_init)
    oac = _gqa_attention(jnp.repeat(sink_gj, nc, axis=1), None, qac, None, None, kac, vtc,
                         has_window=False)
    obc = _diff_attention(lamv, subln_col, qbc, None, None, kbc, vtc, has_latent=False,
                          lam_init=lam_init)
    xl = _attn_out_proj(oa, ob, gate, xl, gt, w_out)
    xc = _attn_out_proj(oac, obc, gatec, xc, gtc, w_out)

    g1 = norm_g[1][None]
    sh, sc, gt = mod(1, 0)
    shc, scc, _ = mod(1, 1)
    w_rin = rec_w_in[0].astype(bf)
    xr, gr = _rec_in_proj(xl, g1, sc, sh, w_rin)
    xrc, _ = _rec_in_proj(xc, g1, scc, shc, w_rin)
    cw = 0.5 * rec_conv_w[0]
    cb = 0.5 * rec_conv_b[0][None]
    zero_h = jnp.zeros((1, RNN_WIDTH), jnp.float32)

    def coeff(dr):
        hb = RNN_BLOCKS // 2
        w = jnp.stack([jnp.concatenate([_block_diag(rec_wa[0, dr, i * hb:(i + 1) * hb]),
                                        _block_diag(rec_wx[0, dr, i * hb:(i + 1) * hb])], axis=1)
                       for i in range(2)]).astype(bf)
        bias = 0.5 * jnp.concatenate([rec_ba[0, dr], rec_bx[0, dr]])[None]
        return w, bias, rec_lam[0, dr][None]

    wf, bias_f, lam_f = coeff(0)
    wr, bias_r, lam_r = coeff(1)
    _, h0f = _rec_scan(xrc, cw, cb, wf, bias_f, lam_f, zero_h, reverse=False)
    _, h0r = _rec_scan(xrc, cw, cb, wr, bias_r, lam_r, zero_h, reverse=True)
    yr, _ = _rec_scan(xr, cw, cb, wr, bias_r, lam_r, h0r, reverse=True)
    out = _rec_scan(xr, cw, cb, wf, bias_f, lam_f, h0f, reverse=False,
                    fused=(yr, gr, xl, gt, rec_w_out[0].astype(bf), final_g[None]))
    return out[None]
```

```python
import functools
import math

import jax
import jax.numpy as jnp
from jax import lax
from jax.experimental import pallas as pl
from jax.experimental.pallas import tpu as pltpu

D_MODEL = 1024
GRID_W = 64
HEAD_DIM = 64
ROPE_BASE = 10000.0
EPS = 1e-6
NEG_INF = -1e30
A_Q_HEADS = 8
A_KV_HEADS = 2
A_GROUP = A_Q_HEADS // A_KV_HEADS
A_WIDTH = A_Q_HEADS * HEAD_DIM
A_KV_WIDTH = A_KV_HEADS * HEAD_DIM
WINDOW = 128
BLOCK = 128
B_HEADS = 4
B_WIDTH = B_HEADS * 2 * HEAD_DIM
ATTN_WIDTH = A_WIDTH + B_WIDTH
RNN_WIDTH = 1280
RNN_BLOCKS = 16
RNN_BLOCK_DIM = RNN_WIDTH // RNN_BLOCKS
CONV_WIDTH = 4
RG_C = 8.0
LOG2E = math.log2(math.e)
Q_SCALE_LOG2 = HEAD_DIM ** -0.5 * LOG2E

LANES = 128
SUBLANES = 8
VMEM_LIMIT = 48 * 1024 * 1024

ROW_TILE = 512
SCAN_TILE = 256
GQA_BLOCKS = 4
DIFF_TQ = 256
DIFF_TK = 512
DIFF_STAGES = 16
DIFF_JUMP_LIMIT = 64.0
SUM_ROWS = 16

_NT = (((1,), (1,)), ((), ()))


def _silu(x):
    return x * jax.nn.sigmoid(x)


def _norm_mod(x, g, sc, sh):
    ms = jnp.mean(x * x, axis=-1, keepdims=True)
    return (x * lax.rsqrt(ms + EPS) * g) * (1.0 + sc) + sh


def _params(*sem, flags=None):
    return pltpu.CompilerParams(dimension_semantics=sem, vmem_limit_bytes=VMEM_LIMIT, flags=flags)


def _adaln_kernel(cv_ref, w_ref, b_ref, o_ref):
    s = _silu(cv_ref[...])
    o_ref[0] = jnp.dot(s, w_ref[0], preferred_element_type=jnp.float32) + b_ref[0]


def _adaln(cv, ada_w, ada_b):
    depth, d, n3 = ada_w.shape
    tn = 1024
    return pl.pallas_call(
        _adaln_kernel,
        out_shape=jax.ShapeDtypeStruct((depth, SUBLANES, n3), jnp.float32),
        grid=(depth, n3 // tn),
        in_specs=[
            pl.BlockSpec((SUBLANES, d), lambda l, j: (0, 0)),
            pl.BlockSpec((1, d, tn), lambda l, j: (l, 0, j)),
            pl.BlockSpec((1, 1, tn), lambda l, j: (l, 0, j)),
        ],
        out_specs=pl.BlockSpec((1, SUBLANES, tn), lambda l, j: (l, 0, j)),
        compiler_params=_params("parallel", "parallel"),
        name="adaln",
    )(cv, ada_w, ada_b.reshape(depth, 1, n3))


def _rope(x, cos, sin_signed, first_half):
    rot = jnp.where(first_half, pltpu.roll(x, LANES - 16, 1), pltpu.roll(x, 16, 1))
    return x * cos + rot * sin_signed


def _attn_in_kernel(x_ref, g_ref, sc_ref, sh_ref, w_ref, wvt_ref, cos_ref, sin_ref,
                    qa_ref, ka_ref, qb_ref, kb_ref, vt_ref, gate_ref, *, rope):
    h = _norm_mod(x_ref[...], g_ref[...], sc_ref[...], sh_ref[...]).astype(jnp.bfloat16)
    p = jnp.dot(h, w_ref[...], preferred_element_type=jnp.float32)
    vt_ref[...] = lax.dot_general(wvt_ref[...], h, _NT,
                                  preferred_element_type=jnp.float32).astype(jnp.bfloat16)
    if rope:
        cos = cos_ref[...]
        sin = sin_ref[...]
        lane = lax.broadcasted_iota(jnp.int32, cos.shape, 1)
        first_half = (lane % 32) < 16

    def chunk(c):
        x = p[:, c * LANES:(c + 1) * LANES]
        return _rope(x, cos, sin, first_half) if rope else x

    o0 = A_WIDTH // LANES
    o1 = o0 + A_KV_WIDTH // LANES
    o2 = o1 + B_WIDTH // LANES
    o3 = o2 + B_WIDTH // LANES
    for c in range(o0):
        qa_ref[:, c * LANES:(c + 1) * LANES] = (chunk(c) * Q_SCALE_LOG2).astype(jnp.bfloat16)
    for c in range(o0, o1):
        ka_ref[:, (c - o0) * LANES:(c - o0 + 1) * LANES] = chunk(c).astype(jnp.bfloat16)
    for c in range(o1, o2):
        qb_ref[:, (c - o1) * LANES:(c - o1 + 1) * LANES] = (chunk(c) * Q_SCALE_LOG2).astype(jnp.bfloat16)
    for c in range(o2, o3):
        kb_ref[:, (c - o2) * LANES:(c - o2 + 1) * LANES] = chunk(c).astype(jnp.bfloat16)
    gate_ref[...] = p[:, o3 * LANES:].astype(jnp.bfloat16)


def _attn_in_proj(x, g, sc, sh, w, wvt, cos, sin, *, rope):
    n, d = x.shape
    tm = min(ROW_TILE, n)
    nw = w.shape[1]
    nv = wvt.shape[0]
    row = lambda i: (i, 0)
    fixed = lambda i: (0, 0)
    bf = jnp.bfloat16
    return pl.pallas_call(
        functools.partial(_attn_in_kernel, rope=rope),
        out_shape=[
            jax.ShapeDtypeStruct((n, A_WIDTH), bf),
            jax.ShapeDtypeStruct((n, A_KV_WIDTH), bf),
            jax.ShapeDtypeStruct((n, B_WIDTH), bf),
            jax.ShapeDtypeStruct((n, B_WIDTH), bf),
            jax.ShapeDtypeStruct((nv, n), bf),
            jax.ShapeDtypeStruct((n, ATTN_WIDTH), bf),
        ],
        grid=(n // tm,),
        in_specs=[
            pl.BlockSpec((tm, d), row),
            pl.BlockSpec((1, d), fixed),
            pl.BlockSpec((1, d), fixed),
            pl.BlockSpec((1, d), fixed),
            pl.BlockSpec((d, nw), fixed),
            pl.BlockSpec((nv, d), fixed),
            pl.BlockSpec((tm, LANES), row),
            pl.BlockSpec((tm, LANES), row),
        ],
        out_specs=[
            pl.BlockSpec((tm, A_WIDTH), row),
            pl.BlockSpec((tm, A_KV_WIDTH), row),
            pl.BlockSpec((tm, B_WIDTH), row),
            pl.BlockSpec((tm, B_WIDTH), row),
            pl.BlockSpec((nv, tm), lambda i: (0, i)),
            pl.BlockSpec((tm, ATTN_WIDTH), row),
        ],
        compiler_params=_params("parallel"),
        name="attn_in_proj",
    )(x, g, sc, sh, w, wvt, cos, sin)


def _gqa_scores(q, keys):
    tq = q.shape[0]
    lane = lax.broadcasted_iota(jnp.int32, (tq, LANES), 1)
    zero = jnp.zeros((tq, LANES), q.dtype)
    out = []
    for g in range(A_KV_HEADS):
        in_g = (lane >= g * HEAD_DIM) & (lane < (g + 1) * HEAD_DIM)
        qg = jnp.concatenate([jnp.where(in_g, q[:, j * LANES:(j + 1) * LANES], zero)
                              for j in range(A_GROUP)], axis=0)
        out.append(lax.dot_general(keys, qg, _NT, preferred_element_type=jnp.float32))
    return out


def _gqa_outputs(scores, vt, bias, sink_ref, tq):
    outs = [None] * A_Q_HEADS
    for g in range(A_KV_HEADS):
        s = scores[g]
        if bias is not None:
            s = jnp.concatenate([s[:3 * BLOCK] + bias, s[3 * BLOCK:]], axis=0)
        sk = sink_ref[g:g + 1, :]
        m = jnp.maximum(jnp.max(s, axis=0, keepdims=True), sk)
        acc = jnp.dot(vt, jnp.exp2((s - m).astype(jnp.bfloat16)),
                      preferred_element_type=jnp.float32)
        l = acc[A_KV_WIDTH:A_KV_WIDTH + 1] + jnp.exp2(sk - m)
        o = acc[g * HEAD_DIM:(g + 1) * HEAD_DIM] / l
        for j in range(A_GROUP):
            outs[2 * j + g] = o[:, j * tq:(j + 1) * tq]
    return jnp.concatenate(outs, axis=0).T


def _gqa_window_kernel(sink_ref, bias_ref, q_ref, *refs, nb):
    nk = GQA_BLOCKS + 2
    k_refs, vt_refs = refs[:nk], refs[nk:2 * nk]
    kx_ref, vxt_ref, o_ref = refs[2 * nk:]
    i = pl.program_id(0)
    ones = jnp.ones((SUM_ROWS, 3 * BLOCK + kx_ref.shape[0]), jnp.bfloat16)
    row = lax.broadcasted_iota(jnp.int32, (3 * BLOCK, 1), 0)
    all_scores, vts, biases = [], [], []
    for t in range(GQA_BLOCKS):
        keys = jnp.concatenate([r[...] for r in k_refs[t:t + 3]] + [kx_ref[...]], axis=0)
        vt = jnp.concatenate([r[...] for r in vt_refs[t:t + 3]] + [vxt_ref[...]], axis=1)
        vts.append(jnp.concatenate([vt, ones], axis=0))
        all_scores.append(_gqa_scores(q_ref[t * BLOCK:(t + 1) * BLOCK, :], keys))
        b = GQA_BLOCKS * i + t
        off_end = ((row < BLOCK) & (b == 0)) | ((row >= 2 * BLOCK) & (b == nb - 1))
        bias = bias_ref[...] + jnp.where(off_end, NEG_INF, 0.0)
        biases.append(jnp.concatenate([bias] * A_GROUP, axis=1))
    for t in range(GQA_BLOCKS):
        o = _gqa_outputs(all_scores[t], vts[t], biases[t], sink_ref, BLOCK)
        o_ref[t * BLOCK:(t + 1) * BLOCK, :] = o.astype(o_ref.dtype)


def _gqa_ctx_kernel(sink_ref, q_ref, kx_ref, vxt_ref, o_ref):
    vt = vxt_ref[...]
    vt = jnp.concatenate([vt, jnp.ones((SUM_ROWS, vt.shape[1]), vt.dtype)], axis=0)
    q = q_ref[...]
    o = _gqa_outputs(_gqa_scores(q, kx_ref[...]), vt, None, sink_ref, q.shape[0])
    o_ref[...] = o.astype(o_ref.dtype)


def _gqa_attention(sink_cols, bias, q, k, vt, kx, vxt, *, has_window):
    n = q.shape[0]
    nx = kx.shape[0]
    fixed = lambda i: (0, 0)
    if has_window:
        nb = n // BLOCK
        tq = GQA_BLOCKS * BLOCK
        steps = nb // GQA_BLOCKS
        assert steps * GQA_BLOCKS == nb
        offs = range(-1, GQA_BLOCKS + 1)
        blk = lambda i, o: jnp.clip(GQA_BLOCKS * i + o, 0, nb - 1)
        k_specs = [pl.BlockSpec((BLOCK, A_KV_WIDTH), lambda i, o=o: (blk(i, o), 0)) for o in offs]
        vt_specs = [pl.BlockSpec((A_KV_WIDTH, BLOCK), lambda i, o=o: (0, blk(i, o))) for o in offs]
        in_specs = ([pl.BlockSpec((A_KV_HEADS, A_GROUP * BLOCK), fixed),
                     pl.BlockSpec((3 * BLOCK, BLOCK), fixed),
                     pl.BlockSpec((tq, A_WIDTH), lambda i: (i, 0))] + k_specs + vt_specs +
                    [pl.BlockSpec((nx, A_KV_WIDTH), fixed), pl.BlockSpec((A_KV_WIDTH, nx), fixed)])
        args = (sink_cols, bias, q) + (k,) * len(offs) + (vt,) * len(offs) + (kx, vxt)
        body = functools.partial(_gqa_window_kernel, nb=nb)
    else:
        steps = 1
        tq = n
        in_specs = [pl.BlockSpec((A_KV_HEADS, A_GROUP * tq), fixed), pl.BlockSpec((tq, A_WIDTH), fixed),
                    pl.BlockSpec((nx, A_KV_WIDTH), fixed), pl.BlockSpec((A_KV_WIDTH, nx), fixed)]
        args = (sink_cols, q, kx, vxt)
        body = _gqa_ctx_kernel
    return pl.pallas_call(
        body,
        out_shape=jax.ShapeDtypeStruct((n, A_WIDTH), jnp.bfloat16),
        grid=(steps,),
        in_specs=in_specs,
        out_specs=pl.BlockSpec((tq, A_WIDTH), lambda i: (i, 0)),
        compiler_params=_params("parallel"),
        name="gqa_window" if has_window else "gqa_ctx",
    )(*args)


def _diff_masked_q(q_ref):
    q = q_ref[...]
    lane = lax.broadcasted_iota(jnp.int32, q.shape, 1)
    zero = jnp.zeros_like(q)
    return jnp.concatenate([jnp.where(lane < HEAD_DIM, q, zero),
                            jnp.where(lane >= HEAD_DIM, q, zero)], axis=0)


def _diff_probs(s_rel):
    return jnp.exp2(s_rel.astype(jnp.bfloat16))


def _diff_pv(vt, p):
    ones = jnp.ones((SUM_ROWS, vt.shape[1]), vt.dtype)
    return jnp.dot(jnp.concatenate([vt, ones], axis=0), p, preferred_element_type=jnp.float32)


def _diff_ctx_step(qz, kx, vxt):
    s = lax.dot_general(kx, qz, _NT, preferred_element_type=jnp.float32)
    m = jnp.max(s, axis=0, keepdims=True)
    return m, _diff_pv(vxt, _diff_probs(s - m))


def _diff_finalize(lam_ref, g_ref, acc, lam_init):
    hw = 2 * HEAD_DIM
    tq = acc.shape[1] // 2
    lv = lam_ref[...]
    lam = (jnp.exp(jnp.sum(lv[0:1] * lv[1:2], keepdims=True))
           - jnp.exp(jnp.sum(lv[2:3] * lv[3:4], keepdims=True)) + lam_init)
    on = acc[:hw] / acc[hw:hw + 1]
    o = on[:, :tq] - lam * on[:, tq:]
    ms = jnp.mean(o * o, axis=0, keepdims=True)
    o = (o * lax.rsqrt(ms + EPS)) * g_ref[...] * (1.0 - lam_init)
    return o.T.astype(jnp.bfloat16)


def _diff_ctx_kernel(lam_ref, g_ref, q_ref, kx_ref, vxt_ref, o_ref, *, lam_init):
    _, acc = _diff_ctx_step(_diff_masked_q(q_ref), kx_ref[...], vxt_ref[...])
    o_ref[...] = _diff_finalize(lam_ref, g_ref, acc, lam_init)


def _bf16_round(x):
    return x.astype(jnp.bfloat16).astype(jnp.float32)


def _diff_kernel(lam_ref, g_ref, q_ref, k_ref, vt_ref, kx_ref, vxt_ref, o_ref,
                 p0, p1, p2, p3, acc_ref, *, lam_init, n_chunks):
    tk = DIFF_TK
    qz = _diff_masked_q(q_ref)
    cols = qz.shape[0]
    qzt = qz.astype(jnp.float32).T.astype(jnp.bfloat16)
    ones_k = jnp.ones((tk, LANES), jnp.bfloat16)
    first_row = lax.broadcasted_iota(jnp.int32, (2 * SUBLANES, cols), 0) == 0
    zero_rows = jnp.zeros((LANES - 2 * SUBLANES, cols), jnp.bfloat16)
    p_buf = (p0, p1, p2, p3)

    def rhs_for(r):
        neg_r = jnp.where(first_row, -r, 0.0).astype(jnp.bfloat16)
        return jnp.concatenate([qzt, neg_r, zero_rows], axis=0)

    def rel_scores(c, rhs):
        k = k_ref[pl.ds(pl.multiple_of(c * tk, tk), tk), :]
        return jnp.dot(jnp.concatenate([k, ones_k], axis=1), rhs, preferred_element_type=jnp.float32)

    def pv(c, p_ref):
        return _diff_pv(vt_ref[:, pl.ds(pl.multiple_of(c * tk, tk), tk)], p_ref[...])

    def col_max(sa, sb):
        return jnp.maximum(jnp.max(sa, axis=0, keepdims=True), jnp.max(sb, axis=0, keepdims=True))

    s_ctx = lax.dot_general(kx_ref[...], qz, _NT, preferred_element_type=jnp.float32)
    m_ctx = jnp.max(s_ctx, axis=0, keepdims=True)
    r0 = _bf16_round(m_ctx)
    rhs0 = rhs_for(r0)
    sa, sb = rel_scores(0, rhs0), rel_scores(1, rhs0)
    acc_ref[...] = _diff_pv(vxt_ref[...], _diff_probs(s_ctx - r0))
    worst = col_max(sa, sb)
    p_buf[0][...] = _diff_probs(sa)
    p_buf[1][...] = _diff_probs(sb)

    def body(j, carry):
        run_max, pend, r_prev, r_cur, worst = carry
        for i in range(0, DIFF_STAGES, 2):
            c = DIFF_STAGES * j + i
            r_next = _bf16_round(run_max)
            rhs = rhs_for(r_next)
            pva = pv(c, p_buf[i % 4])
            sa = rel_scores(jnp.minimum(c + 2, n_chunks - 1), rhs)
            pvb = pv(c + 1, p_buf[(i + 1) % 4])
            sb = rel_scores(jnp.minimum(c + 3, n_chunks - 1), rhs)
            acc_ref[...] = jnp.exp2(r_prev - r_cur) * acc_ref[...] + pva + pvb
            top = col_max(sa, sb)
            p_buf[(i + 2) % 4][...] = _diff_probs(sa)
            p_buf[(i + 3) % 4][...] = _diff_probs(sb)
            worst = jnp.maximum(worst, top)
            run_max = jnp.maximum(run_max, pend)
            pend = top + r_next
            r_prev, r_cur = r_cur, r_next
        return run_max, pend, r_prev, r_cur, worst

    carry = lax.fori_loop(0, n_chunks // DIFF_STAGES, body, (m_ctx, worst + r0, r0, r0, worst))
    o_ref[...] = _diff_finalize(lam_ref, g_ref, acc_ref[...], lam_init)

    @pl.when(jnp.max(carry[4]) > DIFF_JUMP_LIMIT)
    def _():
        m0, acc0 = _diff_ctx_step(qz, kx_ref[...], vxt_ref[...])

        def exact(c, carry):
            m, acc = carry
            off = pl.multiple_of(c * tk, tk)
            s = lax.dot_general(k_ref[pl.ds(off, tk), :], qz, _NT, preferred_element_type=jnp.float32)
            m_new = jnp.maximum(m, jnp.max(s, axis=0, keepdims=True))
            acc = jnp.exp2(m - m_new) * acc + _diff_pv(vt_ref[:, pl.ds(off, tk)], _diff_probs(s - m_new))
            return m_new, acc

        _, acc = lax.fori_loop(0, n_chunks, exact, (m0, acc0))
        o_ref[...] = _diff_finalize(lam_ref, g_ref, acc, lam_init)


def _diff_attention(lamv, subln_col, q, k, vt, kx, vxt, *, has_latent, lam_init):
    n = q.shape[0]
    nx = kx.shape[0]
    tq = min(DIFF_TQ, n)
    hw = 2 * HEAD_DIM
    fixed = lambda h, i: (0, 0)
    in_specs = [pl.BlockSpec((4, HEAD_DIM), fixed), pl.BlockSpec((hw, 1), fixed),
                pl.BlockSpec((tq, hw), lambda h, i: (i, h))]
    args = [lamv, subln_col, q]
    scratch = []
    if has_latent:
        nk = k.shape[0]
        n_chunks = nk // DIFF_TK
        assert DIFF_STAGES % 4 == 0 and n_chunks % DIFF_STAGES == 0 and n_chunks * DIFF_TK == nk
        in_specs += [pl.BlockSpec((nk, hw), lambda h, i: (0, h)),
                     pl.BlockSpec((hw, nk), lambda h, i: (h + 1, 0))]
        args += [k, vt]
        body = functools.partial(_diff_kernel, lam_init=lam_init, n_chunks=n_chunks)
        scratch = ([pltpu.VMEM((DIFF_TK, 2 * tq), jnp.bfloat16)] * 4 +
                   [pltpu.VMEM((hw + SUM_ROWS, 2 * tq), jnp.float32)])
    else:
        body = functools.partial(_diff_ctx_kernel, lam_init=lam_init)
    in_specs += [pl.BlockSpec((nx, hw), lambda h, i: (0, h)),
                 pl.BlockSpec((hw, nx), lambda h, i: (h + 1, 0))]
    args += [kx, vxt]
    return pl.pallas_call(
        body,
        out_shape=jax.ShapeDtypeStruct((n, B_WIDTH), jnp.bfloat16),
        grid=(B_HEADS, n // tq),
        in_specs=in_specs,
        out_specs=pl.BlockSpec((tq, hw), lambda h, i: (i, h)),
        scratch_shapes=scratch,
        compiler_params=_params("parallel", "parallel"),
        name="diff_attn" if has_latent else "diff_attn_ctx",
    )(*args)


def _attn_out_rec_in_kernel(oa_ref, ob_ref, gate_ref, x_ref, gt_ref, w_ref,
                            g_ref, sc_ref, sh_ref, wr_ref, xl_ref, xr_ref, gr_ref):
    o = jnp.concatenate([oa_ref[...], ob_ref[...]], axis=-1).astype(jnp.float32)
    a = o * _silu(gate_ref[...].astype(jnp.float32))
    y = jnp.dot(a.astype(jnp.bfloat16), w_ref[...], preferred_element_type=jnp.float32)
    xl = x_ref[...] + gt_ref[...] * y
    xl_ref[...] = xl
    h = _norm_mod(xl, g_ref[...], sc_ref[...], sh_ref[...]).astype(jnp.bfloat16)
    p = jnp.dot(h, wr_ref[...], preferred_element_type=jnp.float32)
    xr_ref[...] = p[:, :RNN_WIDTH]
    gr_ref[...] = p[:, RNN_WIDTH:]


def _attn_out_rec_in(oa, ob, gate, x, gt, w, g, sc, sh, wr):
    n, d = x.shape
    tm = min(ROW_TILE, n)
    row = lambda i: (i, 0)
    fixed = lambda i: (0, 0)
    vec = pl.BlockSpec((1, d), fixed)
    return pl.pallas_call(
        _attn_out_rec_in_kernel,
        out_shape=[jax.ShapeDtypeStruct((n, d), jnp.float32),
                   jax.ShapeDtypeStruct((n, RNN_WIDTH), jnp.float32),
                   jax.ShapeDtypeStruct((n, RNN_WIDTH), jnp.float32)],
        grid=(n // tm,),
        in_specs=[
            pl.BlockSpec((tm, A_WIDTH), row),
            pl.BlockSpec((tm, B_WIDTH), row),
            pl.BlockSpec((tm, ATTN_WIDTH), row),
            pl.BlockSpec((tm, d), row),
            vec,
            pl.BlockSpec((ATTN_WIDTH, d), fixed),
            vec, vec, vec,
            pl.BlockSpec((d, 2 * RNN_WIDTH), fixed),
        ],
        out_specs=[pl.BlockSpec((tm, d), row), pl.BlockSpec((tm, RNN_WIDTH), row),
                   pl.BlockSpec((tm, RNN_WIDTH), row)],
        compiler_params=_params("parallel"),
        name="attn_out_rec_in",
    )(oa, ob, gate, x, gt, w, g, sc, sh, wr)


def _rec_scan_kernel(*refs, reverse, fuse_out, nt):
    if fuse_out:
        (x_ref, xp_ref, xn_ref, cw_ref, cb_ref, w_ref, bias_ref, lam_ref, h0_ref,
         yr_ref, gr_ref, xl_ref, gt_ref, wo_ref, fg_ref,
         o_ref, xe_ref, a_ref, b_ref, y_ref, h_ref) = refs
    else:
        (x_ref, xp_ref, xn_ref, cw_ref, cb_ref, w_ref, bias_ref, lam_ref, h0_ref,
         y_ref, hf_ref, xe_ref, a_ref, b_ref, h_ref) = refs
    i = pl.program_id(0)
    t = i if not reverse else nt - 1 - i
    tm = x_ref.shape[0]

    @pl.when(i == 0)
    def _():
        h_ref[...] = h0_ref[...]

    xe_ref[0:SUBLANES, :] = jnp.where(t > 0, xp_ref[...], 0.0)
    xe_ref[SUBLANES:SUBLANES + tm, :] = x_ref[...]
    xe_ref[SUBLANES + tm:, :] = jnp.where(t < nt - 1, xn_ref[...], 0.0)
    cw = cw_ref[...]
    xe = xe_ref[...]
    t = pltpu.roll(xe * cw[0:1], 1, 0)
    t = pltpu.roll(t + xe * cw[1:2], 1, 0)
    t = t + xe * cw[2:3] + pltpu.roll(xe * cw[3:4], tm + 2 * SUBLANES - 1, 0)
    uh = t[SUBLANES:SUBLANES + tm] + cb_ref[...]
    ub = uh.astype(jnp.bfloat16)
    half = RNN_WIDTH // 2
    z0 = jnp.dot(ub[:, :half], w_ref[0], preferred_element_type=jnp.float32)
    z1 = jnp.dot(ub[:, half:], w_ref[1], preferred_element_type=jnp.float32)
    bias = bias_ref[...]
    ta = jnp.tanh(jnp.concatenate([z0[:, :half], z1[:, :half]], axis=1) + bias[:, :RNN_WIDTH])
    tx = jnp.tanh(jnp.concatenate([z0[:, half:], z1[:, half:]], axis=1) + bias[:, RNN_WIDTH:])
    nl = -lam_ref[...]
    softplus = jnp.maximum(nl, 0.0) + jnp.log1p(jnp.exp(-jnp.abs(nl)))
    coef = (-0.5 * RG_C * LOG2E) * softplus
    a = jnp.exp2(coef + coef * ta)
    a_ref[...] = a
    om = 1.0 - a * a
    b_ref[...] = (om * lax.rsqrt(jnp.maximum(om, 1e-30))) * ((1.0 + tx) * uh)

    def row(s, h):
        rr = (tm - 1 - s) if reverse else s
        h = a_ref[pl.ds(rr, 1), :] * h + b_ref[pl.ds(rr, 1), :]
        y_ref[pl.ds(rr, 1), :] = h
        return h

    h = lax.fori_loop(0, tm, row, h_ref[...], unroll=8)
    h_ref[...] = h

    if fuse_out:
        y = (y_ref[...] + yr_ref[...]) * _silu(gr_ref[...])
        out = jnp.dot(y.astype(jnp.bfloat16), wo_ref[...], preferred_element_type=jnp.float32)
        xl = xl_ref[...] + gt_ref[...] * out
        ms = jnp.mean(xl * xl, axis=-1, keepdims=True)
        o_ref[...] = xl * lax.rsqrt(ms + EPS) * fg_ref[...]
    else:
        hf_ref[...] = h


def _rec_scan(xr, cw, cb, w, bias, lam, h0, *, reverse, fused=None):
    n = xr.shape[0]
    tm = min(SCAN_TILE, n)
    nt = n // tm
    per = tm // SUBLANES
    nblk8 = n // SUBLANES
    pos = (lambda i: nt - 1 - i) if reverse else (lambda i: i)
    row = lambda i: (pos(i), 0)
    prev = lambda i: (jnp.maximum(pos(i) * per - 1, 0), 0)
    nxt = lambda i: (jnp.minimum((pos(i) + 1) * per, nblk8 - 1), 0)
    fixed = lambda i: (0, 0)
    vec = pl.BlockSpec((1, RNN_WIDTH), fixed)
    in_specs = [
        pl.BlockSpec((tm, RNN_WIDTH), row),
        pl.BlockSpec((SUBLANES, RNN_WIDTH), prev),
        pl.BlockSpec((SUBLANES, RNN_WIDTH), nxt),
        pl.BlockSpec((CONV_WIDTH, RNN_WIDTH), fixed),
        vec,
        pl.BlockSpec((2, RNN_WIDTH // 2, RNN_WIDTH), lambda i: (0, 0, 0)),
        pl.BlockSpec((1, 2 * RNN_WIDTH), fixed),
        vec,
        vec,
    ]
    args = [xr, xr, xr, cw, cb, w, bias, lam, h0]
    scratch = [pltpu.VMEM((tm + 2 * SUBLANES, RNN_WIDTH), jnp.float32),
               pltpu.VMEM((tm, RNN_WIDTH), jnp.float32),
               pltpu.VMEM((tm, RNN_WIDTH), jnp.float32)]
    if fused is not None:
        yr, gr, xl, gt, wo, fg = fused
        d = xl.shape[1]
        in_specs += [
            pl.BlockSpec((tm, RNN_WIDTH), row),
            pl.BlockSpec((tm, RNN_WIDTH), row),
            pl.BlockSpec((tm, d), row),
            pl.BlockSpec((1, d), fixed),
            pl.BlockSpec((RNN_WIDTH, d), fixed),
            pl.BlockSpec((1, d), fixed),
        ]
        args += [yr, gr, xl, gt, wo, fg]
        out_shape = jax.ShapeDtypeStruct((n, d), jnp.float32)
        out_specs = pl.BlockSpec((tm, d), row)
        scratch = scratch + [pltpu.VMEM((tm, RNN_WIDTH), jnp.float32)]
    else:
        out_shape = [jax.ShapeDtypeStruct((n, RNN_WIDTH), jnp.float32),
                     jax.ShapeDtypeStruct((1, RNN_WIDTH), jnp.float32)]
        out_specs = [pl.BlockSpec((tm, RNN_WIDTH), row), vec]
    scratch = scratch + [pltpu.VMEM((1, RNN_WIDTH), jnp.float32)]
    return pl.pallas_call(
        functools.partial(_rec_scan_kernel, reverse=reverse, fuse_out=fused is not None, nt=nt),
        out_shape=out_shape,
        grid=(nt,),
        in_specs=in_specs,
        out_specs=out_specs,
        scratch_shapes=scratch,
        compiler_params=_params("arbitrary"),
        name="rec_scan_out" if fused is not None else ("rec_scan_rev" if reverse else "rec_scan_fwd"),
    )(*args)


def _rope_tables(n):
    rows = n // GRID_W
    n_freq = HEAD_DIM // 4
    inv_freq = ROPE_BASE ** (-jnp.arange(n_freq, dtype=jnp.float32) / n_freq)
    ar = jnp.arange(rows, dtype=jnp.float32)[:, None] * inv_freq
    ac = jnp.arange(GRID_W, dtype=jnp.float32)[:, None] * inv_freq

    def table(fn, sign):
        r = jnp.broadcast_to(fn(ar)[:, None, :], (rows, GRID_W, n_freq)).reshape(n, n_freq)
        c = jnp.broadcast_to(fn(ac)[None, :, :], (rows, GRID_W, n_freq)).reshape(n, n_freq)
        head = jnp.concatenate([sign * r, r, sign * c, c], axis=-1)
        return jnp.tile(head, (1, 2))

    return table(jnp.cos, 1.0), table(jnp.sin, -1.0)


def _block_diag(w):
    nb, bi, bj = w.shape
    on_diag = jnp.eye(nb, dtype=bool)[:, None, :, None]
    return jnp.where(on_diag, w[:, :, None, :], 0).reshape(nb * bi, nb * bj)


def kernel(x, c, ctx, c_ctx, norm_g, ada_w, ada_b, attn_w_in, attn_w_out, attn_sink, lam_q1, lam_k1,
           lam_q2, lam_k2, subln_g, rec_w_in, rec_conv_w, rec_conv_b, rec_wa, rec_ba, rec_wx, rec_bx,
           rec_lam, rec_w_out, final_g):
    assert x.shape[0] == 1 and c.shape[0] == 1 and ctx.shape[0] == 1
    bf = jnp.bfloat16
    d = D_MODEL
    xl = x[0]
    xc = ctx[0]
    n = xl.shape[0]

    cv = jnp.zeros((SUBLANES, d), jnp.float32).at[0].set(c[0]).at[1].set(c_ctx)
    mods = _adaln(cv, ada_w, ada_b)

    def mod(l, r):
        m = mods[l, r]
        return m[None, :d], m[None, d:2 * d], m[None, 2 * d:]

    lam_init = 0.8 - 0.6 * math.exp(-0.3 * 0)
    s1 = A_WIDTH
    s2 = s1 + A_KV_WIDTH
    s3 = s2 + A_KV_WIDTH
    s5 = s3 + 2 * B_WIDTH
    s6 = s5 + B_WIDTH
    head_perm = jnp.array([g * A_GROUP + j for j in range(A_GROUP) for g in range(A_KV_HEADS)])
    col_perm = (head_perm[:, None] * HEAD_DIM + jnp.arange(HEAD_DIM)).reshape(-1)
    w_in = attn_w_in[0]
    w_main = jnp.concatenate([w_in[:, :s1][:, col_perm], w_in[:, s1:s2], w_in[:, s3:s5],
                              w_in[:, s6:s6 + A_WIDTH][:, col_perm], w_in[:, s6 + A_WIDTH:]],
                             axis=1).astype(bf)
    w_vt = jnp.concatenate([w_in[:, s2:s3], w_in[:, s5:s6]], axis=1).T.astype(bf)
    w_out = attn_w_out[0]
    w_out = jnp.concatenate([w_out[:A_WIDTH][col_perm], w_out[A_WIDTH:]], axis=0).astype(bf)
    g0 = norm_g[0][None]
    cos, sin = _rope_tables(n)
    sh, sc, gt = mod(0, 0)
    shc, scc, gtc = mod(0, 1)
    qa, ka, qb, kb, vt, gate = _attn_in_proj(xl, g0, sc, sh, w_main, w_vt, cos, sin, rope=True)
    nc = xc.shape[0]
    qac, kac, qbc, kbc, vtc, gatec = _attn_in_proj(
        xc, g0, scc, shc, w_main, w_vt, cos[:nc], sin[:nc], rope=False)

    sink_gj = (attn_sink[0] * LOG2E).reshape(A_KV_HEADS, A_GROUP)
    kj = jnp.arange(3 * BLOCK)[:, None]
    qi = jnp.arange(BLOCK)[None, :]
    band_bias = jnp.where(jnp.abs(kj - BLOCK - qi) <= WINDOW, 0.0, NEG_INF).astype(jnp.float32)
    lamv = jnp.stack([lam_q1[0], lam_k1[0], lam_q2[0], lam_k2[0]])
    subln_col = subln_g[0][:, None]
    oa = _gqa_attention(jnp.repeat(sink_gj, BLOCK, axis=1), band_bias, qa, ka, vt, kac, vtc,
                        has_window=True)
    ob = _diff_attention(lamv, subln_col, qb, kb, vt, kbc, vtc, has_latent=True, lam_init=lam_init)
    oac = _gqa_attention(jnp.repeat(sink_gj, nc, axis=1), None, qac, None, None, kac, vtc,
                         has_window=False)
    obc = _diff_attention(lamv, subln_col, qbc, None, None, kbc, vtc, has_latent=False,
                          lam_init=lam_init)

    g1 = norm_g[1][None]
    sh, sc, gt1 = mod(1, 0)
    shc, scc, _ = mod(1, 1)
    w_rin = rec_w_in[0].astype(bf)
    xl, xr, gr = _attn_out_rec_in(oa, ob, gate, xl, gt, w_out, g1, sc, sh, w_rin)
    _, xrc, _ = _attn_out_rec_in(oac, obc, gatec, xc, gtc, w_out, g1, scc, shc, w_rin)
    gt = gt1
    cw = 0.5 * rec_conv_w[0]
    cb = 0.5 * rec_conv_b[0][None]
    zero_h = jnp.zeros((1, RNN_WIDTH), jnp.float32)

    def coeff(dr):
        hb = RNN_BLOCKS // 2
        wa, wx = rec_wa[0, dr].astype(bf), rec_wx[0, dr].astype(bf)
        w = jnp.stack([jnp.concatenate([_block_diag(wa[i * hb:(i + 1) * hb]),
                                        _block_diag(wx[i * hb:(i + 1) * hb])], axis=1)
                       for i in range(2)])
        bias = 0.5 * jnp.concatenate([rec_ba[0, dr], rec_bx[0, dr]])[None]
        return w, bias, rec_lam[0, dr][None]

    wf, bias_f, lam_f = coeff(0)
    wr, bias_r, lam_r = coeff(1)
    _, h0f = _rec_scan(xrc, cw, cb, wf, bias_f, lam_f, zero_h, reverse=False)
    _, h0r = _rec_scan(xrc, cw, cb, wr, bias_r, lam_r, zero_h, reverse=True)
    yr, _ = _rec_scan(xr, cw, cb, wr, bias_r, lam_r, h0r, reverse=True)
    out = _rec_scan(xr, cw, cb, wf, bias_f, lam_f, h0f, reverse=False,
                    fused=(yr, gr, xl, gt, rec_w_out[0].astype(bf), final_g[None]))
    return out[None]
```

```python
import functools
import math

import jax
import jax.numpy as jnp
from jax import lax
from jax.experimental import pallas as pl
from jax.experimental.pallas import tpu as pltpu

D_MODEL = 1024
GRID_W = 64
HEAD_DIM = 64
ROPE_BASE = 10000.0
EPS = 1e-6
NEG_INF = -1e30
A_Q_HEADS = 8
A_KV_HEADS = 2
A_GROUP = A_Q_HEADS // A_KV_HEADS
A_WIDTH = A_Q_HEADS * HEAD_DIM
A_KV_WIDTH = A_KV_HEADS * HEAD_DIM
WINDOW = 128
BLOCK = 128
B_HEADS = 4
B_WIDTH = B_HEADS * 2 * HEAD_DIM
ATTN_WIDTH = A_WIDTH + B_WIDTH
RNN_WIDTH = 1280
RNN_BLOCKS = 16
RNN_BLOCK_DIM = RNN_WIDTH // RNN_BLOCKS
CONV_WIDTH = 4
RG_C = 8.0
LOG2E = math.log2(math.e)
Q_SCALE_LOG2 = HEAD_DIM ** -0.5 * LOG2E

LANES = 128
SUBLANES = 8
VMEM_LIMIT = 48 * 1024 * 1024

ROW_TILE = 512
SCAN_TILE = 256
GQA_BLOCKS = 4
DIFF_TQ = 256
DIFF_TK = 512
DIFF_STAGES = 16
DIFF_JUMP_LIMIT = 64.0
SUM_ROWS = 16

_NT = (((1,), (1,)), ((), ()))


def _silu(x):
    return x * jax.nn.sigmoid(x)


def _norm_mod(x, g, sc, sh):
    ms = jnp.mean(x * x, axis=-1, keepdims=True)
    return (x * lax.rsqrt(ms + EPS) * g) * (1.0 + sc) + sh


def _params(*sem, flags=None):
    return pltpu.CompilerParams(dimension_semantics=sem, vmem_limit_bytes=VMEM_LIMIT, flags=flags)


def _adaln_kernel(cv_ref, w_ref, b_ref, o_ref):
    s = _silu(cv_ref[...])
    o_ref[0] = jnp.dot(s, w_ref[0], preferred_element_type=jnp.float32) + b_ref[0]


def _adaln(cv, ada_w, ada_b):
    depth, d, n3 = ada_w.shape
    tn = 1024
    return pl.pallas_call(
        _adaln_kernel,
        out_shape=jax.ShapeDtypeStruct((depth, SUBLANES, n3), jnp.float32),
        grid=(depth, n3 // tn),
        in_specs=[
            pl.BlockSpec((SUBLANES, d), lambda l, j: (0, 0)),
            pl.BlockSpec((1, d, tn), lambda l, j: (l, 0, j)),
            pl.BlockSpec((1, 1, tn), lambda l, j: (l, 0, j)),
        ],
        out_specs=pl.BlockSpec((1, SUBLANES, tn), lambda l, j: (l, 0, j)),
        compiler_params=_params("parallel", "parallel"),
        name="adaln",
    )(cv, ada_w, ada_b.reshape(depth, 1, n3))


def _rope(x, cos, sin_signed, first_half):
    rot = jnp.where(first_half, pltpu.roll(x, LANES - 16, 1), pltpu.roll(x, 16, 1))
    return x * cos + rot * sin_signed


def _attn_in_kernel(x_ref, g_ref, sc_ref, sh_ref, w_ref, wvt_ref, cos_ref, sin_ref,
                    qa_ref, ka_ref, qb_ref, kb_ref, vt_ref, gate_ref, *, rope):
    h = _norm_mod(x_ref[...], g_ref[...], sc_ref[...], sh_ref[...]).astype(jnp.bfloat16)
    p = jnp.dot(h, w_ref[...], preferred_element_type=jnp.float32)
    vt_ref[...] = lax.dot_general(wvt_ref[...], h, _NT,
                                  preferred_element_type=jnp.float32).astype(jnp.bfloat16)
    if rope:
        cos = cos_ref[...]
        sin = sin_ref[...]
        lane = lax.broadcasted_iota(jnp.int32, cos.shape, 1)
        first_half = (lane % 32) < 16

    def chunk(c):
        x = p[:, c * LANES:(c + 1) * LANES]
        return _rope(x, cos, sin, first_half) if rope else x

    o0 = A_WIDTH // LANES
    o1 = o0 + A_KV_WIDTH // LANES
    o2 = o1 + B_WIDTH // LANES
    o3 = o2 + B_WIDTH // LANES
    for c in range(o0):
        qa_ref[:, c * LANES:(c + 1) * LANES] = (chunk(c) * Q_SCALE_LOG2).astype(jnp.bfloat16)
    for c in range(o0, o1):
        ka_ref[:, (c - o0) * LANES:(c - o0 + 1) * LANES] = chunk(c).astype(jnp.bfloat16)
    for c in range(o1, o2):
        qb_ref[:, (c - o1) * LANES:(c - o1 + 1) * LANES] = (chunk(c) * Q_SCALE_LOG2).astype(jnp.bfloat16)
    for c in range(o2, o3):
        kb_ref[:, (c - o2) * LANES:(c - o2 + 1) * LANES] = chunk(c).astype(jnp.bfloat16)
    gate_ref[...] = p[:, o3 * LANES:].astype(jnp.bfloat16)


def _attn_in_proj(x, g, sc, sh, w, wvt, cos, sin, *, rope):
    n, d = x.shape
    tm = min(ROW_TILE, n)
    nw = w.shape[1]
    nv = wvt.shape[0]
    row = lambda i: (i, 0)
    fixed = lambda i: (0, 0)
    bf = jnp.bfloat16
    return pl.pallas_call(
        functools.partial(_attn_in_kernel, rope=rope),
        out_shape=[
            jax.ShapeDtypeStruct((n, A_WIDTH), bf),
            jax.ShapeDtypeStruct((n, A_KV_WIDTH), bf),
            jax.ShapeDtypeStruct((n, B_WIDTH), bf),
            jax.ShapeDtypeStruct((n, B_WIDTH), bf),
            jax.ShapeDtypeStruct((nv, n), bf),
            jax.ShapeDtypeStruct((n, ATTN_WIDTH), bf),
        ],
        grid=(n // tm,),
        in_specs=[
            pl.BlockSpec((tm, d), row),
            pl.BlockSpec((1, d), fixed),
            pl.BlockSpec((1, d), fixed),
            pl.BlockSpec((1, d), fixed),
            pl.BlockSpec((d, nw), fixed),
            pl.BlockSpec((nv, d), fixed),
            pl.BlockSpec((tm, LANES), row),
            pl.BlockSpec((tm, LANES), row),
        ],
        out_specs=[
            pl.BlockSpec((tm, A_WIDTH), row),
            pl.BlockSpec((tm, A_KV_WIDTH), row),
            pl.BlockSpec((tm, B_WIDTH), row),
            pl.BlockSpec((tm, B_WIDTH), row),
            pl.BlockSpec((nv, tm), lambda i: (0, i)),
            pl.BlockSpec((tm, ATTN_WIDTH), row),
        ],
        compiler_params=_params("parallel"),
        name="attn_in_proj",
    )(x, g, sc, sh, w, wvt, cos, sin)


def _gqa_scores(q, keys):
    tq = q.shape[0]
    lane = lax.broadcasted_iota(jnp.int32, (tq, LANES), 1)
    zero = jnp.zeros((tq, LANES), q.dtype)
    out = []
    for g in range(A_KV_HEADS):
        in_g = (lane >= g * HEAD_DIM) & (lane < (g + 1) * HEAD_DIM)
        qg = jnp.concatenate([jnp.where(in_g, q[:, j * LANES:(j + 1) * LANES], zero)
                              for j in range(A_GROUP)], axis=0)
        out.append(lax.dot_general(keys, qg, _NT, preferred_element_type=jnp.float32))
    return out


def _gqa_outputs(scores, vt, bias, sink_ref, tq):
    outs = [None] * A_Q_HEADS
    for g in range(A_KV_HEADS):
        s = scores[g]
        if bias is not None:
            s = jnp.concatenate([s[:3 * BLOCK] + bias, s[3 * BLOCK:]], axis=0)
        sk = sink_ref[g:g + 1, :]
        m = jnp.maximum(jnp.max(s, axis=0, keepdims=True), sk)
        acc = jnp.dot(vt, jnp.exp2((s - m).astype(jnp.bfloat16)),
                      preferred_element_type=jnp.float32)
        l = acc[A_KV_WIDTH:A_KV_WIDTH + 1] + jnp.exp2(sk - m)
        o = acc[g * HEAD_DIM:(g + 1) * HEAD_DIM] / l
        for j in range(A_GROUP):
            outs[2 * j + g] = o[:, j * tq:(j + 1) * tq]
    return jnp.concatenate(outs, axis=0).T


def _gqa_window_kernel(sink_ref, bias_ref, q_ref, *refs, nb):
    nk = GQA_BLOCKS + 2
    k_refs, vt_refs = refs[:nk], refs[nk:2 * nk]
    kx_ref, vxt_ref, o_ref = refs[2 * nk:]
    i = pl.program_id(0)
    ones = jnp.ones((SUM_ROWS, 3 * BLOCK + kx_ref.shape[0]), jnp.bfloat16)
    row = lax.broadcasted_iota(jnp.int32, (3 * BLOCK, 1), 0)
    all_scores, vts, biases = [], [], []
    for t in range(GQA_BLOCKS):
        keys = jnp.concatenate([r[...] for r in k_refs[t:t + 3]] + [kx_ref[...]], axis=0)
        vt = jnp.concatenate([r[...] for r in vt_refs[t:t + 3]] + [vxt_ref[...]], axis=1)
        vts.append(jnp.concatenate([vt, ones], axis=0))
        all_scores.append(_gqa_scores(q_ref[t * BLOCK:(t + 1) * BLOCK, :], keys))
        b = GQA_BLOCKS * i + t
        off_end = ((row < BLOCK) & (b == 0)) | ((row >= 2 * BLOCK) & (b == nb - 1))
        bias = bias_ref[...] + jnp.where(off_end, NEG_INF, 0.0)
        biases.append(jnp.concatenate([bias] * A_GROUP, axis=1))
    for t in range(GQA_BLOCKS):
        o = _gqa_outputs(all_scores[t], vts[t], biases[t], sink_ref, BLOCK)
        o_ref[t * BLOCK:(t + 1) * BLOCK, :] = o.astype(o_ref.dtype)


def _gqa_ctx_kernel(sink_ref, q_ref, kx_ref, vxt_ref, o_ref):
    vt = vxt_ref[...]
    vt = jnp.concatenate([vt, jnp.ones((SUM_ROWS, vt.shape[1]), vt.dtype)], axis=0)
    q = q_ref[...]
    o = _gqa_outputs(_gqa_scores(q, kx_ref[...]), vt, None, sink_ref, q.shape[0])
    o_ref[...] = o.astype(o_ref.dtype)


def _gqa_attention(sink_cols, bias, q, k, vt, kx, vxt, *, has_window):
    n = q.shape[0]
    nx = kx.shape[0]
    fixed = lambda i: (0, 0)
    if has_window:
        nb = n // BLOCK
        tq = GQA_BLOCKS * BLOCK
        steps = nb // GQA_BLOCKS
        assert steps * GQA_BLOCKS == nb
        offs = range(-1, GQA_BLOCKS + 1)
        blk = lambda i, o: jnp.clip(GQA_BLOCKS * i + o, 0, nb - 1)
        k_specs = [pl.BlockSpec((BLOCK, A_KV_WIDTH), lambda i, o=o: (blk(i, o), 0)) for o in offs]
        vt_specs = [pl.BlockSpec((A_KV_WIDTH, BLOCK), lambda i, o=o: (0, blk(i, o))) for o in offs]
        in_specs = ([pl.BlockSpec((A_KV_HEADS, A_GROUP * BLOCK), fixed),
                     pl.BlockSpec((3 * BLOCK, BLOCK), fixed),
                     pl.BlockSpec((tq, A_WIDTH), lambda i: (i, 0))] + k_specs + vt_specs +
                    [pl.BlockSpec((nx, A_KV_WIDTH), fixed), pl.BlockSpec((A_KV_WIDTH, nx), fixed)])
        args = (sink_cols, bias, q) + (k,) * len(offs) + (vt,) * len(offs) + (kx, vxt)
        body = functools.partial(_gqa_window_kernel, nb=nb)
    else:
        steps = 1
        tq = n
        in_specs = [pl.BlockSpec((A_KV_HEADS, A_GROUP * tq), fixed), pl.BlockSpec((tq, A_WIDTH), fixed),
                    pl.BlockSpec((nx, A_KV_WIDTH), fixed), pl.BlockSpec((A_KV_WIDTH, nx), fixed)]
        args = (sink_cols, q, kx, vxt)
        body = _gqa_ctx_kernel
    return pl.pallas_call(
        body,
        out_shape=jax.ShapeDtypeStruct((n, A_WIDTH), jnp.bfloat16),
        grid=(steps,),
        in_specs=in_specs,
        out_specs=pl.BlockSpec((tq, A_WIDTH), lambda i: (i, 0)),
        compiler_params=_params("parallel"),
        name="gqa_window" if has_window else "gqa_ctx",
    )(*args)


def _diff_masked_q(q_ref):
    q = q_ref[...]
    lane = lax.broadcasted_iota(jnp.int32, q.shape, 1)
    zero = jnp.zeros_like(q)
    return jnp.concatenate([jnp.where(lane < HEAD_DIM, q, zero),
                            jnp.where(lane >= HEAD_DIM, q, zero)], axis=0)


def _diff_probs(s_rel):
    return jnp.exp2(s_rel.astype(jnp.bfloat16))


def _diff_pv(vt, p):
    ones = jnp.ones((SUM_ROWS, vt.shape[1]), vt.dtype)
    return jnp.dot(jnp.concatenate([vt, ones], axis=0), p, preferred_element_type=jnp.float32)


def _diff_ctx_step(qz, kx, vxt):
    s = lax.dot_general(kx, qz, _NT, preferred_element_type=jnp.float32)
    m = jnp.max(s, axis=0, keepdims=True)
    return m, _diff_pv(vxt, _diff_probs(s - m))


def _diff_finalize(lam_ref, g_ref, acc, lam_init):
    hw = 2 * HEAD_DIM
    tq = acc.shape[1] // 2
    lv = lam_ref[...]
    lam = (jnp.exp(jnp.sum(lv[0:1] * lv[1:2], keepdims=True))
           - jnp.exp(jnp.sum(lv[2:3] * lv[3:4], keepdims=True)) + lam_init)
    on = acc[:hw] / acc[hw:hw + 1]
    o = on[:, :tq] - lam * on[:, tq:]
    ms = jnp.mean(o * o, axis=0, keepdims=True)
    o = (o * lax.rsqrt(ms + EPS)) * g_ref[...] * (1.0 - lam_init)
    return o.T.astype(jnp.bfloat16)


def _diff_ctx_kernel(lam_ref, g_ref, q_ref, kx_ref, vxt_ref, o_ref, *, lam_init):
    _, acc = _diff_ctx_step(_diff_masked_q(q_ref), kx_ref[...], vxt_ref[...])
    o_ref[...] = _diff_finalize(lam_ref, g_ref, acc, lam_init)


def _bf16_round(x):
    return x.astype(jnp.bfloat16).astype(jnp.float32)


def _diff_kernel(lam_ref, g_ref, q_ref, k_ref, vt_ref, kx_ref, vxt_ref, o_ref,
                 p0, p1, p2, p3, acc_ref, *, lam_init, n_chunks):
    tk = DIFF_TK
    qz = _diff_masked_q(q_ref)
    cols = qz.shape[0]
    qzt = qz.astype(jnp.float32).T.astype(jnp.bfloat16)
    ones_k = jnp.ones((tk, LANES), jnp.bfloat16)
    first_row = lax.broadcasted_iota(jnp.int32, (2 * SUBLANES, cols), 0) == 0
    zero_rows = jnp.zeros((LANES - 2 * SUBLANES, cols), jnp.bfloat16)
    p_buf = (p0, p1, p2, p3)

    def rhs_for(r):
        neg_r = jnp.where(first_row, -r, 0.0).astype(jnp.bfloat16)
        return jnp.concatenate([qzt, neg_r, zero_rows], axis=0)

    def rel_scores(c, rhs):
        k = k_ref[pl.ds(pl.multiple_of(c * tk, tk), tk), :]
        return jnp.dot(jnp.concatenate([k, ones_k], axis=1), rhs, preferred_element_type=jnp.float32)

    def pv(c, p_ref):
        return _diff_pv(vt_ref[:, pl.ds(pl.multiple_of(c * tk, tk), tk)], p_ref[...])

    def col_max(sa, sb):
        return jnp.maximum(jnp.max(sa, axis=0, keepdims=True), jnp.max(sb, axis=0, keepdims=True))

    s_ctx = lax.dot_general(kx_ref[...], qz, _NT, preferred_element_type=jnp.float32)
    m_ctx = jnp.max(s_ctx, axis=0, keepdims=True)
    r0 = _bf16_round(m_ctx)
    rhs0 = rhs_for(r0)
    sa, sb = rel_scores(0, rhs0), rel_scores(1, rhs0)
    acc_ref[...] = _diff_pv(vxt_ref[...], _diff_probs(s_ctx - r0))
    worst = col_max(sa, sb)
    p_buf[0][...] = _diff_probs(sa)
    p_buf[1][...] = _diff_probs(sb)

    def body(j, carry):
        run_max, pend, r_prev, r_cur, worst = carry
        for i in range(0, DIFF_STAGES, 2):
            c = DIFF_STAGES * j + i
            r_next = _bf16_round(run_max)
            rhs = rhs_for(r_next)
            pva = pv(c, p_buf[i % 4])
            sa = rel_scores(jnp.minimum(c + 2, n_chunks - 1), rhs)
            pvb = pv(c + 1, p_buf[(i + 1) % 4])
            sb = rel_scores(jnp.minimum(c + 3, n_chunks - 1), rhs)
            acc_ref[...] = jnp.exp2(r_prev - r_cur) * acc_ref[...] + pva + pvb
            top = col_max(sa, sb)
            p_buf[(i + 2) % 4][...] = _diff_probs(sa)
            p_buf[(i + 3) % 4][...] = _diff_probs(sb)
            worst = jnp.maximum(worst, top)
            run_max = jnp.maximum(run_max, pend)
            pend = top + r_next
            r_prev, r_cur = r_cur, r_next
        return run_max, pend, r_prev, r_cur, worst

    carry = lax.fori_loop(0, n_chunks // DIFF_STAGES, body, (m_ctx, worst + r0, r0, r0, worst))
    o_ref[...] = _diff_finalize(lam_ref, g_ref, acc_ref[...], lam_init)

    @pl.when(jnp.max(carry[4]) > DIFF_JUMP_LIMIT)
    def _():
        m0, acc0 = _diff_ctx_step(qz, kx_ref[...], vxt_ref[...])

        def exact(c, carry):
            m, acc = carry
            off = pl.multiple_of(c * tk, tk)
            s = lax.dot_general(k_ref[pl.ds(off, tk), :], qz, _NT, preferred_element_type=jnp.float32)
            m_new = jnp.maximum(m, jnp.max(s, axis=0, keepdims=True))
            acc = jnp.exp2(m - m_new) * acc + _diff_pv(vt_ref[:, pl.ds(off, tk)], _diff_probs(s - m_new))
            return m_new, acc

        _, acc = lax.fori_loop(0, n_chunks, exact, (m0, acc0))
        o_ref[...] = _diff_finalize(lam_ref, g_ref, acc, lam_init)


def _diff_attention(lamv, subln_col, q, k, vt, kx, vxt, *, has_latent, lam_init):
    n = q.shape[0]
    nx = kx.shape[0]
    tq = min(DIFF_TQ, n)
    hw = 2 * HEAD_DIM
    fixed = lambda h, i: (0, 0)
    in_specs = [pl.BlockSpec((4, HEAD_DIM), fixed), pl.BlockSpec((hw, 1), fixed),
                pl.BlockSpec((tq, hw), lambda h, i: (i, h))]
    args = [lamv, subln_col, q]
    scratch = []
    if has_latent:
        nk = k.shape[0]
        n_chunks = nk // DIFF_TK
        assert DIFF_STAGES % 4 == 0 and n_chunks % DIFF_STAGES == 0 and n_chunks * DIFF_TK == nk
        in_specs += [pl.BlockSpec((nk, hw), lambda h, i: (0, h)),
                     pl.BlockSpec((hw, nk), lambda h, i: (h + 1, 0))]
        args += [k, vt]
        body = functools.partial(_diff_kernel, lam_init=lam_init, n_chunks=n_chunks)
        scratch = ([pltpu.VMEM((DIFF_TK, 2 * tq), jnp.bfloat16)] * 4 +
                   [pltpu.VMEM((hw + SUM_ROWS, 2 * tq), jnp.float32)])
    else:
        body = functools.partial(_diff_ctx_kernel, lam_init=lam_init)
    in_specs += [pl.BlockSpec((nx, hw), lambda h, i: (0, h)),
                 pl.BlockSpec((hw, nx), lambda h, i: (h + 1, 0))]
    args += [kx, vxt]
    return pl.pallas_call(
        body,
        out_shape=jax.ShapeDtypeStruct((n, B_WIDTH), jnp.bfloat16),
        grid=(B_HEADS, n // tq),
        in_specs=in_specs,
        out_specs=pl.BlockSpec((tq, hw), lambda h, i: (i, h)),
        scratch_shapes=scratch,
        compiler_params=_params("parallel", "parallel"),
        name="diff_attn" if has_latent else "diff_attn_ctx",
    )(*args)


def _attn_out_rec_in_kernel(oa_ref, ob_ref, gate_ref, x_ref, gt_ref, w_ref,
                            g_ref, sc_ref, sh_ref, wr_ref, xl_ref, xr_ref, gr_ref):
    o = jnp.concatenate([oa_ref[...], ob_ref[...]], axis=-1).astype(jnp.float32)
    a = o * _silu(gate_ref[...].astype(jnp.float32))
    y = jnp.dot(a.astype(jnp.bfloat16), w_ref[...], preferred_element_type=jnp.float32)
    xl = x_ref[...] + gt_ref[...] * y
    xl_ref[...] = xl
    h = _norm_mod(xl, g_ref[...], sc_ref[...], sh_ref[...]).astype(jnp.bfloat16)
    p = jnp.dot(h, wr_ref[...], preferred_element_type=jnp.float32)
    xr_ref[...] = p[:, :RNN_WIDTH]
    gr_ref[...] = p[:, RNN_WIDTH:]


def _attn_out_rec_in(oa, ob, gate, x, gt, w, g, sc, sh, wr):
    n, d = x.shape
    tm = min(ROW_TILE, n)
    row = lambda i: (i, 0)
    fixed = lambda i: (0, 0)
    vec = pl.BlockSpec((1, d), fixed)
    return pl.pallas_call(
        _attn_out_rec_in_kernel,
        out_shape=[jax.ShapeDtypeStruct((n, d), jnp.float32),
                   jax.ShapeDtypeStruct((n, RNN_WIDTH), jnp.float32),
                   jax.ShapeDtypeStruct((n, RNN_WIDTH), jnp.float32)],
        grid=(n // tm,),
        in_specs=[
            pl.BlockSpec((tm, A_WIDTH), row),
            pl.BlockSpec((tm, B_WIDTH), row),
            pl.BlockSpec((tm, ATTN_WIDTH), row),
            pl.BlockSpec((tm, d), row),
            vec,
            pl.BlockSpec((ATTN_WIDTH, d), fixed),
            vec, vec, vec,
            pl.BlockSpec((d, 2 * RNN_WIDTH), fixed),
        ],
        out_specs=[pl.BlockSpec((tm, d), row), pl.BlockSpec((tm, RNN_WIDTH), row),
                   pl.BlockSpec((tm, RNN_WIDTH), row)],
        compiler_params=_params("parallel"),
        name="attn_out_rec_in",
    )(oa, ob, gate, x, gt, w, g, sc, sh, wr)


def _rec_scan_kernel(*refs, reverse, fuse_out, nt):
    if fuse_out:
        (x_ref, xp_ref, xn_ref, cw_ref, cb_ref, w_ref, bias_ref, lam_ref, h0_ref,
         yr_ref, gr_ref, xl_ref, gt_ref, wo_ref, fg_ref,
         o_ref, xe_ref, a_ref, b_ref, y_ref, h_ref) = refs
    else:
        (x_ref, xp_ref, xn_ref, cw_ref, cb_ref, w_ref, bias_ref, lam_ref, h0_ref,
         y_ref, hf_ref, xe_ref, a_ref, b_ref, h_ref) = refs
    i = pl.program_id(0)
    pos = i if not reverse else nt - 1 - i
    tm = x_ref.shape[0]

    @pl.when(i == 0)
    def _():
        h_ref[...] = h0_ref[...]

    xe_ref[0:SUBLANES, :] = jnp.where(pos > 0, xp_ref[...], 0.0)
    xe_ref[SUBLANES:SUBLANES + tm, :] = x_ref[...]
    xe_ref[SUBLANES + tm:, :] = jnp.where(pos < nt - 1, xn_ref[...], 0.0)
    cw = cw_ref[...]
    xe = xe_ref[...]
    taps = pltpu.roll(xe * cw[0:1], 1, 0)
    taps = pltpu.roll(taps + xe * cw[1:2], 1, 0)
    taps = taps + xe * cw[2:3] + pltpu.roll(xe * cw[3:4], tm + 2 * SUBLANES - 1, 0)
    uh = taps[SUBLANES:SUBLANES + tm] + cb_ref[...]
    ub = uh.astype(jnp.bfloat16)
    half = RNN_WIDTH // 2
    z0 = jnp.dot(ub[:, :half], w_ref[0], preferred_element_type=jnp.float32)
    z1 = jnp.dot(ub[:, half:], w_ref[1], preferred_element_type=jnp.float32)
    bias = bias_ref[...]
    ta = jnp.tanh(jnp.concatenate([z0[:, :half], z1[:, :half]], axis=1) + bias[:, :RNN_WIDTH])
    tx = jnp.tanh(jnp.concatenate([z0[:, half:], z1[:, half:]], axis=1) + bias[:, RNN_WIDTH:])
    nl = -lam_ref[...]
    softplus = jnp.maximum(nl, 0.0) + jnp.log1p(jnp.exp(-jnp.abs(nl)))
    coef = (-0.5 * RG_C * LOG2E) * softplus
    a = jnp.exp2(coef + coef * ta)
    a_ref[...] = a
    om = 1.0 - a * a
    b_ref[...] = (om * lax.rsqrt(jnp.maximum(om, 1e-30))) * ((1.0 + tx) * uh)

    def row(s, h):
        rr = (tm - 1 - s) if reverse else s
        h = a_ref[pl.ds(rr, 1), :] * h + b_ref[pl.ds(rr, 1), :]
        y_ref[pl.ds(rr, 1), :] = h
        return h

    h = lax.fori_loop(0, tm, row, h_ref[...], unroll=8)
    h_ref[...] = h

    if fuse_out:
        y = (y_ref[...] + yr_ref[...]) * _silu(gr_ref[...])
        out = jnp.dot(y.astype(jnp.bfloat16), wo_ref[...], preferred_element_type=jnp.float32)
        xl = xl_ref[...] + gt_ref[...] * out
        ms = jnp.mean(xl * xl, axis=-1, keepdims=True)
        o_ref[...] = xl * lax.rsqrt(ms + EPS) * fg_ref[...]
    else:
        hf_ref[...] = h


def _rec_scan(xr, cw, cb, w, bias, lam, h0, *, reverse, fused=None):
    assert CONV_WIDTH == 4
    n = xr.shape[0]
    tm = min(SCAN_TILE, n)
    nt = n // tm
    per = tm // SUBLANES
    nblk8 = n // SUBLANES
    pos = (lambda i: nt - 1 - i) if reverse else (lambda i: i)
    row = lambda i: (pos(i), 0)
    prev = lambda i: (jnp.maximum(pos(i) * per - 1, 0), 0)
    nxt = lambda i: (jnp.minimum((pos(i) + 1) * per, nblk8 - 1), 0)
    fixed = lambda i: (0, 0)
    vec = pl.BlockSpec((1, RNN_WIDTH), fixed)
    in_specs = [
        pl.BlockSpec((tm, RNN_WIDTH), row),
        pl.BlockSpec((SUBLANES, RNN_WIDTH), prev),
        pl.BlockSpec((SUBLANES, RNN_WIDTH), nxt),
        pl.BlockSpec((CONV_WIDTH, RNN_WIDTH), fixed),
        vec,
        pl.BlockSpec((2, RNN_WIDTH // 2, RNN_WIDTH), lambda i: (0, 0, 0)),
        pl.BlockSpec((1, 2 * RNN_WIDTH), fixed),
        vec,
        vec,
    ]
    args = [xr, xr, xr, cw, cb, w, bias, lam, h0]
    scratch = [pltpu.VMEM((tm + 2 * SUBLANES, RNN_WIDTH), jnp.float32),
               pltpu.VMEM((tm, RNN_WIDTH), jnp.float32),
               pltpu.VMEM((tm, RNN_WIDTH), jnp.float32)]
    if fused is not None:
        yr, gr, xl, gt, wo, fg = fused
        d = xl.shape[1]
        in_specs += [
            pl.BlockSpec((tm, RNN_WIDTH), row),
            pl.BlockSpec((tm, RNN_WIDTH), row),
            pl.BlockSpec((tm, d), row),
            pl.BlockSpec((1, d), fixed),
            pl.BlockSpec((RNN_WIDTH, d), fixed),
            pl.BlockSpec((1, d), fixed),
        ]
        args += [yr, gr, xl, gt, wo, fg]
        out_shape = jax.ShapeDtypeStruct((n, d), jnp.float32)
        out_specs = pl.BlockSpec((tm, d), row)
        scratch = scratch + [pltpu.VMEM((tm, RNN_WIDTH), jnp.float32)]
    else:
        out_shape = [jax.ShapeDtypeStruct((n, RNN_WIDTH), jnp.float32),
                     jax.ShapeDtypeStruct((1, RNN_WIDTH), jnp.float32)]
        out_specs = [pl.BlockSpec((tm, RNN_WIDTH), row), vec]
    scratch = scratch + [pltpu.VMEM((1, RNN_WIDTH), jnp.float32)]
    return pl.pallas_call(
        functools.partial(_rec_scan_kernel, reverse=reverse, fuse_out=fused is not None, nt=nt),
        out_shape=out_shape,
        grid=(nt,),
        in_specs=in_specs,
        out_specs=out_specs,
        scratch_shapes=scratch,
        compiler_params=_params("arbitrary"),
        name="rec_scan_out" if fused is not None else ("rec_scan_rev" if reverse else "rec_scan_fwd"),
    )(*args)


def _rope_tables(n):
    rows = n // GRID_W
    n_freq = HEAD_DIM // 4
    inv_freq = ROPE_BASE ** (-jnp.arange(n_freq, dtype=jnp.float32) / n_freq)
    ar = jnp.arange(rows, dtype=jnp.float32)[:, None] * inv_freq
    ac = jnp.arange(GRID_W, dtype=jnp.float32)[:, None] * inv_freq

    def table(fn, sign):
        r = jnp.broadcast_to(fn(ar)[:, None, :], (rows, GRID_W, n_freq)).reshape(n, n_freq)
        c = jnp.broadcast_to(fn(ac)[None, :, :], (rows, GRID_W, n_freq)).reshape(n, n_freq)
        head = jnp.concatenate([sign * r, r, sign * c, c], axis=-1)
        return jnp.tile(head, (1, 2))

    return table(jnp.cos, 1.0), table(jnp.sin, -1.0)


def _block_diag(w):
    nb, bi, bj = w.shape
    on_diag = jnp.eye(nb, dtype=bool)[:, None, :, None]
    return jnp.where(on_diag, w[:, :, None, :], 0).reshape(nb * bi, nb * bj)


def kernel(x, c, ctx, c_ctx, norm_g, ada_w, ada_b, attn_w_in, attn_w_out, attn_sink, lam_q1, lam_k1,
           lam_q2, lam_k2, subln_g, rec_w_in, rec_conv_w, rec_conv_b, rec_wa, rec_ba, rec_wx, rec_bx,
           rec_lam, rec_w_out, final_g):
    assert x.shape[0] == 1 and c.shape[0] == 1 and ctx.shape[0] == 1
    bf = jnp.bfloat16
    d = D_MODEL
    xl = x[0]
    xc = ctx[0]
    n = xl.shape[0]

    cv = jnp.zeros((SUBLANES, d), jnp.float32).at[0].set(c[0]).at[1].set(c_ctx)
    mods = _adaln(cv, ada_w, ada_b)

    def mod(l, r):
        m = mods[l, r]
        return m[None, :d], m[None, d:2 * d], m[None, 2 * d:]

    lam_init = 0.8 - 0.6 * math.exp(-0.3 * 0)
    s1 = A_WIDTH
    s2 = s1 + A_KV_WIDTH
    s3 = s2 + A_KV_WIDTH
    s5 = s3 + 2 * B_WIDTH
    s6 = s5 + B_WIDTH
    head_perm = jnp.array([g * A_GROUP + j for j in range(A_GROUP) for g in range(A_KV_HEADS)])
    col_perm = (head_perm[:, None] * HEAD_DIM + jnp.arange(HEAD_DIM)).reshape(-1)
    w_in = attn_w_in[0]
    w_main = jnp.concatenate([w_in[:, :s1][:, col_perm], w_in[:, s1:s2], w_in[:, s3:s5],
                              w_in[:, s6:s6 + A_WIDTH][:, col_perm], w_in[:, s6 + A_WIDTH:]],
                             axis=1).astype(bf)
    w_vt = jnp.concatenate([w_in[:, s2:s3], w_in[:, s5:s6]], axis=1).T.astype(bf)
    w_out = attn_w_out[0]
    w_out = jnp.concatenate([w_out[:A_WIDTH][col_perm], w_out[A_WIDTH:]], axis=0).astype(bf)
    g0 = norm_g[0][None]
    cos, sin = _rope_tables(n)
    sh, sc, gt = mod(0, 0)
    shc, scc, gtc = mod(0, 1)
    qa, ka, qb, kb, vt, gate = _attn_in_proj(xl, g0, sc, sh, w_main, w_vt, cos, sin, rope=True)
    nc = xc.shape[0]
    qac, kac, qbc, kbc, vtc, gatec = _attn_in_proj(
        xc, g0, scc, shc, w_main, w_vt, cos[:nc], sin[:nc], rope=False)

    sink_gj = (attn_sink[0] * LOG2E).reshape(A_KV_HEADS, A_GROUP)
    kj = jnp.arange(3 * BLOCK)[:, None]
    qi = jnp.arange(BLOCK)[None, :]
    band_bias = jnp.where(jnp.abs(kj - BLOCK - qi) <= WINDOW, 0.0, NEG_INF).astype(jnp.float32)
    lamv = jnp.stack([lam_q1[0], lam_k1[0], lam_q2[0], lam_k2[0]])
    subln_col = subln_g[0][:, None]
    oa = _gqa_attention(jnp.repeat(sink_gj, BLOCK, axis=1), band_bias, qa, ka, vt, kac, vtc,
                        has_window=True)
    ob = _diff_attention(lamv, subln_col, qb, kb, vt, kbc, vtc, has_latent=True, lam_init=lam_init)
    oac = _gqa_attention(jnp.repeat(sink_gj, nc, axis=1), None, qac, None, None, kac, vtc,
                         has_window=False)
    obc = _diff_attention(lamv, subln_col, qbc, None, None, kbc, vtc, has_latent=False,
                          lam_init=lam_init)

    g1 = norm_g[1][None]
    sh, sc, gt1 = mod(1, 0)
    shc, scc, _ = mod(1, 1)
    w_rin = rec_w_in[0].astype(bf)
    xl, xr, gr = _attn_out_rec_in(oa, ob, gate, xl, gt, w_out, g1, sc, sh, w_rin)
    _, xrc, _ = _attn_out_rec_in(oac, obc, gatec, xc, gtc, w_out, g1, scc, shc, w_rin)
    gt = gt1
    cw = 0.5 * rec_conv_w[0]
    cb = 0.5 * rec_conv_b[0][None]
    zero_h = jnp.zeros((1, RNN_WIDTH), jnp.float32)

    def coeff(dr):
        hb = RNN_BLOCKS // 2
        wa, wx = rec_wa[0, dr].astype(bf), rec_wx[0, dr].astype(bf)
        w = jnp.stack([jnp.concatenate([_block_diag(wa[i * hb:(i + 1) * hb]),
                                        _block_diag(wx[i * hb:(i + 1) * hb])], axis=1)
                       for i in range(2)])
        bias = 0.5 * jnp.concatenate([rec_ba[0, dr], rec_bx[0, dr]])[None]
        return w, bias, rec_lam[0, dr][None]

    wf, bias_f, lam_f = coeff(0)
    wr, bias_r, lam_r = coeff(1)
    _, h0f = _rec_scan(xrc, cw, cb, wf, bias_f, lam_f, zero_h, reverse=False)
    _, h0r = _rec_scan(xrc, cw, cb, wr, bias_r, lam_r, zero_h, reverse=True)
    yr, _ = _rec_scan(xr, cw, cb, wr, bias_r, lam_r, h0r, reverse=True)
    out = _rec_scan(xr, cw, cb, wf, bias_f, lam_f, h0f, reverse=False,
                    fused=(yr, gr, xl, gt, rec_w_out[0].astype(bf), final_g[None]))
    return out[None]
```

```python
import functools
import math

import jax
import jax.numpy as jnp
from jax import lax
from jax.experimental import pallas as pl
from jax.experimental.pallas import tpu as pltpu

D_MODEL = 1024
GRID_W = 64
HEAD_DIM = 64
ROPE_BASE = 10000.0
EPS = 1e-6
NEG_INF = -1e30
A_Q_HEADS = 8
A_KV_HEADS = 2
A_GROUP = A_Q_HEADS // A_KV_HEADS
A_WIDTH = A_Q_HEADS * HEAD_DIM
A_KV_WIDTH = A_KV_HEADS * HEAD_DIM
WINDOW = 128
BLOCK = 128
B_HEADS = 4
B_WIDTH = B_HEADS * 2 * HEAD_DIM
ATTN_WIDTH = A_WIDTH + B_WIDTH
RNN_WIDTH = 1280
RNN_BLOCKS = 16
RNN_BLOCK_DIM = RNN_WIDTH // RNN_BLOCKS
CONV_WIDTH = 4
RG_C = 8.0
LOG2E = math.log2(math.e)
Q_SCALE_LOG2 = HEAD_DIM ** -0.5 * LOG2E

LANES = 128
SUBLANES = 8
VMEM_LIMIT = 48 * 1024 * 1024

ROW_TILE = 512
SCAN_TILE = 256
GQA_BLOCKS = 4
DIFF_TQ = 256
DIFF_TK = 512
DIFF_STAGES = 16
DIFF_JUMP_LIMIT = 64.0
SUM_ROWS = 16

_NT = (((1,), (1,)), ((), ()))


def _silu(x):
    return x * jax.nn.sigmoid(x)


def _norm_mod(x, g, sc, sh):
    ms = jnp.mean(x * x, axis=-1, keepdims=True)
    return (x * lax.rsqrt(ms + EPS) * g) * (1.0 + sc) + sh


def _params(*sem, flags=None):
    return pltpu.CompilerParams(dimension_semantics=sem, vmem_limit_bytes=VMEM_LIMIT, flags=flags)


def _adaln_kernel(cv_ref, w_ref, b_ref, o_ref):
    s = _silu(cv_ref[...])
    o_ref[0] = jnp.dot(s, w_ref[0], preferred_element_type=jnp.float32) + b_ref[0]


def _adaln(cv, ada_w, ada_b):
    depth, d, n3 = ada_w.shape
    tn = 1024
    return pl.pallas_call(
        _adaln_kernel,
        out_shape=jax.ShapeDtypeStruct((depth, SUBLANES, n3), jnp.float32),
        grid=(depth, n3 // tn),
        in_specs=[
            pl.BlockSpec((SUBLANES, d), lambda l, j: (0, 0)),
            pl.BlockSpec((1, d, tn), lambda l, j: (l, 0, j)),
            pl.BlockSpec((1, 1, tn), lambda l, j: (l, 0, j)),
        ],
        out_specs=pl.BlockSpec((1, SUBLANES, tn), lambda l, j: (l, 0, j)),
        compiler_params=_params("parallel", "parallel"),
        name="adaln",
    )(cv, ada_w, ada_b.reshape(depth, 1, n3))


def _rope(x, cos, sin_signed, first_half):
    rot = jnp.where(first_half, pltpu.roll(x, LANES - 16, 1), pltpu.roll(x, 16, 1))
    return x * cos + rot * sin_signed


def _attn_in_kernel(x_ref, g_ref, sc_ref, sh_ref, w_ref, wvt_ref, cos_ref, sin_ref,
                    qa_ref, ka_ref, qb_ref, kb_ref, vt_ref, gate_ref, *, rope):
    h = _norm_mod(x_ref[...], g_ref[...], sc_ref[...], sh_ref[...]).astype(jnp.bfloat16)
    p = jnp.dot(h, w_ref[...], preferred_element_type=jnp.float32)
    vt_ref[...] = lax.dot_general(wvt_ref[...], h, _NT,
                                  preferred_element_type=jnp.float32).astype(jnp.bfloat16)
    if rope:
        cos = cos_ref[...]
        sin = sin_ref[...]
        lane = lax.broadcasted_iota(jnp.int32, cos.shape, 1)
        first_half = (lane % 32) < 16

    def chunk(c):
        x = p[:, c * LANES:(c + 1) * LANES]
        return _rope(x, cos, sin, first_half) if rope else x

    o0 = A_WIDTH // LANES
    o1 = o0 + A_KV_WIDTH // LANES
    o2 = o1 + B_WIDTH // LANES
    o3 = o2 + B_WIDTH // LANES
    for c in range(o0):
        qa_ref[:, c * LANES:(c + 1) * LANES] = (chunk(c) * Q_SCALE_LOG2).astype(jnp.bfloat16)
    for c in range(o0, o1):
        ka_ref[:, (c - o0) * LANES:(c - o0 + 1) * LANES] = chunk(c).astype(jnp.bfloat16)
    for c in range(o1, o2):
        qb_ref[:, (c - o1) * LANES:(c - o1 + 1) * LANES] = (chunk(c) * Q_SCALE_LOG2).astype(jnp.bfloat16)
    for c in range(o2, o3):
        kb_ref[:, (c - o2) * LANES:(c - o2 + 1) * LANES] = chunk(c).astype(jnp.bfloat16)
    gate_ref[...] = p[:, o3 * LANES:].astype(jnp.bfloat16)


def _attn_in_proj(x, g, sc, sh, w, wvt, cos, sin, *, rope):
    n, d = x.shape
    tm = min(ROW_TILE, n)
    nw = w.shape[1]
    nv = wvt.shape[0]
    row = lambda i: (i, 0)
    fixed = lambda i: (0, 0)
    bf = jnp.bfloat16
    return pl.pallas_call(
        functools.partial(_attn_in_kernel, rope=rope),
        out_shape=[
            jax.ShapeDtypeStruct((n, A_WIDTH), bf),
            jax.ShapeDtypeStruct((n, A_KV_WIDTH), bf),
            jax.ShapeDtypeStruct((n, B_WIDTH), bf),
            jax.ShapeDtypeStruct((n, B_WIDTH), bf),
            jax.ShapeDtypeStruct((nv, n), bf),
            jax.ShapeDtypeStruct((n, ATTN_WIDTH), bf),
        ],
        grid=(n // tm,),
        in_specs=[
            pl.BlockSpec((tm, d), row),
            pl.BlockSpec((1, d), fixed),
            pl.BlockSpec((1, d), fixed),
            pl.BlockSpec((1, d), fixed),
            pl.BlockSpec((d, nw), fixed),
            pl.BlockSpec((nv, d), fixed),
            pl.BlockSpec((tm, LANES), row),
            pl.BlockSpec((tm, LANES), row),
        ],
        out_specs=[
            pl.BlockSpec((tm, A_WIDTH), row),
            pl.BlockSpec((tm, A_KV_WIDTH), row),
            pl.BlockSpec((tm, B_WIDTH), row),
            pl.BlockSpec((tm, B_WIDTH), row),
            pl.BlockSpec((nv, tm), lambda i: (0, i)),
            pl.BlockSpec((tm, ATTN_WIDTH), row),
        ],
        compiler_params=_params("parallel"),
        name="attn_in_proj",
    )(x, g, sc, sh, w, wvt, cos, sin)


def _gqa_scores(q, keys):
    tq = q.shape[0]
    lane = lax.broadcasted_iota(jnp.int32, (tq, LANES), 1)
    zero = jnp.zeros((tq, LANES), q.dtype)
    out = []
    for g in range(A_KV_HEADS):
        in_g = (lane >= g * HEAD_DIM) & (lane < (g + 1) * HEAD_DIM)
        qg = jnp.concatenate([jnp.where(in_g, q[:, j * LANES:(j + 1) * LANES], zero)
                              for j in range(A_GROUP)], axis=0)
        out.append(lax.dot_general(keys, qg, _NT, preferred_element_type=jnp.float32))
    return out


def _gqa_outputs(scores, vt, bias, sink_ref, tq):
    outs = [None] * A_Q_HEADS
    for g in range(A_KV_HEADS):
        s = scores[g]
        if bias is not None:
            s = jnp.concatenate([s[:3 * BLOCK] + bias, s[3 * BLOCK:]], axis=0)
        sk = sink_ref[g:g + 1, :]
        m = jnp.maximum(jnp.max(s, axis=0, keepdims=True), sk)
        acc = jnp.dot(vt, jnp.exp2((s - m).astype(jnp.bfloat16)),
                      preferred_element_type=jnp.float32)
        l = acc[A_KV_WIDTH:A_KV_WIDTH + 1] + jnp.exp2(sk - m)
        o = acc[g * HEAD_DIM:(g + 1) * HEAD_DIM] / l
        for j in range(A_GROUP):
            outs[2 * j + g] = o[:, j * tq:(j + 1) * tq]
    return jnp.concatenate(outs, axis=0).T


def _gqa_window_kernel(sink_ref, bias_ref, q_ref, *refs, nb):
    nk = GQA_BLOCKS + 2
    k_refs, vt_refs = refs[:nk], refs[nk:2 * nk]
    kx_ref, vxt_ref, o_ref = refs[2 * nk:]
    i = pl.program_id(0)
    ones = jnp.ones((SUM_ROWS, 3 * BLOCK + kx_ref.shape[0]), jnp.bfloat16)
    row = lax.broadcasted_iota(jnp.int32, (3 * BLOCK, 1), 0)
    all_scores, vts, biases = [], [], []
    for t in range(GQA_BLOCKS):
        keys = jnp.concatenate([r[...] for r in k_refs[t:t + 3]] + [kx_ref[...]], axis=0)
        vt = jnp.concatenate([r[...] for r in vt_refs[t:t + 3]] + [vxt_ref[...]], axis=1)
        vts.append(jnp.concatenate([vt, ones], axis=0))
        all_scores.append(_gqa_scores(q_ref[t * BLOCK:(t + 1) * BLOCK, :], keys))
        b = GQA_BLOCKS * i + t
        off_end = ((row < BLOCK) & (b == 0)) | ((row >= 2 * BLOCK) & (b == nb - 1))
        bias = bias_ref[...] + jnp.where(off_end, NEG_INF, 0.0)
        biases.append(jnp.concatenate([bias] * A_GROUP, axis=1))
    for t in range(GQA_BLOCKS):
        o = _gqa_outputs(all_scores[t], vts[t], biases[t], sink_ref, BLOCK)
        o_ref[t * BLOCK:(t + 1) * BLOCK, :] = o.astype(o_ref.dtype)


def _gqa_ctx_kernel(sink_ref, q_ref, kx_ref, vxt_ref, o_ref):
    vt = vxt_ref[...]
    vt = jnp.concatenate([vt, jnp.ones((SUM_ROWS, vt.shape[1]), vt.dtype)], axis=0)
    q = q_ref[...]
    o = _gqa_outputs(_gqa_scores(q, kx_ref[...]), vt, None, sink_ref, q.shape[0])
    o_ref[...] = o.astype(o_ref.dtype)


def _gqa_attention(sink_cols, bias, q, k, vt, kx, vxt, *, has_window):
    n = q.shape[0]
    nx = kx.shape[0]
    fixed = lambda i: (0, 0)
    if has_window:
        nb = n // BLOCK
        tq = GQA_BLOCKS * BLOCK
        steps = nb // GQA_BLOCKS
        assert steps * GQA_BLOCKS == nb
        offs = range(-1, GQA_BLOCKS + 1)
        blk = lambda i, o: jnp.clip(GQA_BLOCKS * i + o, 0, nb - 1)
        k_specs = [pl.BlockSpec((BLOCK, A_KV_WIDTH), lambda i, o=o: (blk(i, o), 0)) for o in offs]
        vt_specs = [pl.BlockSpec((A_KV_WIDTH, BLOCK), lambda i, o=o: (0, blk(i, o))) for o in offs]
        in_specs = ([pl.BlockSpec((A_KV_HEADS, A_GROUP * BLOCK), fixed),
                     pl.BlockSpec((3 * BLOCK, BLOCK), fixed),
                     pl.BlockSpec((tq, A_WIDTH), lambda i: (i, 0))] + k_specs + vt_specs +
                    [pl.BlockSpec((nx, A_KV_WIDTH), fixed), pl.BlockSpec((A_KV_WIDTH, nx), fixed)])
        args = (sink_cols, bias, q) + (k,) * len(offs) + (vt,) * len(offs) + (kx, vxt)
        body = functools.partial(_gqa_window_kernel, nb=nb)
    else:
        steps = 1
        tq = n
        in_specs = [pl.BlockSpec((A_KV_HEADS, A_GROUP * tq), fixed), pl.BlockSpec((tq, A_WIDTH), fixed),
                    pl.BlockSpec((nx, A_KV_WIDTH), fixed), pl.BlockSpec((A_KV_WIDTH, nx), fixed)]
        args = (sink_cols, q, kx, vxt)
        body = _gqa_ctx_kernel
    return pl.pallas_call(
        body,
        out_shape=jax.ShapeDtypeStruct((n, A_WIDTH), jnp.bfloat16),
        grid=(steps,),
        in_specs=in_specs,
        out_specs=pl.BlockSpec((tq, A_WIDTH), lambda i: (i, 0)),
        compiler_params=_params("parallel"),
        name="gqa_window" if has_window else "gqa_ctx",
    )(*args)


def _diff_masked_q(q_ref):
    q = q_ref[...]
    lane = lax.broadcasted_iota(jnp.int32, q.shape, 1)
    zero = jnp.zeros_like(q)
    return jnp.concatenate([jnp.where(lane < HEAD_DIM, q, zero),
                            jnp.where(lane >= HEAD_DIM, q, zero)], axis=0)


def _diff_probs(s_rel):
    return jnp.exp2(s_rel.astype(jnp.bfloat16))


def _diff_pv(vt, p):
    ones = jnp.ones((SUM_ROWS, vt.shape[1]), vt.dtype)
    return jnp.dot(jnp.concatenate([vt, ones], axis=0), p, preferred_element_type=jnp.float32)


def _diff_ctx_step(qz, kx, vxt):
    s = lax.dot_general(kx, qz, _NT, preferred_element_type=jnp.float32)
    m = jnp.max(s, axis=0, keepdims=True)
    return m, _diff_pv(vxt, _diff_probs(s - m))


def _diff_finalize(lam_ref, g_ref, acc, lam_init):
    hw = 2 * HEAD_DIM
    tq = acc.shape[1] // 2
    lv = lam_ref[...]
    lam = (jnp.exp(jnp.sum(lv[0:1] * lv[1:2], keepdims=True))
           - jnp.exp(jnp.sum(lv[2:3] * lv[3:4], keepdims=True)) + lam_init)
    on = acc[:hw] / acc[hw:hw + 1]
    o = on[:, :tq] - lam * on[:, tq:]
    ms = jnp.mean(o * o, axis=0, keepdims=True)
    o = (o * lax.rsqrt(ms + EPS)) * g_ref[...] * (1.0 - lam_init)
    return o.T.astype(jnp.bfloat16)


def _diff_ctx_kernel(lam_ref, g_ref, q_ref, kx_ref, vxt_ref, o_ref, *, lam_init):
    _, acc = _diff_ctx_step(_diff_masked_q(q_ref), kx_ref[...], vxt_ref[...])
    o_ref[...] = _diff_finalize(lam_ref, g_ref, acc, lam_init)


def _bf16_round(x):
    return x.astype(jnp.bfloat16).astype(jnp.float32)


def _diff_kernel(lam_ref, g_ref, q_ref, k_ref, vt_ref, kx_ref, vxt_ref, o_ref,
                 p0, p1, p2, p3, acc_ref, *, lam_init, n_chunks):
    tk = DIFF_TK
    qz = _diff_masked_q(q_ref)
    cols = qz.shape[0]
    qzt = qz.astype(jnp.float32).T.astype(jnp.bfloat16)
    ones_k = jnp.ones((tk, LANES), jnp.bfloat16)
    first_row = lax.broadcasted_iota(jnp.int32, (2 * SUBLANES, cols), 0) == 0
    zero_rows = jnp.zeros((LANES - 2 * SUBLANES, cols), jnp.bfloat16)
    p_buf = (p0, p1, p2, p3)

    def rhs_for(r):
        neg_r = jnp.where(first_row, -r, 0.0).astype(jnp.bfloat16)
        return jnp.concatenate([qzt, neg_r, zero_rows], axis=0)

    def rel_scores(c, rhs):
        k = k_ref[pl.ds(pl.multiple_of(c * tk, tk), tk), :]
        return jnp.dot(jnp.concatenate([k, ones_k], axis=1), rhs, preferred_element_type=jnp.float32)

    def pv(c, p_ref):
        return _diff_pv(vt_ref[:, pl.ds(pl.multiple_of(c * tk, tk), tk)], p_ref[...])

    def col_max(sa, sb):
        return jnp.maximum(jnp.max(sa, axis=0, keepdims=True), jnp.max(sb, axis=0, keepdims=True))

    s_ctx = lax.dot_general(kx_ref[...], qz, _NT, preferred_element_type=jnp.float32)
    m_ctx = jnp.max(s_ctx, axis=0, keepdims=True)
    r0 = _bf16_round(m_ctx)
    rhs0 = rhs_for(r0)
    sa, sb = rel_scores(0, rhs0), rel_scores(1, rhs0)
    acc_ref[...] = _diff_pv(vxt_ref[...], _diff_probs(s_ctx - r0))
    worst = col_max(sa, sb)
    p_buf[0][...] = _diff_probs(sa)
    p_buf[1][...] = _diff_probs(sb)

    def body(j, carry):
        run_max, pend, r_prev, r_cur, worst = carry
        for i in range(0, DIFF_STAGES, 2):
            c = DIFF_STAGES * j + i
            r_next = _bf16_round(run_max)
            rhs = rhs_for(r_next)
            pva = pv(c, p_buf[i % 4])
            sa = rel_scores(jnp.minimum(c + 2, n_chunks - 1), rhs)
            pvb = pv(c + 1, p_buf[(i + 1) % 4])
            sb = rel_scores(jnp.minimum(c + 3, n_chunks - 1), rhs)
            acc_ref[...] = jnp.exp2(r_prev - r_cur) * acc_ref[...] + pva + pvb
            top = col_max(sa, sb)
            p_buf[(i + 2) % 4][...] = _diff_probs(sa)
            p_buf[(i + 3) % 4][...] = _diff_probs(sb)
            worst = jnp.maximum(worst, top)
            run_max = jnp.maximum(run_max, pend)
            pend = top + r_next
            r_prev, r_cur = r_cur, r_next
        return run_max, pend, r_prev, r_cur, worst

    carry = lax.fori_loop(0, n_chunks // DIFF_STAGES, body, (m_ctx, worst + r0, r0, r0, worst))
    o_ref[...] = _diff_finalize(lam_ref, g_ref, acc_ref[...], lam_init)

    @pl.when(jnp.max(carry[4]) > DIFF_JUMP_LIMIT)
    def _():
        m0, acc0 = _diff_ctx_step(qz, kx_ref[...], vxt_ref[...])

        def exact(c, carry):
            m, acc = carry
            off = pl.multiple_of(c * tk, tk)
            s = lax.dot_general(k_ref[pl.ds(off, tk), :], qz, _NT, preferred_element_type=jnp.float32)
            m_new = jnp.maximum(m, jnp.max(s, axis=0, keepdims=True))
            acc = jnp.exp2(m - m_new) * acc + _diff_pv(vt_ref[:, pl.ds(off, tk)], _diff_probs(s - m_new))
            return m_new, acc

        _, acc = lax.fori_loop(0, n_chunks, exact, (m0, acc0))
        o_ref[...] = _diff_finalize(lam_ref, g_ref, acc, lam_init)


def _diff_attention(lamv, subln_col, q, k, vt, kx, vxt, *, has_latent, lam_init):
    n = q.shape[0]
    nx = kx.shape[0]
    tq = min(DIFF_TQ, n)
    hw = 2 * HEAD_DIM
    fixed = lambda h, i: (0, 0)
    in_specs = [pl.BlockSpec((4, HEAD_DIM), fixed), pl.BlockSpec((hw, 1), fixed),
                pl.BlockSpec((tq, hw), lambda h, i: (i, h))]
    args = [lamv, subln_col, q]
    scratch = []
    if has_latent:
        nk = k.shape[0]
        n_chunks = nk // DIFF_TK
        assert DIFF_STAGES % 4 == 0 and n_chunks % DIFF_STAGES == 0 and n_chunks * DIFF_TK == nk
        in_specs += [pl.BlockSpec((nk, hw), lambda h, i: (0, h)),
                     pl.BlockSpec((hw, nk), lambda h, i: (h + 1, 0))]
        args += [k, vt]
        body = functools.partial(_diff_kernel, lam_init=lam_init, n_chunks=n_chunks)
        scratch = ([pltpu.VMEM((DIFF_TK, 2 * tq), jnp.bfloat16)] * 4 +
                   [pltpu.VMEM((hw + SUM_ROWS, 2 * tq), jnp.float32)])
    else:
        body = functools.partial(_diff_ctx_kernel, lam_init=lam_init)
    in_specs += [pl.BlockSpec((nx, hw), lambda h, i: (0, h)),
                 pl.BlockSpec((hw, nx), lambda h, i: (h + 1, 0))]
    args += [kx, vxt]
    return pl.pallas_call(
        body,
        out_shape=jax.ShapeDtypeStruct((n, B_WIDTH), jnp.bfloat16),
        grid=(B_HEADS, n // tq),
        in_specs=in_specs,
        out_specs=pl.BlockSpec((tq, hw), lambda h, i: (i, h)),
        scratch_shapes=scratch,
        compiler_params=_params("parallel", "parallel"),
        name="diff_attn" if has_latent else "diff_attn_ctx",
    )(*args)


def _attn_out_rec_in_kernel(oa_ref, ob_ref, gate_ref, x_ref, gt_ref, w_ref,
                            g_ref, sc_ref, sh_ref, wr_ref, xl_ref, xr_ref, gr_ref):
    o = jnp.concatenate([oa_ref[...], ob_ref[...]], axis=-1).astype(jnp.float32)
    a = o * _silu(gate_ref[...].astype(jnp.float32))
    y = jnp.dot(a.astype(jnp.bfloat16), w_ref[...], preferred_element_type=jnp.float32)
    xl = x_ref[...] + gt_ref[...] * y
    xl_ref[...] = xl
    h = _norm_mod(xl, g_ref[...], sc_ref[...], sh_ref[...]).astype(jnp.bfloat16)
    p = jnp.dot(h, wr_ref[...], preferred_element_type=jnp.float32)
    xr_ref[...] = p[:, :RNN_WIDTH]
    gr_ref[...] = p[:, RNN_WIDTH:]


def _attn_out_rec_in(oa, ob, gate, x, gt, w, g, sc, sh, wr):
    n, d = x.shape
    tm = min(ROW_TILE, n)
    row = lambda i: (i, 0)
    fixed = lambda i: (0, 0)
    vec = pl.BlockSpec((1, d), fixed)
    return pl.pallas_call(
        _attn_out_rec_in_kernel,
        out_shape=[jax.ShapeDtypeStruct((n, d), jnp.float32),
                   jax.ShapeDtypeStruct((n, RNN_WIDTH), jnp.float32),
                   jax.ShapeDtypeStruct((n, RNN_WIDTH), jnp.float32)],
        grid=(n // tm,),
        in_specs=[
            pl.BlockSpec((tm, A_WIDTH), row),
            pl.BlockSpec((tm, B_WIDTH), row),
            pl.BlockSpec((tm, ATTN_WIDTH), row),
            pl.BlockSpec((tm, d), row),
            vec,
            pl.BlockSpec((ATTN_WIDTH, d), fixed),
            vec, vec, vec,
            pl.BlockSpec((d, 2 * RNN_WIDTH), fixed),
        ],
        out_specs=[pl.BlockSpec((tm, d), row), pl.BlockSpec((tm, RNN_WIDTH), row),
                   pl.BlockSpec((tm, RNN_WIDTH), row)],
        compiler_params=_params("parallel"),
        name="attn_out_rec_in",
    )(oa, ob, gate, x, gt, w, g, sc, sh, wr)


def _rec_scan_kernel(*refs, reverse, fuse_out, nt):
    if fuse_out:
        (x_ref, xp_ref, xn_ref, cw_ref, cb_ref, w_ref, bias_ref, lam_ref, h0_ref,
         yr_ref, gr_ref, xl_ref, gt_ref, wo_ref, fg_ref,
         o_ref, xe_ref, a_ref, b_ref, y_ref, h_ref) = refs
    else:
        (x_ref, xp_ref, xn_ref, cw_ref, cb_ref, w_ref, bias_ref, lam_ref, h0_ref,
         y_ref, hf_ref, xe_ref, a_ref, b_ref, h_ref) = refs
    i = pl.program_id(0)
    pos = i if not reverse else nt - 1 - i
    tm = x_ref.shape[0]

    @pl.when(i == 0)
    def _():
        h_ref[...] = h0_ref[...]

    xe_ref[0:SUBLANES, :] = jnp.where(pos > 0, xp_ref[...], 0.0)
    xe_ref[SUBLANES:SUBLANES + tm, :] = x_ref[...]
    xe_ref[SUBLANES + tm:, :] = jnp.where(pos < nt - 1, xn_ref[...], 0.0)
    cw = cw_ref[...]
    xe = xe_ref[...]
    taps = pltpu.roll(xe * cw[0:1], 1, 0)
    taps = pltpu.roll(taps + xe * cw[1:2], 1, 0)
    taps = taps + xe * cw[2:3] + pltpu.roll(xe * cw[3:4], tm + 2 * SUBLANES - 1, 0)
    uh = taps[SUBLANES:SUBLANES + tm] + cb_ref[...]
    ub = uh.astype(jnp.bfloat16)
    half = RNN_WIDTH // 2
    z0 = jnp.dot(ub[:, :half], w_ref[0], preferred_element_type=jnp.float32)
    z1 = jnp.dot(ub[:, half:], w_ref[1], preferred_element_type=jnp.float32)
    bias = bias_ref[...]
    ta = jnp.tanh(jnp.concatenate([z0[:, :half], z1[:, :half]], axis=1) + bias[:, :RNN_WIDTH])
    tx = jnp.tanh(jnp.concatenate([z0[:, half:], z1[:, half:]], axis=1) + bias[:, RNN_WIDTH:])
    nl = -lam_ref[...]
    softplus = jnp.maximum(nl, 0.0) + jnp.log1p(jnp.exp(-jnp.abs(nl)))
    coef = (-0.5 * RG_C * LOG2E) * softplus
    a = jnp.exp2(coef + coef * ta)
    a_ref[...] = a
    om = 1.0 - a * a
    b_ref[...] = (om * lax.rsqrt(jnp.maximum(om, 1e-30))) * ((1.0 + tx) * uh)

    def row(s, h):
        rr = (tm - 1 - s) if reverse else s
        h = a_ref[pl.ds(rr, 1), :] * h + b_ref[pl.ds(rr, 1), :]
        y_ref[pl.ds(rr, 1), :] = h
        return h

    h = lax.fori_loop(0, tm, row, h_ref[...], unroll=32)
    h_ref[...] = h

    if fuse_out:
        y = (y_ref[...] + yr_ref[...]) * _silu(gr_ref[...])
        out = jnp.dot(y.astype(jnp.bfloat16), wo_ref[...], preferred_element_type=jnp.float32)
        xl = xl_ref[...] + gt_ref[...] * out
        ms = jnp.mean(xl * xl, axis=-1, keepdims=True)
        o_ref[...] = xl * lax.rsqrt(ms + EPS) * fg_ref[...]
    else:
        hf_ref[...] = h


def _rec_scan(xr, cw, cb, w, bias, lam, h0, *, reverse, fused=None):
    assert CONV_WIDTH == 4
    n = xr.shape[0]
    tm = min(SCAN_TILE, n)
    nt = n // tm
    per = tm // SUBLANES
    nblk8 = n // SUBLANES
    pos = (lambda i: nt - 1 - i) if reverse else (lambda i: i)
    row = lambda i: (pos(i), 0)
    prev = lambda i: (jnp.maximum(pos(i) * per - 1, 0), 0)
    nxt = lambda i: (jnp.minimum((pos(i) + 1) * per, nblk8 - 1), 0)
    fixed = lambda i: (0, 0)
    vec = pl.BlockSpec((1, RNN_WIDTH), fixed)
    in_specs = [
        pl.BlockSpec((tm, RNN_WIDTH), row),
        pl.BlockSpec((SUBLANES, RNN_WIDTH), prev),
        pl.BlockSpec((SUBLANES, RNN_WIDTH), nxt),
        pl.BlockSpec((CONV_WIDTH, RNN_WIDTH), fixed),
        vec,
        pl.BlockSpec((2, RNN_WIDTH // 2, RNN_WIDTH), lambda i: (0, 0, 0)),
        pl.BlockSpec((1, 2 * RNN_WIDTH), fixed),
        vec,
        vec,
    ]
    args = [xr, xr, xr, cw, cb, w, bias, lam, h0]
    scratch = [pltpu.VMEM((tm + 2 * SUBLANES, RNN_WIDTH), jnp.float32),
               pltpu.VMEM((tm, RNN_WIDTH), jnp.float32),
               pltpu.VMEM((tm, RNN_WIDTH), jnp.float32)]
    if fused is not None:
        yr, gr, xl, gt, wo, fg = fused
        d = xl.shape[1]
        in_specs += [
            pl.BlockSpec((tm, RNN_WIDTH), row),
            pl.BlockSpec((tm, RNN_WIDTH), row),
            pl.BlockSpec((tm, d), row),
            pl.BlockSpec((1, d), fixed),
            pl.BlockSpec((RNN_WIDTH, d), fixed),
            pl.BlockSpec((1, d), fixed),
        ]
        args += [yr, gr, xl, gt, wo, fg]
        out_shape = jax.ShapeDtypeStruct((n, d), jnp.float32)
        out_specs = pl.BlockSpec((tm, d), row)
        scratch = scratch + [pltpu.VMEM((tm, RNN_WIDTH), jnp.float32)]
    else:
        out_shape = [jax.ShapeDtypeStruct((n, RNN_WIDTH), jnp.float32),
                     jax.ShapeDtypeStruct((1, RNN_WIDTH), jnp.float32)]
        out_specs = [pl.BlockSpec((tm, RNN_WIDTH), row), vec]
    scratch = scratch + [pltpu.VMEM((1, RNN_WIDTH), jnp.float32)]
    return pl.pallas_call(
        functools.partial(_rec_scan_kernel, reverse=reverse, fuse_out=fused is not None, nt=nt),
        out_shape=out_shape,
        grid=(nt,),
        in_specs=in_specs,
        out_specs=out_specs,
        scratch_shapes=scratch,
        compiler_params=_params("arbitrary"),
        name="rec_scan_out" if fused is not None else ("rec_scan_rev" if reverse else "rec_scan_fwd"),
    )(*args)


def _rope_tables(n):
    rows = n // GRID_W
    n_freq = HEAD_DIM // 4
    inv_freq = ROPE_BASE ** (-jnp.arange(n_freq, dtype=jnp.float32) / n_freq)
    ar = jnp.arange(rows, dtype=jnp.float32)[:, None] * inv_freq
    ac = jnp.arange(GRID_W, dtype=jnp.float32)[:, None] * inv_freq

    def table(fn, sign):
        r = jnp.broadcast_to(fn(ar)[:, None, :], (rows, GRID_W, n_freq)).reshape(n, n_freq)
        c = jnp.broadcast_to(fn(ac)[None, :, :], (rows, GRID_W, n_freq)).reshape(n, n_freq)
        return jnp.concatenate([sign * r, r, sign * c, c] * 2, axis=-1)

    return table(jnp.cos, 1.0), table(jnp.sin, -1.0)


def _block_diag(w):
    nb, bi, bj = w.shape
    on_diag = jnp.eye(nb, dtype=bool)[:, None, :, None]
    return jnp.where(on_diag, w[:, :, None, :], 0).reshape(nb * bi, nb * bj)


def kernel(x, c, ctx, c_ctx, norm_g, ada_w, ada_b, attn_w_in, attn_w_out, attn_sink, lam_q1, lam_k1,
           lam_q2, lam_k2, subln_g, rec_w_in, rec_conv_w, rec_conv_b, rec_wa, rec_ba, rec_wx, rec_bx,
           rec_lam, rec_w_out, final_g):
    assert x.shape[0] == 1 and c.shape[0] == 1 and ctx.shape[0] == 1
    bf = jnp.bfloat16
    d = D_MODEL
    xl = x[0]
    xc = ctx[0]
    n = xl.shape[0]

    cv = jnp.zeros((SUBLANES, d), jnp.float32).at[0].set(c[0]).at[1].set(c_ctx)
    mods = _adaln(cv, ada_w, ada_b)

    def mod(l, r):
        m = mods[l, r]
        return m[None, :d], m[None, d:2 * d], m[None, 2 * d:]

    lam_init = 0.8 - 0.6 * math.exp(-0.3 * 0)
    s1 = A_WIDTH
    s2 = s1 + A_KV_WIDTH
    s3 = s2 + A_KV_WIDTH
    s5 = s3 + 2 * B_WIDTH
    s6 = s5 + B_WIDTH
    head_perm = jnp.array([g * A_GROUP + j for j in range(A_GROUP) for g in range(A_KV_HEADS)])
    col_perm = (head_perm[:, None] * HEAD_DIM + jnp.arange(HEAD_DIM)).reshape(-1)
    w_in = attn_w_in[0]
    w_main = jnp.concatenate([w_in[:, :s1][:, col_perm], w_in[:, s1:s2], w_in[:, s3:s5],
                              w_in[:, s6:s6 + A_WIDTH][:, col_perm], w_in[:, s6 + A_WIDTH:]],
                             axis=1).astype(bf)
    w_v = lax.optimization_barrier(jnp.concatenate([w_in[:, s2:s3], w_in[:, s5:s6]], axis=1).astype(bf))
    w_vt = w_v.T
    w_out = attn_w_out[0]
    w_out = jnp.concatenate([w_out[:A_WIDTH][col_perm], w_out[A_WIDTH:]], axis=0).astype(bf)
    g0 = norm_g[0][None]
    cos, sin = _rope_tables(n)
    sh, sc, gt = mod(0, 0)
    shc, scc, gtc = mod(0, 1)
    qa, ka, qb, kb, vt, gate = _attn_in_proj(xl, g0, sc, sh, w_main, w_vt, cos, sin, rope=True)
    nc = xc.shape[0]
    qac, kac, qbc, kbc, vtc, gatec = _attn_in_proj(
        xc, g0, scc, shc, w_main, w_vt, cos[:nc], sin[:nc], rope=False)

    sink_gj = (attn_sink[0] * LOG2E).reshape(A_KV_HEADS, A_GROUP)
    kj = jnp.arange(3 * BLOCK)[:, None]
    qi = jnp.arange(BLOCK)[None, :]
    band_bias = jnp.where(jnp.abs(kj - BLOCK - qi) <= WINDOW, 0.0, NEG_INF).astype(jnp.float32)
    lamv = jnp.stack([lam_q1[0], lam_k1[0], lam_q2[0], lam_k2[0]])
    subln_col = subln_g[0][:, None]
    oa = _gqa_attention(jnp.repeat(sink_gj, BLOCK, axis=1), band_bias, qa, ka, vt, kac, vtc,
                        has_window=True)
    ob = _diff_attention(lamv, subln_col, qb, kb, vt, kbc, vtc, has_latent=True, lam_init=lam_init)
    oac = _gqa_attention(jnp.repeat(sink_gj, nc, axis=1), None, qac, None, None, kac, vtc,
                         has_window=False)
    obc = _diff_attention(lamv, subln_col, qbc, None, None, kbc, vtc, has_latent=False,
                          lam_init=lam_init)

    g1 = norm_g[1][None]
    sh, sc, gt1 = mod(1, 0)
    shc, scc, _ = mod(1, 1)
    w_rin = rec_w_in[0].astype(bf)
    xl, xr, gr = _attn_out_rec_in(oa, ob, gate, xl, gt, w_out, g1, sc, sh, w_rin)
    _, xrc, _ = _attn_out_rec_in(oac, obc, gatec, xc, gtc, w_out, g1, scc, shc, w_rin)
    gt = gt1
    cw = 0.5 * rec_conv_w[0]
    cb = 0.5 * rec_conv_b[0][None]
    zero_h = jnp.zeros((1, RNN_WIDTH), jnp.float32)

    def coeff(dr):
        hb = RNN_BLOCKS // 2
        wa, wx = rec_wa[0, dr].astype(bf), rec_wx[0, dr].astype(bf)
        w = jnp.stack([jnp.concatenate([_block_diag(wa[i * hb:(i + 1) * hb]),
                                        _block_diag(wx[i * hb:(i + 1) * hb])], axis=1)
                       for i in range(2)])
        bias = 0.5 * jnp.concatenate([rec_ba[0, dr], rec_bx[0, dr]])[None]
        return w, bias, rec_lam[0, dr][None]

    wf, bias_f, lam_f = coeff(0)
    wr, bias_r, lam_r = coeff(1)
    _, h0f = _rec_scan(xrc, cw, cb, wf, bias_f, lam_f, zero_h, reverse=False)
    _, h0r = _rec_scan(xrc, cw, cb, wr, bias_r, lam_r, zero_h, reverse=True)
    yr, _ = _rec_scan(xr, cw, cb, wr, bias_r, lam_r, h0r, reverse=True)
    out = _rec_scan(xr, cw, cb, wf, bias_f, lam_f, h0f, reverse=False,
                    fused=(yr, gr, xl, gt, rec_w_out[0].astype(bf), final_g[None]))
    return out[None]
```

```python
import functools
import math

import jax
import jax.numpy as jnp
from jax import lax
from jax.experimental import pallas as pl
from jax.experimental.pallas import tpu as pltpu

D_MODEL = 1024
GRID_W = 64
HEAD_DIM = 64
ROPE_BASE = 10000.0
EPS = 1e-6
NEG_INF = -1e30
A_Q_HEADS = 8
A_KV_HEADS = 2
A_GROUP = A_Q_HEADS // A_KV_HEADS
A_WIDTH = A_Q_HEADS * HEAD_DIM
A_KV_WIDTH = A_KV_HEADS * HEAD_DIM
WINDOW = 128
BLOCK = 128
B_HEADS = 4
B_WIDTH = B_HEADS * 2 * HEAD_DIM
ATTN_WIDTH = A_WIDTH + B_WIDTH
RNN_WIDTH = 1280
RNN_BLOCKS = 16
RNN_BLOCK_DIM = RNN_WIDTH // RNN_BLOCKS
CONV_WIDTH = 4
RG_C = 8.0
LOG2E = math.log2(math.e)
Q_SCALE_LOG2 = HEAD_DIM ** -0.5 * LOG2E

LANES = 128
SUBLANES = 8
VMEM_LIMIT = 48 * 1024 * 1024

ROW_TILE = 512
SCAN_TILE = 256
GQA_BLOCKS = 4
DIFF_TQ = 256
DIFF_SUB = 4
DIFF_TK = 512
DIFF_STAGES = 16
DIFF_JUMP_LIMIT = 64.0
SUM_ROWS = 16

_NT = (((1,), (1,)), ((), ()))


def _silu(x):
    return x * jax.nn.sigmoid(x)


def _norm_mod(x, g, sc, sh):
    ms = jnp.mean(x * x, axis=-1, keepdims=True)
    return (x * lax.rsqrt(ms + EPS) * g) * (1.0 + sc) + sh


def _params(*sem, flags=None):
    return pltpu.CompilerParams(dimension_semantics=sem, vmem_limit_bytes=VMEM_LIMIT, flags=flags)


def _adaln_kernel(cv_ref, w_ref, b_ref, o_ref):
    s = _silu(cv_ref[...])
    o_ref[0] = jnp.dot(s, w_ref[0], preferred_element_type=jnp.float32) + b_ref[0]


def _adaln(cv, ada_w, ada_b):
    depth, d, n3 = ada_w.shape
    tn = 1024
    return pl.pallas_call(
        _adaln_kernel,
        out_shape=jax.ShapeDtypeStruct((depth, SUBLANES, n3), jnp.float32),
        grid=(depth, n3 // tn),
        in_specs=[
            pl.BlockSpec((SUBLANES, d), lambda l, j: (0, 0)),
            pl.BlockSpec((1, d, tn), lambda l, j: (l, 0, j)),
            pl.BlockSpec((1, 1, tn), lambda l, j: (l, 0, j)),
        ],
        out_specs=pl.BlockSpec((1, SUBLANES, tn), lambda l, j: (l, 0, j)),
        compiler_params=_params("parallel", "parallel"),
        name="adaln",
    )(cv, ada_w, ada_b.reshape(depth, 1, n3))


def _rope(x, cos, sin_signed, first_half):
    rot = jnp.where(first_half, pltpu.roll(x, LANES - 16, 1), pltpu.roll(x, 16, 1))
    return x * cos + rot * sin_signed


def _attn_in_kernel(x_ref, g_ref, sc_ref, sh_ref, w_ref, wvt_ref, cos_ref, sin_ref,
                    qa_ref, ka_ref, qb_ref, kb_ref, vt_ref, gate_ref, *, rope):
    h = _norm_mod(x_ref[...], g_ref[...], sc_ref[...], sh_ref[...]).astype(jnp.bfloat16)
    p = jnp.dot(h, w_ref[...], preferred_element_type=jnp.float32)
    vt_ref[...] = lax.dot_general(wvt_ref[...], h, _NT,
                                  preferred_element_type=jnp.float32).astype(jnp.bfloat16)
    if rope:
        cos = jnp.concatenate([cos_ref[...]] * 2, axis=1)
        sin = jnp.concatenate([sin_ref[...]] * 2, axis=1)
        lane = lax.broadcasted_iota(jnp.int32, cos.shape, 1)
        first_half = (lane % 32) < 16

    def chunk(c):
        x = p[:, c * LANES:(c + 1) * LANES]
        return _rope(x, cos, sin, first_half) if rope else x

    o0 = A_WIDTH // LANES
    o1 = o0 + A_KV_WIDTH // LANES
    o2 = o1 + B_WIDTH // LANES
    o3 = o2 + B_WIDTH // LANES
    for c in range(o0):
        qa_ref[:, c * LANES:(c + 1) * LANES] = (chunk(c) * Q_SCALE_LOG2).astype(jnp.bfloat16)
    for c in range(o0, o1):
        ka_ref[:, (c - o0) * LANES:(c - o0 + 1) * LANES] = chunk(c).astype(jnp.bfloat16)
    for c in range(o1, o2):
        qb_ref[:, (c - o1) * LANES:(c - o1 + 1) * LANES] = (chunk(c) * Q_SCALE_LOG2).astype(jnp.bfloat16)
    for c in range(o2, o3):
        kb_ref[:, (c - o2) * LANES:(c - o2 + 1) * LANES] = chunk(c).astype(jnp.bfloat16)
    gate_ref[...] = p[:, o3 * LANES:].astype(jnp.bfloat16)


def _attn_in_proj(x, g, sc, sh, w, wvt, cos, sin, *, rope):
    n, d = x.shape
    tm = min(ROW_TILE, n)
    nw = w.shape[1]
    nv = wvt.shape[0]
    row = lambda i: (i, 0)
    fixed = lambda i: (0, 0)
    bf = jnp.bfloat16
    return pl.pallas_call(
        functools.partial(_attn_in_kernel, rope=rope),
        out_shape=[
            jax.ShapeDtypeStruct((n, A_WIDTH), bf),
            jax.ShapeDtypeStruct((n, A_KV_WIDTH), bf),
            jax.ShapeDtypeStruct((n, B_WIDTH), bf),
            jax.ShapeDtypeStruct((n, B_WIDTH), bf),
            jax.ShapeDtypeStruct((nv, n), bf),
            jax.ShapeDtypeStruct((n, ATTN_WIDTH), bf),
        ],
        grid=(n // tm,),
        in_specs=[
            pl.BlockSpec((tm, d), row),
            pl.BlockSpec((1, d), fixed),
            pl.BlockSpec((1, d), fixed),
            pl.BlockSpec((1, d), fixed),
            pl.BlockSpec((d, nw), fixed),
            pl.BlockSpec((nv, d), fixed),
            pl.BlockSpec((tm, HEAD_DIM), row),
            pl.BlockSpec((tm, HEAD_DIM), row),
        ],
        out_specs=[
            pl.BlockSpec((tm, A_WIDTH), row),
            pl.BlockSpec((tm, A_KV_WIDTH), row),
            pl.BlockSpec((tm, B_WIDTH), row),
            pl.BlockSpec((tm, B_WIDTH), row),
            pl.BlockSpec((nv, tm), lambda i: (0, i)),
            pl.BlockSpec((tm, ATTN_WIDTH), row),
        ],
        compiler_params=_params("parallel"),
        name="attn_in_proj",
    )(x, g, sc, sh, w, wvt, cos, sin)


def _gqa_scores(q, keys):
    tq = q.shape[0]
    lane = lax.broadcasted_iota(jnp.int32, (tq, LANES), 1)
    zero = jnp.zeros((tq, LANES), q.dtype)
    out = []
    for g in range(A_KV_HEADS):
        in_g = (lane >= g * HEAD_DIM) & (lane < (g + 1) * HEAD_DIM)
        qg = jnp.concatenate([jnp.where(in_g, q[:, j * LANES:(j + 1) * LANES], zero)
                              for j in range(A_GROUP)], axis=0)
        out.append(lax.dot_general(keys, qg, _NT, preferred_element_type=jnp.float32))
    return out


def _gqa_outputs(scores, vt, bias, sink_ref, tq):
    outs = [None] * A_Q_HEADS
    for g in range(A_KV_HEADS):
        s = scores[g]
        if bias is not None:
            s = jnp.concatenate([s[:3 * BLOCK] + bias, s[3 * BLOCK:]], axis=0)
        sk = sink_ref[g:g + 1, :]
        m = jnp.maximum(jnp.max(s, axis=0, keepdims=True), sk)
        acc = jnp.dot(vt, jnp.exp2((s - m).astype(jnp.bfloat16)),
                      preferred_element_type=jnp.float32)
        l = acc[A_KV_WIDTH:A_KV_WIDTH + 1] + jnp.exp2(sk - m)
        o = acc[g * HEAD_DIM:(g + 1) * HEAD_DIM] / l
        for j in range(A_GROUP):
            outs[2 * j + g] = o[:, j * tq:(j + 1) * tq]
    return jnp.concatenate(outs, axis=0).T


def _gqa_window_kernel(sink_ref, bias_ref, q_ref, *refs, nb):
    nk = GQA_BLOCKS + 2
    k_refs, vt_refs = refs[:nk], refs[nk:2 * nk]
    kx_ref, vxt_ref, o_ref = refs[2 * nk:]
    i = pl.program_id(0)
    ones = jnp.ones((SUM_ROWS, 3 * BLOCK + kx_ref.shape[0]), jnp.bfloat16)
    row = lax.broadcasted_iota(jnp.int32, (3 * BLOCK, 1), 0)
    all_scores, vts, biases = [], [], []
    for t in range(GQA_BLOCKS):
        keys = jnp.concatenate([r[...] for r in k_refs[t:t + 3]] + [kx_ref[...]], axis=0)
        vt = jnp.concatenate([r[...] for r in vt_refs[t:t + 3]] + [vxt_ref[...]], axis=1)
        vts.append(jnp.concatenate([vt, ones], axis=0))
        all_scores.append(_gqa_scores(q_ref[t * BLOCK:(t + 1) * BLOCK, :], keys))
        b = GQA_BLOCKS * i + t
        off_end = ((row < BLOCK) & (b == 0)) | ((row >= 2 * BLOCK) & (b == nb - 1))
        bias = bias_ref[...] + jnp.where(off_end, NEG_INF, 0.0)
        biases.append(jnp.concatenate([bias] * A_GROUP, axis=1))
    for t in range(GQA_BLOCKS):
        o = _gqa_outputs(all_scores[t], vts[t], biases[t], sink_ref, BLOCK)
        o_ref[t * BLOCK:(t + 1) * BLOCK, :] = o.astype(o_ref.dtype)


def _gqa_ctx_kernel(sink_ref, q_ref, kx_ref, vxt_ref, o_ref):
    vt = vxt_ref[...]
    vt = jnp.concatenate([vt, jnp.ones((SUM_ROWS, vt.shape[1]), vt.dtype)], axis=0)
    q = q_ref[...]
    o = _gqa_outputs(_gqa_scores(q, kx_ref[...]), vt, None, sink_ref, q.shape[0])
    o_ref[...] = o.astype(o_ref.dtype)


def _gqa_attention(sink_cols, bias, q, k, vt, kx, vxt, *, has_window):
    n = q.shape[0]
    nx = kx.shape[0]
    fixed = lambda i: (0, 0)
    if has_window:
        nb = n // BLOCK
        tq = GQA_BLOCKS * BLOCK
        steps = nb // GQA_BLOCKS
        assert steps * GQA_BLOCKS == nb
        offs = range(-1, GQA_BLOCKS + 1)
        blk = lambda i, o: jnp.clip(GQA_BLOCKS * i + o, 0, nb - 1)
        k_specs = [pl.BlockSpec((BLOCK, A_KV_WIDTH), lambda i, o=o: (blk(i, o), 0)) for o in offs]
        vt_specs = [pl.BlockSpec((A_KV_WIDTH, BLOCK), lambda i, o=o: (0, blk(i, o))) for o in offs]
        in_specs = ([pl.BlockSpec((A_KV_HEADS, A_GROUP * BLOCK), fixed),
                     pl.BlockSpec((3 * BLOCK, BLOCK), fixed),
                     pl.BlockSpec((tq, A_WIDTH), lambda i: (i, 0))] + k_specs + vt_specs +
                    [pl.BlockSpec((nx, A_KV_WIDTH), fixed), pl.BlockSpec((A_KV_WIDTH, nx), fixed)])
        args = (sink_cols, bias, q) + (k,) * len(offs) + (vt,) * len(offs) + (kx, vxt)
        body = functools.partial(_gqa_window_kernel, nb=nb)
    else:
        steps = 1
        tq = n
        in_specs = [pl.BlockSpec((A_KV_HEADS, A_GROUP * tq), fixed), pl.BlockSpec((tq, A_WIDTH), fixed),
                    pl.BlockSpec((nx, A_KV_WIDTH), fixed), pl.BlockSpec((A_KV_WIDTH, nx), fixed)]
        args = (sink_cols, q, kx, vxt)
        body = _gqa_ctx_kernel
    return pl.pallas_call(
        body,
        out_shape=jax.ShapeDtypeStruct((n, A_WIDTH), jnp.bfloat16),
        grid=(steps,),
        in_specs=in_specs,
        out_specs=pl.BlockSpec((tq, A_WIDTH), lambda i: (i, 0)),
        compiler_params=_params("parallel"),
        name="gqa_window" if has_window else "gqa_ctx",
    )(*args)


def _diff_masked_q(q):
    lane = lax.broadcasted_iota(jnp.int32, q.shape, 1)
    zero = jnp.zeros_like(q)
    return jnp.concatenate([jnp.where(lane < HEAD_DIM, q, zero),
                            jnp.where(lane >= HEAD_DIM, q, zero)], axis=0)


def _diff_probs(s_rel):
    return jnp.exp2(s_rel.astype(jnp.bfloat16))


def _diff_pv(vt, p):
    ones = jnp.ones((SUM_ROWS, vt.shape[1]), vt.dtype)
    return jnp.dot(jnp.concatenate([vt, ones], axis=0), p, preferred_element_type=jnp.float32)


def _diff_ctx_step(qz, kx, vxt):
    s = lax.dot_general(kx, qz, _NT, preferred_element_type=jnp.float32)
    m = jnp.max(s, axis=0, keepdims=True)
    return m, _diff_pv(vxt, _diff_probs(s - m))


def _diff_finalize(lam_ref, g_ref, acc, lam_init):
    hw = 2 * HEAD_DIM
    tq = acc.shape[1] // 2
    lv = lam_ref[...]
    lam = (jnp.exp(jnp.sum(lv[0:1] * lv[1:2], keepdims=True))
           - jnp.exp(jnp.sum(lv[2:3] * lv[3:4], keepdims=True)) + lam_init)
    on = acc[:hw] / acc[hw:hw + 1]
    o = on[:, :tq] - lam * on[:, tq:]
    ms = jnp.mean(o * o, axis=0, keepdims=True)
    o = (o * lax.rsqrt(ms + EPS)) * g_ref[...] * (1.0 - lam_init)
    return o.T.astype(jnp.bfloat16)


def _diff_ctx_kernel(lam_ref, g_ref, q_ref, kx_ref, vxt_ref, o_ref, *, lam_init):
    _, acc = _diff_ctx_step(_diff_masked_q(q_ref[...]), kx_ref[...], vxt_ref[...])
    o_ref[...] = _diff_finalize(lam_ref, g_ref, acc, lam_init)


def _bf16_round(x):
    return x.astype(jnp.bfloat16).astype(jnp.float32)


def _diff_kernel(lam_ref, g_ref, q_ref, k_ref, vt_ref, kx_ref, vxt_ref, o_ref, *scratch,
                 lam_init, n_chunks):
    def sub_tile(t, carry):
        rows = pl.ds(pl.multiple_of(t * DIFF_TQ, DIFF_TQ), DIFF_TQ)
        _diff_tile(lam_ref, g_ref, q_ref[rows, :], k_ref, vt_ref, kx_ref, vxt_ref, o_ref.at[rows, :],
                   *scratch, lam_init=lam_init, n_chunks=n_chunks)
        return carry

    lax.fori_loop(0, q_ref.shape[0] // DIFF_TQ, sub_tile, 0)


def _diff_tile(lam_ref, g_ref, q, k_ref, vt_ref, kx_ref, vxt_ref, o_ref,
               p0, p1, p2, p3, acc_ref, *, lam_init, n_chunks):
    tk = DIFF_TK
    qz = _diff_masked_q(q)
    cols = qz.shape[0]
    qzt = qz.astype(jnp.float32).T.astype(jnp.bfloat16)
    ones_k = jnp.ones((tk, LANES), jnp.bfloat16)
    first_row = lax.broadcasted_iota(jnp.int32, (2 * SUBLANES, cols), 0) == 0
    zero_rows = jnp.zeros((LANES - 2 * SUBLANES, cols), jnp.bfloat16)
    p_buf = (p0, p1, p2, p3)

    def rhs_for(r):
        neg_r = jnp.where(first_row, -r, 0.0).astype(jnp.bfloat16)
        return jnp.concatenate([qzt, neg_r, zero_rows], axis=0)

    def rel_scores(c, rhs):
        k = k_ref[pl.ds(pl.multiple_of(c * tk, tk), tk), :]
        return jnp.dot(jnp.concatenate([k, ones_k], axis=1), rhs, preferred_element_type=jnp.float32)

    def pv(c, p_ref):
        return _diff_pv(vt_ref[:, pl.ds(pl.multiple_of(c * tk, tk), tk)], p_ref[...])

    def col_max(sa, sb):
        return jnp.maximum(jnp.max(sa, axis=0, keepdims=True), jnp.max(sb, axis=0, keepdims=True))

    s_ctx = lax.dot_general(kx_ref[...], qz, _NT, preferred_element_type=jnp.float32)
    m_ctx = jnp.max(s_ctx, axis=0, keepdims=True)
    r0 = _bf16_round(m_ctx)
    rhs0 = rhs_for(r0)
    sa, sb = rel_scores(0, rhs0), rel_scores(1, rhs0)
    acc_ref[...] = _diff_pv(vxt_ref[...], _diff_probs(s_ctx - r0))
    worst = col_max(sa, sb)
    p_buf[0][...] = _diff_probs(sa)
    p_buf[1][...] = _diff_probs(sb)

    def body(j, carry):
        run_max, pend, r_prev, r_cur, worst = carry
        for i in range(0, DIFF_STAGES, 2):
            c = DIFF_STAGES * j + i
            r_next = _bf16_round(run_max)
            rhs = rhs_for(r_next)
            pva = pv(c, p_buf[i % 4])
            sa = rel_scores(jnp.minimum(c + 2, n_chunks - 1), rhs)
            pvb = pv(c + 1, p_buf[(i + 1) % 4])
            sb = rel_scores(jnp.minimum(c + 3, n_chunks - 1), rhs)
            acc_ref[...] = jnp.exp2(r_prev - r_cur) * acc_ref[...] + pva + pvb
            top = col_max(sa, sb)
            p_buf[(i + 2) % 4][...] = _diff_probs(sa)
            p_buf[(i + 3) % 4][...] = _diff_probs(sb)
            worst = jnp.maximum(worst, top)
            run_max = jnp.maximum(run_max, pend)
            pend = top + r_next
            r_prev, r_cur = r_cur, r_next
        return run_max, pend, r_prev, r_cur, worst

    carry = lax.fori_loop(0, n_chunks // DIFF_STAGES, body, (m_ctx, worst + r0, r0, r0, worst))
    o_ref[...] = _diff_finalize(lam_ref, g_ref, acc_ref[...], lam_init)

    @pl.when(jnp.max(carry[4]) > DIFF_JUMP_LIMIT)
    def _():
        m0, acc0 = _diff_ctx_step(qz, kx_ref[...], vxt_ref[...])

        def exact(c, carry):
            m, acc = carry
            off = pl.multiple_of(c * tk, tk)
            s = lax.dot_general(k_ref[pl.ds(off, tk), :], qz, _NT, preferred_element_type=jnp.float32)
            m_new = jnp.maximum(m, jnp.max(s, axis=0, keepdims=True))
            acc = jnp.exp2(m - m_new) * acc + _diff_pv(vt_ref[:, pl.ds(off, tk)], _diff_probs(s - m_new))
            return m_new, acc

        _, acc = lax.fori_loop(0, n_chunks, exact, (m0, acc0))
        o_ref[...] = _diff_finalize(lam_ref, g_ref, acc, lam_init)


def _diff_attention(lamv, subln_col, q, k, vt, kx, vxt, *, has_latent, lam_init):
    n = q.shape[0]
    nx = kx.shape[0]
    tq = min(DIFF_TQ, n)
    tq_step = tq * DIFF_SUB if has_latent else tq
    hw = 2 * HEAD_DIM
    fixed = lambda h, i: (0, 0)
    in_specs = [pl.BlockSpec((4, HEAD_DIM), fixed), pl.BlockSpec((hw, 1), fixed),
                pl.BlockSpec((tq_step, hw), lambda h, i: (i, h))]
    args = [lamv, subln_col, q]
    scratch = []
    if has_latent:
        nk = k.shape[0]
        n_chunks = nk // DIFF_TK
        assert DIFF_STAGES % 4 == 0 and n_chunks % DIFF_STAGES == 0 and n_chunks * DIFF_TK == nk
        in_specs += [pl.BlockSpec((nk, hw), lambda h, i: (0, h)),
                     pl.BlockSpec((hw, nk), lambda h, i: (h + 1, 0))]
        args += [k, vt]
        body = functools.partial(_diff_kernel, lam_init=lam_init, n_chunks=n_chunks)
        scratch = ([pltpu.VMEM((DIFF_TK, 2 * tq), jnp.bfloat16)] * 4 +
                   [pltpu.VMEM((hw + SUM_ROWS, 2 * tq), jnp.float32)])
    else:
        body = functools.partial(_diff_ctx_kernel, lam_init=lam_init)
    in_specs += [pl.BlockSpec((nx, hw), lambda h, i: (0, h)),
                 pl.BlockSpec((hw, nx), lambda h, i: (h + 1, 0))]
    args += [kx, vxt]
    return pl.pallas_call(
        body,
        out_shape=jax.ShapeDtypeStruct((n, B_WIDTH), jnp.bfloat16),
        grid=(B_HEADS, n // tq_step),
        in_specs=in_specs,
        out_specs=pl.BlockSpec((tq_step, hw), lambda h, i: (i, h)),
        scratch_shapes=scratch,
        compiler_params=_params("parallel", "parallel"),
        name="diff_attn" if has_latent else "diff_attn_ctx",
    )(*args)


def _attn_out_rec_in_kernel(oa_ref, ob_ref, gate_ref, x_ref, gt_ref, w_ref,
                            g_ref, sc_ref, sh_ref, wr_ref, xl_ref, xr_ref, gr_ref):
    o = jnp.concatenate([oa_ref[...], ob_ref[...]], axis=-1).astype(jnp.float32)
    a = o * _silu(gate_ref[...].astype(jnp.float32))
    y = jnp.dot(a.astype(jnp.bfloat16), w_ref[...], preferred_element_type=jnp.float32)
    xl = x_ref[...] + gt_ref[...] * y
    xl_ref[...] = xl
    h = _norm_mod(xl, g_ref[...], sc_ref[...], sh_ref[...]).astype(jnp.bfloat16)
    p = jnp.dot(h, wr_ref[...], preferred_element_type=jnp.float32)
    xr_ref[...] = p[:, :RNN_WIDTH]
    gr_ref[...] = p[:, RNN_WIDTH:]


def _attn_out_rec_in(oa, ob, gate, x, gt, w, g, sc, sh, wr):
    n, d = x.shape
    tm = min(ROW_TILE, n)
    row = lambda i: (i, 0)
    fixed = lambda i: (0, 0)
    vec = pl.BlockSpec((1, d), fixed)
    return pl.pallas_call(
        _attn_out_rec_in_kernel,
        out_shape=[jax.ShapeDtypeStruct((n, d), jnp.float32),
                   jax.ShapeDtypeStruct((n, RNN_WIDTH), jnp.float32),
                   jax.ShapeDtypeStruct((n, RNN_WIDTH), jnp.float32)],
        grid=(n // tm,),
        in_specs=[
            pl.BlockSpec((tm, A_WIDTH), row),
            pl.BlockSpec((tm, B_WIDTH), row),
            pl.BlockSpec((tm, ATTN_WIDTH), row),
            pl.BlockSpec((tm, d), row),
            vec,
            pl.BlockSpec((ATTN_WIDTH, d), fixed),
            vec, vec, vec,
            pl.BlockSpec((d, 2 * RNN_WIDTH), fixed),
        ],
        out_specs=[pl.BlockSpec((tm, d), row), pl.BlockSpec((tm, RNN_WIDTH), row),
                   pl.BlockSpec((tm, RNN_WIDTH), row)],
        compiler_params=_params("parallel"),
        name="attn_out_rec_in",
    )(oa, ob, gate, x, gt, w, g, sc, sh, wr)


def _rec_scan_kernel(*refs, reverse, fuse_out, nt):
    if fuse_out:
        (x_ref, xp_ref, xn_ref, cw_ref, cb_ref, w_ref, bias_ref, lam_ref, h0_ref,
         yr_ref, gr_ref, xl_ref, gt_ref, wo_ref, fg_ref,
         o_ref, xe_ref, a_ref, b_ref, y_ref, h_ref) = refs
    else:
        (x_ref, xp_ref, xn_ref, cw_ref, cb_ref, w_ref, bias_ref, lam_ref, h0_ref,
         y_ref, hf_ref, xe_ref, a_ref, b_ref, h_ref) = refs
    i = pl.program_id(0)
    pos = i if not reverse else nt - 1 - i
    tm = x_ref.shape[0]

    @pl.when(i == 0)
    def _():
        h_ref[...] = h0_ref[...]

    xe_ref[0:SUBLANES, :] = jnp.where(pos > 0, xp_ref[...], 0.0)
    xe_ref[SUBLANES:SUBLANES + tm, :] = x_ref[...]
    xe_ref[SUBLANES + tm:, :] = jnp.where(pos < nt - 1, xn_ref[...], 0.0)
    cw = cw_ref[...]
    xe = xe_ref[...]
    taps = pltpu.roll(xe * cw[0:1], 1, 0)
    taps = pltpu.roll(taps + xe * cw[1:2], 1, 0)
    taps = taps + xe * cw[2:3] + pltpu.roll(xe * cw[3:4], tm + 2 * SUBLANES - 1, 0)
    uh = taps[SUBLANES:SUBLANES + tm] + cb_ref[...]
    ub = uh.astype(jnp.bfloat16)
    half = RNN_WIDTH // 2
    z0 = jnp.dot(ub[:, :half], w_ref[0], preferred_element_type=jnp.float32)
    z1 = jnp.dot(ub[:, half:], w_ref[1], preferred_element_type=jnp.float32)
    bias = bias_ref[...]
    ta = jnp.tanh(jnp.concatenate([z0[:, :half], z1[:, :half]], axis=1) + bias[:, :RNN_WIDTH])
    tx = jnp.tanh(jnp.concatenate([z0[:, half:], z1[:, half:]], axis=1) + bias[:, RNN_WIDTH:])
    nl = -lam_ref[...]
    softplus = jnp.maximum(nl, 0.0) + jnp.log1p(jnp.exp(-jnp.abs(nl)))
    coef = (-0.5 * RG_C * LOG2E) * softplus
    a = jnp.exp2(coef + coef * ta)
    a_ref[...] = a
    om = 1.0 - a * a
    b_ref[...] = (om * lax.rsqrt(jnp.maximum(om, 1e-30))) * ((1.0 + tx) * uh)

    def row(s, h):
        rr = (tm - 1 - s) if reverse else s
        h = a_ref[pl.ds(rr, 1), :] * h + b_ref[pl.ds(rr, 1), :]
        y_ref[pl.ds(rr, 1), :] = h
        return h

    h = lax.fori_loop(0, tm, row, h_ref[...], unroll=8 if fuse_out else 32)
    h_ref[...] = h

    if fuse_out:
        y = (y_ref[...] + yr_ref[...]) * _silu(gr_ref[...])
        out = jnp.dot(y.astype(jnp.bfloat16), wo_ref[...], preferred_element_type=jnp.float32)
        xl = xl_ref[...] + gt_ref[...] * out
        ms = jnp.mean(xl * xl, axis=-1, keepdims=True)
        o_ref[...] = xl * lax.rsqrt(ms + EPS) * fg_ref[...]
    else:
        hf_ref[...] = h


def _rec_scan(xr, cw, cb, w, bias, lam, h0, *, reverse, fused=None):
    assert CONV_WIDTH == 4
    n = xr.shape[0]
    tm = min(SCAN_TILE, n)
    nt = n // tm
    per = tm // SUBLANES
    nblk8 = n // SUBLANES
    pos = (lambda i: nt - 1 - i) if reverse else (lambda i: i)
    row = lambda i: (pos(i), 0)
    prev = lambda i: (jnp.maximum(pos(i) * per - 1, 0), 0)
    nxt = lambda i: (jnp.minimum((pos(i) + 1) * per, nblk8 - 1), 0)
    fixed = lambda i: (0, 0)
    vec = pl.BlockSpec((1, RNN_WIDTH), fixed)
    in_specs = [
        pl.BlockSpec((tm, RNN_WIDTH), row),
        pl.BlockSpec((SUBLANES, RNN_WIDTH), prev),
        pl.BlockSpec((SUBLANES, RNN_WIDTH), nxt),
        pl.BlockSpec((CONV_WIDTH, RNN_WIDTH), fixed),
        vec,
        pl.BlockSpec((2, RNN_WIDTH // 2, RNN_WIDTH), lambda i: (0, 0, 0)),
        pl.BlockSpec((1, 2 * RNN_WIDTH), fixed),
        vec,
        vec,
    ]
    args = [xr, xr, xr, cw, cb, w, bias, lam, h0]
    scratch = [pltpu.VMEM((tm + 2 * SUBLANES, RNN_WIDTH), jnp.float32),
               pltpu.VMEM((tm, RNN_WIDTH), jnp.float32),
               pltpu.VMEM((tm, RNN_WIDTH), jnp.float32)]
    if fused is not None:
        yr, gr, xl, gt, wo, fg = fused
        d = xl.shape[1]
        in_specs += [
            pl.BlockSpec((tm, RNN_WIDTH), row),
            pl.BlockSpec((tm, RNN_WIDTH), row),
            pl.BlockSpec((tm, d), row),
            pl.BlockSpec((1, d), fixed),
            pl.BlockSpec((RNN_WIDTH, d), fixed),
            pl.BlockSpec((1, d), fixed),
        ]
        args += [yr, gr, xl, gt, wo, fg]
        out_shape = jax.ShapeDtypeStruct((n, d), jnp.float32)
        out_specs = pl.BlockSpec((tm, d), row)
        scratch = scratch + [pltpu.VMEM((tm, RNN_WIDTH), jnp.float32)]
    else:
        out_shape = [jax.ShapeDtypeStruct((n, RNN_WIDTH), jnp.float32),
                     jax.ShapeDtypeStruct((1, RNN_WIDTH), jnp.float32)]
        out_specs = [pl.BlockSpec((tm, RNN_WIDTH), row), vec]
    scratch = scratch + [pltpu.VMEM((1, RNN_WIDTH), jnp.float32)]
    return pl.pallas_call(
        functools.partial(_rec_scan_kernel, reverse=reverse, fuse_out=fused is not None, nt=nt),
        out_shape=out_shape,
        grid=(nt,),
        in_specs=in_specs,
        out_specs=out_specs,
        scratch_shapes=scratch,
        compiler_params=_params("arbitrary"),
        name="rec_scan_out" if fused is not None else ("rec_scan_rev" if reverse else "rec_scan_fwd"),
    )(*args)


def _rope_tables(n):
    rows = n // GRID_W
    n_freq = HEAD_DIM // 4
    inv_freq = ROPE_BASE ** (-jnp.arange(n_freq, dtype=jnp.float32) / n_freq)
    ar = jnp.arange(rows, dtype=jnp.float32)[:, None] * inv_freq
    ac = jnp.arange(GRID_W, dtype=jnp.float32)[:, None] * inv_freq

    def table(fn, sign):
        r = jnp.broadcast_to(fn(ar)[:, None, :], (rows, GRID_W, n_freq)).reshape(n, n_freq)
        c = jnp.broadcast_to(fn(ac)[None, :, :], (rows, GRID_W, n_freq)).reshape(n, n_freq)
        return jnp.concatenate([sign * r, r, sign * c, c], axis=-1)

    return table(jnp.cos, 1.0), table(jnp.sin, -1.0)


def _block_diag(w):
    nb, bi, bj = w.shape
    on_diag = jnp.eye(nb, dtype=bool)[:, None, :, None]
    return jnp.where(on_diag, w[:, :, None, :], 0).reshape(nb * bi, nb * bj)


def kernel(x, c, ctx, c_ctx, norm_g, ada_w, ada_b, attn_w_in, attn_w_out, attn_sink, lam_q1, lam_k1,
           lam_q2, lam_k2, subln_g, rec_w_in, rec_conv_w, rec_conv_b, rec_wa, rec_ba, rec_wx, rec_bx,
           rec_lam, rec_w_out, final_g):
    assert x.shape[0] == 1 and c.shape[0] == 1 and ctx.shape[0] == 1
    bf = jnp.bfloat16
    d = D_MODEL
    xl = x[0]
    xc = ctx[0]
    n = xl.shape[0]

    cv = jnp.zeros((SUBLANES, d), jnp.float32).at[0].set(c[0]).at[1].set(c_ctx)
    mods = _adaln(cv, ada_w, ada_b)

    def mod(l, r):
        m = mods[l, r]
        return m[None, :d], m[None, d:2 * d], m[None, 2 * d:]

    lam_init = 0.8 - 0.6 * math.exp(-0.3 * 0)
    s1 = A_WIDTH
    s2 = s1 + A_KV_WIDTH
    s3 = s2 + A_KV_WIDTH
    s5 = s3 + 2 * B_WIDTH
    s6 = s5 + B_WIDTH
    head_perm = jnp.array([g * A_GROUP + j for j in range(A_GROUP) for g in range(A_KV_HEADS)])
    col_perm = (head_perm[:, None] * HEAD_DIM + jnp.arange(HEAD_DIM)).reshape(-1)
    w_in = attn_w_in[0]
    w_main = jnp.concatenate([w_in[:, :s1][:, col_perm], w_in[:, s1:s2], w_in[:, s3:s5],
                              w_in[:, s6:s6 + A_WIDTH][:, col_perm], w_in[:, s6 + A_WIDTH:]],
                             axis=1).astype(bf)
    w_vt = jnp.concatenate([w_in[:, s2:s3], w_in[:, s5:s6]], axis=1).T.astype(bf)
    w_out = attn_w_out[0]
    w_out = jnp.concatenate([w_out[:A_WIDTH][col_perm], w_out[A_WIDTH:]], axis=0).astype(bf)
    g0 = norm_g[0][None]
    cos, sin = _rope_tables(n)
    sh, sc, gt = mod(0, 0)
    shc, scc, gtc = mod(0, 1)
    qa, ka, qb, kb, vt, gate = _attn_in_proj(xl, g0, sc, sh, w_main, w_vt, cos, sin, rope=True)
    nc = xc.shape[0]
    qac, kac, qbc, kbc, vtc, gatec = _attn_in_proj(
        xc, g0, scc, shc, w_main, w_vt, cos[:nc], sin[:nc], rope=False)

    sink_gj = (attn_sink[0] * LOG2E).reshape(A_KV_HEADS, A_GROUP)
    kj = jnp.arange(3 * BLOCK)[:, None]
    qi = jnp.arange(BLOCK)[None, :]
    band_bias = jnp.where(jnp.abs(kj - BLOCK - qi) <= WINDOW, 0.0, NEG_INF).astype(jnp.float32)
    lamv = jnp.stack([lam_q1[0], lam_k1[0], lam_q2[0], lam_k2[0]])
    subln_col = subln_g[0][:, None]
    oa = _gqa_attention(jnp.repeat(sink_gj, BLOCK, axis=1), band_bias, qa, ka, vt, kac, vtc,
                        has_window=True)
    ob = _diff_attention(lamv, subln_col, qb, kb, vt, kbc, vtc, has_latent=True, lam_init=lam_init)
    oac = _gqa_attention(jnp.repeat(sink_gj, nc, axis=1), None, qac, None, None, kac, vtc,
                         has_window=False)
    obc = _diff_attention(lamv, subln_col, qbc, None, None, kbc, vtc, has_latent=False,
                          lam_init=lam_init)

    g1 = norm_g[1][None]
    sh, sc, gt1 = mod(1, 0)
    shc, scc, _ = mod(1, 1)
    w_rin = rec_w_in[0].astype(bf)
    xl, xr, gr = _attn_out_rec_in(oa, ob, gate, xl, gt, w_out, g1, sc, sh, w_rin)
    _, xrc, _ = _attn_out_rec_in(oac, obc, gatec, xc, gtc, w_out, g1, scc, shc, w_rin)
    gt = gt1
    cw = 0.5 * rec_conv_w[0]
    cb = 0.5 * rec_conv_b[0][None]
    zero_h = jnp.zeros((1, RNN_WIDTH), jnp.float32)

    def coeff(dr):
        hb = RNN_BLOCKS // 2
        wa, wx = rec_wa[0, dr].astype(bf), rec_wx[0, dr].astype(bf)
        w = jnp.stack([jnp.concatenate([_block_diag(wa[i * hb:(i + 1) * hb]),
                                        _block_diag(wx[i * hb:(i + 1) * hb])], axis=1)
                       for i in range(2)])
        bias = 0.5 * jnp.concatenate([rec_ba[0, dr], rec_bx[0, dr]])[None]
        return w, bias, rec_lam[0, dr][None]

    wf, bias_f, lam_f = coeff(0)
    wr, bias_r, lam_r = coeff(1)
    _, h0f = _rec_scan(xrc, cw, cb, wf, bias_f, lam_f, zero_h, reverse=False)
    _, h0r = _rec_scan(xrc, cw, cb, wr, bias_r, lam_r, zero_h, reverse=True)
    yr, _ = _rec_scan(xr, cw, cb, wr, bias_r, lam_r, h0r, reverse=True)
    out = _rec_scan(xr, cw, cb, wf, bias_f, lam_f, h0f, reverse=False,
                    fused=(yr, gr, xl, gt, rec_w_out[0].astype(bf), final_g[None]))
    return out[None]
```

```python
import functools
import math

import jax
import jax.numpy as jnp
from jax import lax
from jax.experimental import pallas as pl
from jax.experimental.pallas import tpu as pltpu

D_MODEL = 1024
GRID_W = 64
HEAD_DIM = 64
ROPE_BASE = 10000.0
EPS = 1e-6
NEG_INF = -1e30
A_Q_HEADS = 8
A_KV_HEADS = 2
A_GROUP = A_Q_HEADS // A_KV_HEADS
A_WIDTH = A_Q_HEADS * HEAD_DIM
A_KV_WIDTH = A_KV_HEADS * HEAD_DIM
WINDOW = 128
BLOCK = 128
B_HEADS = 4
B_WIDTH = B_HEADS * 2 * HEAD_DIM
ATTN_WIDTH = A_WIDTH + B_WIDTH
RNN_WIDTH = 1280
RNN_BLOCKS = 16
RNN_BLOCK_DIM = RNN_WIDTH // RNN_BLOCKS
CONV_WIDTH = 4
RG_C = 8.0
LOG2E = math.log2(math.e)
Q_SCALE_LOG2 = HEAD_DIM ** -0.5 * LOG2E

LANES = 128
SUBLANES = 8
VMEM_LIMIT = 48 * 1024 * 1024

ROW_TILE = 512
SCAN_TILE = 256
GQA_BLOCKS = 4
DIFF_TQ = 256
DIFF_TK = 512
DIFF_STAGES = 16
DIFF_JUMP_LIMIT = 64.0
SUM_ROWS = 16

_NT = (((1,), (1,)), ((), ()))


def _silu(x):
    return x * jax.nn.sigmoid(x)


def _norm_mod(x, g, sc, sh):
    ms = jnp.mean(x * x, axis=-1, keepdims=True)
    return (x * lax.rsqrt(ms + EPS) * g) * (1.0 + sc) + sh


def _params(*sem, flags=None):
    return pltpu.CompilerParams(dimension_semantics=sem, vmem_limit_bytes=VMEM_LIMIT, flags=flags)


def _adaln_kernel(cv_ref, w_ref, b_ref, o_ref):
    s = _silu(cv_ref[...])
    o_ref[0] = jnp.dot(s, w_ref[0], preferred_element_type=jnp.float32) + b_ref[0]


def _adaln(cv, ada_w, ada_b):
    depth, d, n3 = ada_w.shape
    tn = 1024
    return pl.pallas_call(
        _adaln_kernel,
        out_shape=jax.ShapeDtypeStruct((depth, SUBLANES, n3), jnp.float32),
        grid=(depth, n3 // tn),
        in_specs=[
            pl.BlockSpec((SUBLANES, d), lambda l, j: (0, 0)),
            pl.BlockSpec((1, d, tn), lambda l, j: (l, 0, j)),
            pl.BlockSpec((1, 1, tn), lambda l, j: (l, 0, j)),
        ],
        out_specs=pl.BlockSpec((1, SUBLANES, tn), lambda l, j: (l, 0, j)),
        compiler_params=_params("parallel", "parallel"),
        name="adaln",
    )(cv, ada_w, ada_b.reshape(depth, 1, n3))


def _rope(x, cos, sin_signed, first_half):
    rot = jnp.where(first_half, pltpu.roll(x, LANES - 16, 1), pltpu.roll(x, 16, 1))
    return x * cos + rot * sin_signed


def _attn_in_kernel(x_ref, g_ref, sc_ref, sh_ref, w_ref, wvt_ref, cos_ref, sin_ref,
                    qa_ref, ka_ref, qb_ref, kb_ref, vt_ref, gate_ref, *, rope):
    h = _norm_mod(x_ref[...], g_ref[...], sc_ref[...], sh_ref[...]).astype(jnp.bfloat16)
    p = jnp.dot(h, w_ref[...], preferred_element_type=jnp.float32)
    vt_ref[...] = lax.dot_general(wvt_ref[...], h, _NT,
                                  preferred_element_type=jnp.float32).astype(jnp.bfloat16)
    if rope:
        cos = cos_ref[...]
        sin = sin_ref[...]
        lane = lax.broadcasted_iota(jnp.int32, cos.shape, 1)
        first_half = (lane % 32) < 16

    def chunk(c):
        x = p[:, c * LANES:(c + 1) * LANES]
        return _rope(x, cos, sin, first_half) if rope else x

    o0 = A_WIDTH // LANES
    o1 = o0 + A_KV_WIDTH // LANES
    o2 = o1 + B_WIDTH // LANES
    o3 = o2 + B_WIDTH // LANES
    for c in range(o0):
        qa_ref[:, c * LANES:(c + 1) * LANES] = (chunk(c) * Q_SCALE_LOG2).astype(jnp.bfloat16)
    for c in range(o0, o1):
        ka_ref[:, (c - o0) * LANES:(c - o0 + 1) * LANES] = chunk(c).astype(jnp.bfloat16)
    for c in range(o1, o2):
        qb_ref[:, (c - o1) * LANES:(c - o1 + 1) * LANES] = (chunk(c) * Q_SCALE_LOG2).astype(jnp.bfloat16)
    for c in range(o2, o3):
        kb_ref[:, (c - o2) * LANES:(c - o2 + 1) * LANES] = chunk(c).astype(jnp.bfloat16)
    gate_ref[...] = p[:, o3 * LANES:].astype(jnp.bfloat16)


def _attn_in_proj(x, g, sc, sh, w, wvt, cos, sin, *, rope):
    n, d = x.shape
    tm = min(ROW_TILE, n)
    nw = w.shape[1]
    nv = wvt.shape[0]
    row = lambda i: (i, 0)
    fixed = lambda i: (0, 0)
    bf = jnp.bfloat16
    return pl.pallas_call(
        functools.partial(_attn_in_kernel, rope=rope),
        out_shape=[
            jax.ShapeDtypeStruct((n, A_WIDTH), bf),
            jax.ShapeDtypeStruct((n, A_KV_WIDTH), bf),
            jax.ShapeDtypeStruct((n, B_WIDTH), bf),
            jax.ShapeDtypeStruct((n, B_WIDTH), bf),
            jax.ShapeDtypeStruct((nv, n), bf),
            jax.ShapeDtypeStruct((n, ATTN_WIDTH), bf),
        ],
        grid=(n // tm,),
        in_specs=[
            pl.BlockSpec((tm, d), row),
            pl.BlockSpec((1, d), fixed),
            pl.BlockSpec((1, d), fixed),
            pl.BlockSpec((1, d), fixed),
            pl.BlockSpec((d, nw), fixed),
            pl.BlockSpec((nv, d), fixed),
            pl.BlockSpec((tm, LANES), row),
            pl.BlockSpec((tm, LANES), row),
        ],
        out_specs=[
            pl.BlockSpec((tm, A_WIDTH), row),
            pl.BlockSpec((tm, A_KV_WIDTH), row),
            pl.BlockSpec((tm, B_WIDTH), row),
            pl.BlockSpec((tm, B_WIDTH), row),
            pl.BlockSpec((nv, tm), lambda i: (0, i)),
            pl.BlockSpec((tm, ATTN_WIDTH), row),
        ],
        compiler_params=_params("parallel"),
        name="attn_in_proj",
    )(x, g, sc, sh, w, wvt, cos, sin)


def _gqa_scores(q, keys):
    tq = q.shape[0]
    lane = lax.broadcasted_iota(jnp.int32, (tq, LANES), 1)
    zero = jnp.zeros((tq, LANES), q.dtype)
    out = []
    for g in range(A_KV_HEADS):
        in_g = (lane >= g * HEAD_DIM) & (lane < (g + 1) * HEAD_DIM)
        qg = jnp.concatenate([jnp.where(in_g, q[:, j * LANES:(j + 1) * LANES], zero)
                              for j in range(A_GROUP)], axis=0)
        out.append(lax.dot_general(keys, qg, _NT, preferred_element_type=jnp.float32))
    return out


def _gqa_outputs(scores, vt, bias, sink_ref, tq):
    outs = [None] * A_Q_HEADS
    for g in range(A_KV_HEADS):
        s = scores[g]
        if bias is not None:
            s = jnp.concatenate([s[:3 * BLOCK] + bias, s[3 * BLOCK:]], axis=0)
        sk = sink_ref[g:g + 1, :]
        m = jnp.maximum(jnp.max(s, axis=0, keepdims=True), sk)
        acc = jnp.dot(vt, jnp.exp2((s - m).astype(jnp.bfloat16)),
                      preferred_element_type=jnp.float32)
        l = acc[A_KV_WIDTH:A_KV_WIDTH + 1] + jnp.exp2(sk - m)
        o = acc[g * HEAD_DIM:(g + 1) * HEAD_DIM] / l
        for j in range(A_GROUP):
            outs[2 * j + g] = o[:, j * tq:(j + 1) * tq]
    return jnp.concatenate(outs, axis=0).T


def _gqa_window_kernel(sink_ref, bias_ref, q_ref, *refs, nb):
    nk = GQA_BLOCKS + 2
    k_refs, vt_refs = refs[:nk], refs[nk:2 * nk]
    kx_ref, vxt_ref, o_ref = refs[2 * nk:]
    i = pl.program_id(0)
    ones = jnp.ones((SUM_ROWS, 3 * BLOCK + kx_ref.shape[0]), jnp.bfloat16)
    row = lax.broadcasted_iota(jnp.int32, (3 * BLOCK, 1), 0)
    all_scores, vts, biases = [], [], []
    for t in range(GQA_BLOCKS):
        keys = jnp.concatenate([r[...] for r in k_refs[t:t + 3]] + [kx_ref[...]], axis=0)
        vt = jnp.concatenate([r[...] for r in vt_refs[t:t + 3]] + [vxt_ref[...]], axis=1)
        vts.append(jnp.concatenate([vt, ones], axis=0))
        all_scores.append(_gqa_scores(q_ref[t * BLOCK:(t + 1) * BLOCK, :], keys))
        b = GQA_BLOCKS * i + t
        off_end = ((row < BLOCK) & (b == 0)) | ((row >= 2 * BLOCK) & (b == nb - 1))
        bias = bias_ref[...] + jnp.where(off_end, NEG_INF, 0.0)
        biases.append(jnp.concatenate([bias] * A_GROUP, axis=1))
    for t in range(GQA_BLOCKS):
        o = _gqa_outputs(all_scores[t], vts[t], biases[t], sink_ref, BLOCK)
        o_ref[t * BLOCK:(t + 1) * BLOCK, :] = o.astype(o_ref.dtype)


def _gqa_ctx_kernel(sink_ref, q_ref, kx_ref, vxt_ref, o_ref):
    vt = vxt_ref[...]
    vt = jnp.concatenate([vt, jnp.ones((SUM_ROWS, vt.shape[1]), vt.dtype)], axis=0)
    q = q_ref[...]
    o = _gqa_outputs(_gqa_scores(q, kx_ref[...]), vt, None, sink_ref, q.shape[0])
    o_ref[...] = o.astype(o_ref.dtype)


def _gqa_attention(sink_cols, bias, q, k, vt, kx, vxt, *, has_window):
    n = q.shape[0]
    nx = kx.shape[0]
    fixed = lambda i: (0, 0)
    if has_window:
        nb = n // BLOCK
        tq = GQA_BLOCKS * BLOCK
        steps = nb // GQA_BLOCKS
        assert steps * GQA_BLOCKS == nb
        offs = range(-1, GQA_BLOCKS + 1)
        blk = lambda i, o: jnp.clip(GQA_BLOCKS * i + o, 0, nb - 1)
        k_specs = [pl.BlockSpec((BLOCK, A_KV_WIDTH), lambda i, o=o: (blk(i, o), 0)) for o in offs]
        vt_specs = [pl.BlockSpec((A_KV_WIDTH, BLOCK), lambda i, o=o: (0, blk(i, o))) for o in offs]
        in_specs = ([pl.BlockSpec((A_KV_HEADS, A_GROUP * BLOCK), fixed),
                     pl.BlockSpec((3 * BLOCK, BLOCK), fixed),
                     pl.BlockSpec((tq, A_WIDTH), lambda i: (i, 0))] + k_specs + vt_specs +
                    [pl.BlockSpec((nx, A_KV_WIDTH), fixed), pl.BlockSpec((A_KV_WIDTH, nx), fixed)])
        args = (sink_cols, bias, q) + (k,) * len(offs) + (vt,) * len(offs) + (kx, vxt)
        body = functools.partial(_gqa_window_kernel, nb=nb)
    else:
        steps = 1
        tq = n
        in_specs = [pl.BlockSpec((A_KV_HEADS, A_GROUP * tq), fixed), pl.BlockSpec((tq, A_WIDTH), fixed),
                    pl.BlockSpec((nx, A_KV_WIDTH), fixed), pl.BlockSpec((A_KV_WIDTH, nx), fixed)]
        args = (sink_cols, q, kx, vxt)
        body = _gqa_ctx_kernel
    return pl.pallas_call(
        body,
        out_shape=jax.ShapeDtypeStruct((n, A_WIDTH), jnp.bfloat16),
        grid=(steps,),
        in_specs=in_specs,
        out_specs=pl.BlockSpec((tq, A_WIDTH), lambda i: (i, 0)),
        compiler_params=_params("parallel"),
        name="gqa_window" if has_window else "gqa_ctx",
    )(*args)


def _diff_masked_q(q_ref):
    q = q_ref[...]
    lane = lax.broadcasted_iota(jnp.int32, q.shape, 1)
    zero = jnp.zeros_like(q)
    return jnp.concatenate([jnp.where(lane < HEAD_DIM, q, zero),
                            jnp.where(lane >= HEAD_DIM, q, zero)], axis=0)


def _diff_probs(s_rel):
    return jnp.exp2(s_rel.astype(jnp.bfloat16))


def _diff_pv(vt, p):
    ones = jnp.ones((SUM_ROWS, vt.shape[1]), vt.dtype)
    return jnp.dot(jnp.concatenate([vt, ones], axis=0), p, preferred_element_type=jnp.float32)


def _diff_ctx_step(qz, kx, vxt):
    s = lax.dot_general(kx, qz, _NT, preferred_element_type=jnp.float32)
    m = jnp.max(s, axis=0, keepdims=True)
    return m, _diff_pv(vxt, _diff_probs(s - m))


def _diff_finalize(lam_ref, g_ref, acc, lam_init):
    hw = 2 * HEAD_DIM
    tq = acc.shape[1] // 2
    lv = lam_ref[...]
    lam = (jnp.exp(jnp.sum(lv[0:1] * lv[1:2], keepdims=True))
           - jnp.exp(jnp.sum(lv[2:3] * lv[3:4], keepdims=True)) + lam_init)
    on = acc[:hw] / acc[hw:hw + 1]
    o = on[:, :tq] - lam * on[:, tq:]
    ms = jnp.mean(o * o, axis=0, keepdims=True)
    o = (o * lax.rsqrt(ms + EPS)) * g_ref[...] * (1.0 - lam_init)
    return o.T.astype(jnp.bfloat16)


def _diff_ctx_kernel(lam_ref, g_ref, q_ref, kx_ref, vxt_ref, o_ref, *, lam_init):
    _, acc = _diff_ctx_step(_diff_masked_q(q_ref), kx_ref[...], vxt_ref[...])
    o_ref[...] = _diff_finalize(lam_ref, g_ref, acc, lam_init)


def _bf16_round(x):
    return x.astype(jnp.bfloat16).astype(jnp.float32)


def _diff_kernel(lam_ref, g_ref, q_ref, k_ref, vt_ref, kx_ref, vxt_ref, o_ref,
                 p0, p1, p2, p3, acc_ref, *, lam_init, n_chunks):
    tk = DIFF_TK
    qz = _diff_masked_q(q_ref)
    cols = qz.shape[0]
    qzt = qz.astype(jnp.float32).T.astype(jnp.bfloat16)
    ones_k = jnp.ones((tk, LANES), jnp.bfloat16)
    first_row = lax.broadcasted_iota(jnp.int32, (2 * SUBLANES, cols), 0) == 0
    zero_rows = jnp.zeros((LANES - 2 * SUBLANES, cols), jnp.bfloat16)
    p_buf = (p0, p1, p2, p3)

    def rhs_for(r):
        neg_r = jnp.where(first_row, -r, 0.0).astype(jnp.bfloat16)
        return jnp.concatenate([qzt, neg_r, zero_rows], axis=0)

    def rel_scores(c, rhs):
        k = k_ref[pl.ds(pl.multiple_of(c * tk, tk), tk), :]
        return jnp.dot(jnp.concatenate([k, ones_k], axis=1), rhs, preferred_element_type=jnp.float32)

    def pv(c, p_ref):
        return _diff_pv(vt_ref[:, pl.ds(pl.multiple_of(c * tk, tk), tk)], p_ref[...])

    def col_max(sa, sb):
        return jnp.maximum(jnp.max(sa, axis=0, keepdims=True), jnp.max(sb, axis=0, keepdims=True))

    s_ctx = lax.dot_general(kx_ref[...], qz, _NT, preferred_element_type=jnp.float32)
    m_ctx = jnp.max(s_ctx, axis=0, keepdims=True)
    r0 = _bf16_round(m_ctx)
    rhs0 = rhs_for(r0)
    sa, sb = rel_scores(0, rhs0), rel_scores(1, rhs0)
    acc_ref[...] = _diff_pv(vxt_ref[...], _diff_probs(s_ctx - r0))
    worst = col_max(sa, sb)
    p_buf[0][...] = _diff_probs(sa)
    p_buf[1][...] = _diff_probs(sb)

    def body(j, carry):
        run_max, pend, r_prev, r_cur, worst = carry
        for i in range(0, DIFF_STAGES, 2):
            c = DIFF_STAGES * j + i
            r_next = _bf16_round(run_max)
            rhs = rhs_for(r_next)
            pva = pv(c, p_buf[i % 4])
            sa = rel_scores(jnp.minimum(c + 2, n_chunks - 1), rhs)
            pvb = pv(c + 1, p_buf[(i + 1) % 4])
            sb = rel_scores(jnp.minimum(c + 3, n_chunks - 1), rhs)
            acc_ref[...] = jnp.exp2(r_prev - r_cur) * acc_ref[...] + pva + pvb
            top = col_max(sa, sb)
            p_buf[(i + 2) % 4][...] = _diff_probs(sa)
            p_buf[(i + 3) % 4][...] = _diff_probs(sb)
            worst = jnp.maximum(worst, top)
            run_max = jnp.maximum(run_max, pend)
            pend = top + r_next
            r_prev, r_cur = r_cur, r_next
        return run_max, pend, r_prev, r_cur, worst

    carry = lax.fori_loop(0, n_chunks // DIFF_STAGES, body, (m_ctx, worst + r0, r0, r0, worst))
    o_ref[...] = _diff_finalize(lam_ref, g_ref, acc_ref[...], lam_init)

    @pl.when(jnp.max(carry[4]) > DIFF_JUMP_LIMIT)
    def _():
        m0, acc0 = _diff_ctx_step(qz, kx_ref[...], vxt_ref[...])

        def exact(c, carry):
            m, acc = carry
            off = pl.multiple_of(c * tk, tk)
            s = lax.dot_general(k_ref[pl.ds(off, tk), :], qz, _NT, preferred_element_type=jnp.float32)
            m_new = jnp.maximum(m, jnp.max(s, axis=0, keepdims=True))
            acc = jnp.exp2(m - m_new) * acc + _diff_pv(vt_ref[:, pl.ds(off, tk)], _diff_probs(s - m_new))
            return m_new, acc

        _, acc = lax.fori_loop(0, n_chunks, exact, (m0, acc0))
        o_ref[...] = _diff_finalize(lam_ref, g_ref, acc, lam_init)


def _diff_attention(lamv, subln_col, q, k, vt, kx, vxt, *, has_latent, lam_init):
    n = q.shape[0]
    nx = kx.shape[0]
    tq = min(DIFF_TQ, n)
    hw = 2 * HEAD_DIM
    fixed = lambda h, i: (0, 0)
    in_specs = [pl.BlockSpec((4, HEAD_DIM), fixed), pl.BlockSpec((hw, 1), fixed),
                pl.BlockSpec((tq, hw), lambda h, i: (i, h))]
    args = [lamv, subln_col, q]
    scratch = []
    if has_latent:
        nk = k.shape[0]
        n_chunks = nk // DIFF_TK
        assert DIFF_STAGES % 4 == 0 and n_chunks % DIFF_STAGES == 0 and n_chunks * DIFF_TK == nk
        in_specs += [pl.BlockSpec((nk, hw), lambda h, i: (0, h)),
                     pl.BlockSpec((hw, nk), lambda h, i: (h + 1, 0))]
        args += [k, vt]
        body = functools.partial(_diff_kernel, lam_init=lam_init, n_chunks=n_chunks)
        scratch = ([pltpu.VMEM((DIFF_TK, 2 * tq), jnp.bfloat16)] * 4 +
                   [pltpu.VMEM((hw + SUM_ROWS, 2 * tq), jnp.float32)])
    else:
        body = functools.partial(_diff_ctx_kernel, lam_init=lam_init)
    in_specs += [pl.BlockSpec((nx, hw), lambda h, i: (0, h)),
                 pl.BlockSpec((hw, nx), lambda h, i: (h + 1, 0))]
    args += [kx, vxt]
    return pl.pallas_call(
        body,
        out_shape=jax.ShapeDtypeStruct((n, B_WIDTH), jnp.bfloat16),
        grid=(B_HEADS, n // tq),
        in_specs=in_specs,
        out_specs=pl.BlockSpec((tq, hw), lambda h, i: (i, h)),
        scratch_shapes=scratch,
        compiler_params=_params("parallel", "parallel"),
        name="diff_attn" if has_latent else "diff_attn_ctx",
    )(*args)


def _attn_out_rec_in_kernel(oa_ref, ob_ref, gate_ref, x_ref, gt_ref, w_ref,
                            g_ref, sc_ref, sh_ref, wr_ref, xl_ref, xr_ref, gr_ref):
    o = jnp.concatenate([oa_ref[...], ob_ref[...]], axis=-1).astype(jnp.float32)
    a = o * _silu(gate_ref[...].astype(jnp.float32))
    y = jnp.dot(a.astype(jnp.bfloat16), w_ref[...], preferred_element_type=jnp.float32)
    xl = x_ref[...] + gt_ref[...] * y
    xl_ref[...] = xl
    h = _norm_mod(xl, g_ref[...], sc_ref[...], sh_ref[...]).astype(jnp.bfloat16)
    p = jnp.dot(h, wr_ref[...], preferred_element_type=jnp.float32)
    xr_ref[...] = p[:, :RNN_WIDTH]
    gr_ref[...] = p[:, RNN_WIDTH:]


def _attn_out_rec_in(oa, ob, gate, x, gt, w, g, sc, sh, wr):
    n, d = x.shape
    tm = min(ROW_TILE, n)
    row = lambda i: (i, 0)
    fixed = lambda i: (0, 0)
    vec = pl.BlockSpec((1, d), fixed)
    return pl.pallas_call(
        _attn_out_rec_in_kernel,
        out_shape=[jax.ShapeDtypeStruct((n, d), jnp.float32),
                   jax.ShapeDtypeStruct((n, RNN_WIDTH), jnp.float32),
                   jax.ShapeDtypeStruct((n, RNN_WIDTH), jnp.float32)],
        grid=(n // tm,),
        in_specs=[
            pl.BlockSpec((tm, A_WIDTH), row),
            pl.BlockSpec((tm, B_WIDTH), row),
            pl.BlockSpec((tm, ATTN_WIDTH), row),
            pl.BlockSpec((tm, d), row),
            vec,
            pl.BlockSpec((ATTN_WIDTH, d), fixed),
            vec, vec, vec,
            pl.BlockSpec((d, 2 * RNN_WIDTH), fixed),
        ],
        out_specs=[pl.BlockSpec((tm, d), row), pl.BlockSpec((tm, RNN_WIDTH), row),
                   pl.BlockSpec((tm, RNN_WIDTH), row)],
        compiler_params=_params("parallel"),
        name="attn_out_rec_in",
    )(oa, ob, gate, x, gt, w, g, sc, sh, wr)


def _rec_scan_kernel(*refs, reverse, fuse_out, nt):
    if fuse_out:
        (x_ref, xp_ref, xn_ref, cw_ref, cb_ref, w_ref, bias_ref, lam_ref, h0_ref,
         yr_ref, gr_ref, xl_ref, gt_ref, wo_ref, fg_ref,
         o_ref, xe_ref, a_ref, b_ref, y_ref, h_ref) = refs
    else:
        (x_ref, xp_ref, xn_ref, cw_ref, cb_ref, w_ref, bias_ref, lam_ref, h0_ref,
         y_ref, hf_ref, xe_ref, a_ref, b_ref, h_ref) = refs
    i = pl.program_id(0)
    pos = i if not reverse else nt - 1 - i
    tm = x_ref.shape[0]

    @pl.when(i == 0)
    def _():
        h_ref[...] = h0_ref[...]

    xe_ref[0:SUBLANES, :] = jnp.where(pos > 0, xp_ref[...], 0.0)
    xe_ref[SUBLANES:SUBLANES + tm, :] = x_ref[...]
    xe_ref[SUBLANES + tm:, :] = jnp.where(pos < nt - 1, xn_ref[...], 0.0)
    cw = cw_ref[...]
    xe = xe_ref[...]
    taps = pltpu.roll(xe * cw[0:1], 1, 0)
    taps = pltpu.roll(taps + xe * cw[1:2], 1, 0)
    taps = taps + xe * cw[2:3] + pltpu.roll(xe * cw[3:4], tm + 2 * SUBLANES - 1, 0)
    uh = taps[SUBLANES:SUBLANES + tm] + cb_ref[...]
    ub = uh.astype(jnp.bfloat16)
    half = RNN_WIDTH // 2
    z0 = jnp.dot(ub[:, :half], w_ref[0], preferred_element_type=jnp.float32)
    z1 = jnp.dot(ub[:, half:], w_ref[1], preferred_element_type=jnp.float32)
    bias = bias_ref[...]
    ta = jnp.tanh(jnp.concatenate([z0[:, :half], z1[:, :half]], axis=1) + bias[:, :RNN_WIDTH])
    tx = jnp.tanh(jnp.concatenate([z0[:, half:], z1[:, half:]], axis=1) + bias[:, RNN_WIDTH:])
    nl = -lam_ref[...]
    softplus = jnp.maximum(nl, 0.0) + jnp.log1p(jnp.exp(-jnp.abs(nl)))
    coef = (-0.5 * RG_C * LOG2E) * softplus
    a = jnp.exp2(coef + coef * ta)
    a_ref[...] = a
    om = 1.0 - a * a
    b_ref[...] = (om * lax.rsqrt(jnp.maximum(om, 1e-30))) * ((1.0 + tx) * uh)

    def row(s, h):
        rr = (tm - 1 - s) if reverse else s
        h = a_ref[pl.ds(rr, 1), :] * h + b_ref[pl.ds(rr, 1), :]
        y_ref[pl.ds(rr, 1), :] = h
        return h

    h = lax.fori_loop(0, tm, row, h_ref[...], unroll=8 if fuse_out else 32)
    h_ref[...] = h

    if fuse_out:
        y = (y_ref[...] + yr_ref[...]) * _silu(gr_ref[...])
        out = jnp.dot(y.astype(jnp.bfloat16), wo_ref[...], preferred_element_type=jnp.float32)
        xl = xl_ref[...] + gt_ref[...] * out
        ms = jnp.mean(xl * xl, axis=-1, keepdims=True)
        o_ref[...] = xl * lax.rsqrt(ms + EPS) * fg_ref[...]
    else:
        hf_ref[...] = h


def _rec_scan(xr, cw, cb, w, bias, lam, h0, *, reverse, fused=None):
    assert CONV_WIDTH == 4
    n = xr.shape[0]
    tm = min(SCAN_TILE, n)
    nt = n // tm
    per = tm // SUBLANES
    nblk8 = n // SUBLANES
    pos = (lambda i: nt - 1 - i) if reverse else (lambda i: i)
    row = lambda i: (pos(i), 0)
    prev = lambda i: (jnp.maximum(pos(i) * per - 1, 0), 0)
    nxt = lambda i: (jnp.minimum((pos(i) + 1) * per, nblk8 - 1), 0)
    fixed = lambda i: (0, 0)
    vec = pl.BlockSpec((1, RNN_WIDTH), fixed)
    in_specs = [
        pl.BlockSpec((tm, RNN_WIDTH), row),
        pl.BlockSpec((SUBLANES, RNN_WIDTH), prev),
        pl.BlockSpec((SUBLANES, RNN_WIDTH), nxt),
        pl.BlockSpec((CONV_WIDTH, RNN_WIDTH), fixed),
        vec,
        pl.BlockSpec((2, RNN_WIDTH // 2, RNN_WIDTH), lambda i: (0, 0, 0)),
        pl.BlockSpec((1, 2 * RNN_WIDTH), fixed),
        vec,
        vec,
    ]
    args = [xr, xr, xr, cw, cb, w, bias, lam, h0]
    scratch = [pltpu.VMEM((tm + 2 * SUBLANES, RNN_WIDTH), jnp.float32),
               pltpu.VMEM((tm, RNN_WIDTH), jnp.float32),
               pltpu.VMEM((tm, RNN_WIDTH), jnp.float32)]
    if fused is not None:
        yr, gr, xl, gt, wo, fg = fused
        d = xl.shape[1]
        in_specs += [
            pl.BlockSpec((tm, RNN_WIDTH), row),
            pl.BlockSpec((tm, RNN_WIDTH), row),
            pl.BlockSpec((tm, d), row),
            pl.BlockSpec((1, d), fixed),
            pl.BlockSpec((RNN_WIDTH, d), fixed),
            pl.BlockSpec((1, d), fixed),
        ]
        args += [yr, gr, xl, gt, wo, fg]
        out_shape = jax.ShapeDtypeStruct((n, d), jnp.float32)
        out_specs = pl.BlockSpec((tm, d), row)
        scratch = scratch + [pltpu.VMEM((tm, RNN_WIDTH), jnp.float32)]
    else:
        out_shape = [jax.ShapeDtypeStruct((n, RNN_WIDTH), jnp.float32),
                     jax.ShapeDtypeStruct((1, RNN_WIDTH), jnp.float32)]
        out_specs = [pl.BlockSpec((tm, RNN_WIDTH), row), vec]
    scratch = scratch + [pltpu.VMEM((1, RNN_WIDTH), jnp.float32)]
    return pl.pallas_call(
        functools.partial(_rec_scan_kernel, reverse=reverse, fuse_out=fused is not None, nt=nt),
        out_shape=out_shape,
        grid=(nt,),
        in_specs=in_specs,
        out_specs=out_specs,
        scratch_shapes=scratch,
        compiler_params=_params("arbitrary"),
        name="rec_scan_out" if fused is not None else ("rec_scan_rev" if reverse else "rec_scan_fwd"),
    )(*args)


def _rope_tables(n):
    rows = n // GRID_W
    n_freq = HEAD_DIM // 4
    inv_freq = ROPE_BASE ** (-jnp.arange(n_freq, dtype=jnp.float32) / n_freq)
    ar = jnp.arange(rows, dtype=jnp.float32)[:, None] * inv_freq
    ac = jnp.arange(GRID_W, dtype=jnp.float32)[:, None] * inv_freq

    def table(fn, sign):
        r = jnp.broadcast_to(fn(ar)[:, None, :], (rows, GRID_W, n_freq)).reshape(n, n_freq)
        c = jnp.broadcast_to(fn(ac)[None, :, :], (rows, GRID_W, n_freq)).reshape(n, n_freq)
        head = jnp.concatenate([sign * r, r, sign * c, c], axis=-1)
        return jnp.tile(head, (1, 2))

    return table(jnp.cos, 1.0), table(jnp.sin, -1.0)


def _block_diag(w):
    nb, bi, bj = w.shape
    on_diag = jnp.eye(nb, dtype=bool)[:, None, :, None]
    return jnp.where(on_diag, w[:, :, None, :], 0).reshape(nb * bi, nb * bj)


def kernel(x, c, ctx, c_ctx, norm_g, ada_w, ada_b, attn_w_in, attn_w_out, attn_sink, lam_q1, lam_k1,
           lam_q2, lam_k2, subln_g, rec_w_in, rec_conv_w, rec_conv_b, rec_wa, rec_ba, rec_wx, rec_bx,
           rec_lam, rec_w_out, final_g):
    assert x.shape[0] == 1 and c.shape[0] == 1 and ctx.shape[0] == 1
    bf = jnp.bfloat16
    d = D_MODEL
    xl = x[0]
    xc = ctx[0]
    n = xl.shape[0]

    cv = jnp.zeros((SUBLANES, d), jnp.float32).at[0].set(c[0]).at[1].set(c_ctx)
    mods = _adaln(cv, ada_w, ada_b)

    def mod(l, r):
        m = mods[l, r]
        return m[None, :d], m[None, d:2 * d], m[None, 2 * d:]

    lam_init = 0.8 - 0.6 * math.exp(-0.3 * 0)
    s1 = A_WIDTH
    s2 = s1 + A_KV_WIDTH
    s3 = s2 + A_KV_WIDTH
    s5 = s3 + 2 * B_WIDTH
    s6 = s5 + B_WIDTH
    head_perm = jnp.array([g * A_GROUP + j for j in range(A_GROUP) for g in range(A_KV_HEADS)])
    col_perm = (head_perm[:, None] * HEAD_DIM + jnp.arange(HEAD_DIM)).reshape(-1)
    w_in = attn_w_in[0]
    w_main = jnp.concatenate([w_in[:, :s1][:, col_perm], w_in[:, s1:s2], w_in[:, s3:s5],
                              w_in[:, s6:s6 + A_WIDTH][:, col_perm], w_in[:, s6 + A_WIDTH:]],
                             axis=1).astype(bf)
    w_vt = jnp.concatenate([w_in[:, s2:s3], w_in[:, s5:s6]], axis=1).T.astype(bf)
    w_out = attn_w_out[0]
    w_out = jnp.concatenate([w_out[:A_WIDTH][col_perm], w_out[A_WIDTH:]], axis=0).astype(bf)
    g0 = norm_g[0][None]
    cos, sin = _rope_tables(n)
    sh, sc, gt = mod(0, 0)
    shc, scc, gtc = mod(0, 1)
    qa, ka, qb, kb, vt, gate = _attn_in_proj(xl, g0, sc, sh, w_main, w_vt, cos, sin, rope=True)
    nc = xc.shape[0]
    qac, kac, qbc, kbc, vtc, gatec = _attn_in_proj(
        xc, g0, scc, shc, w_main, w_vt, cos[:nc], sin[:nc], rope=False)

    sink_gj = (attn_sink[0] * LOG2E).reshape(A_KV_HEADS, A_GROUP)
    kj = jnp.arange(3 * BLOCK)[:, None]
    qi = jnp.arange(BLOCK)[None, :]
    band_bias = jnp.where(jnp.abs(kj - BLOCK - qi) <= WINDOW, 0.0, NEG_INF).astype(jnp.float32)
    lamv = jnp.stack([lam_q1[0], lam_k1[0], lam_q2[0], lam_k2[0]])
    subln_col = subln_g[0][:, None]
    oa = _gqa_attention(jnp.repeat(sink_gj, BLOCK, axis=1), band_bias, qa, ka, vt, kac, vtc,
                        has_window=True)
    ob = _diff_attention(lamv, subln_col, qb, kb, vt, kbc, vtc, has_latent=True, lam_init=lam_init)
    oac = _gqa_attention(jnp.repeat(sink_gj, nc, axis=1), None, qac, None, None, kac, vtc,
                         has_window=False)
    obc = _diff_attention(lamv, subln_col, qbc, None, None, kbc, vtc, has_latent=False,
                          lam_init=lam_init)

    g1 = norm_g[1][None]
    sh, sc, gt1 = mod(1, 0)
    shc, scc, _ = mod(1, 1)
    w_rin = rec_w_in[0].astype(bf)
    xl, xr, gr = _attn_out_rec_in(oa, ob, gate, xl, gt, w_out, g1, sc, sh, w_rin)
    _, xrc, _ = _attn_out_rec_in(oac, obc, gatec, xc, gtc, w_out, g1, scc, shc, w_rin)
    gt = gt1
    cw = 0.5 * rec_conv_w[0]
    cb = 0.5 * rec_conv_b[0][None]
    zero_h = jnp.zeros((1, RNN_WIDTH), jnp.float32)

    def coeff(dr):
        hb = RNN_BLOCKS // 2
        wa, wx = rec_wa[0, dr].astype(bf), rec_wx[0, dr].astype(bf)
        w = jnp.stack([jnp.concatenate([_block_diag(wa[i * hb:(i + 1) * hb]),
                                        _block_diag(wx[i * hb:(i + 1) * hb])], axis=1)
                       for i in range(2)])
        bias = 0.5 * jnp.concatenate([rec_ba[0, dr], rec_bx[0, dr]])[None]
        return w, bias, rec_lam[0, dr][None]

    wf, bias_f, lam_f = coeff(0)
    wr, bias_r, lam_r = coeff(1)
    _, h0f = _rec_scan(xrc, cw, cb, wf, bias_f, lam_f, zero_h, reverse=False)
    _, h0r = _rec_scan(xrc, cw, cb, wr, bias_r, lam_r, zero_h, reverse=True)
    yr, _ = _rec_scan(xr, cw, cb, wr, bias_r, lam_r, h0r, reverse=True)
    out = _rec_scan(xr, cw, cb, wf, bias_f, lam_f, h0f, reverse=False,
                    fused=(yr, gr, xl, gt, rec_w_out[0].astype(bf), final_g[None]))
    return out[None]
```

```python
import functools
import math

import jax
import jax.numpy as jnp
from jax import lax
from jax.experimental import pallas as pl
from jax.experimental.pallas import tpu as pltpu

D_MODEL = 1024
GRID_W = 64
HEAD_DIM = 64
ROPE_BASE = 10000.0
EPS = 1e-6
NEG_INF = -1e30
A_Q_HEADS = 8
A_KV_HEADS = 2
A_GROUP = A_Q_HEADS // A_KV_HEADS
A_WIDTH = A_Q_HEADS * HEAD_DIM
A_KV_WIDTH = A_KV_HEADS * HEAD_DIM
WINDOW = 128
BLOCK = 128
B_HEADS = 4
B_WIDTH = B_HEADS * 2 * HEAD_DIM
ATTN_WIDTH = A_WIDTH + B_WIDTH
RNN_WIDTH = 1280
RNN_BLOCKS = 16
CONV_WIDTH = 4
RG_C = 8.0
LOG2E = math.log2(math.e)
Q_SCALE_LOG2 = HEAD_DIM ** -0.5 * LOG2E

LANES = 128
SUBLANES = 8
VMEM_LIMIT = 48 * 1024 * 1024

ROW_TILE = 512
SCAN_TILE = 256
GQA_BLOCKS = 4
DIFF_TQ = 256
DIFF_TK = 512
DIFF_STAGES = 16
DIFF_JUMP_LIMIT = 64.0
SUM_ROWS = 16

_NT = (((1,), (1,)), ((), ()))


def _silu(x):
    return x * jax.nn.sigmoid(x)


def _norm_mod(x, g, sc, sh):
    ms = jnp.mean(x * x, axis=-1, keepdims=True)
    return (x * lax.rsqrt(ms + EPS) * g) * (1.0 + sc) + sh


def _params(*sem):
    return pltpu.CompilerParams(dimension_semantics=sem, vmem_limit_bytes=VMEM_LIMIT)


def _adaln_kernel(cv_ref, w_ref, b_ref, o_ref):
    s = _silu(cv_ref[...])
    o_ref[0] = jnp.dot(s, w_ref[0], preferred_element_type=jnp.float32) + b_ref[0]


def _adaln(cv, ada_w, ada_b):
    depth, d, n3 = ada_w.shape
    tn = 1024
    return pl.pallas_call(
        _adaln_kernel,
        out_shape=jax.ShapeDtypeStruct((depth, SUBLANES, n3), jnp.float32),
        grid=(depth, n3 // tn),
        in_specs=[
            pl.BlockSpec((SUBLANES, d), lambda l, j: (0, 0)),
            pl.BlockSpec((1, d, tn), lambda l, j: (l, 0, j)),
            pl.BlockSpec((1, 1, tn), lambda l, j: (l, 0, j)),
        ],
        out_specs=pl.BlockSpec((1, SUBLANES, tn), lambda l, j: (l, 0, j)),
        compiler_params=_params("parallel", "parallel"),
        name="adaln",
    )(cv, ada_w, ada_b.reshape(depth, 1, n3))


def _rope(x, cos, sin_signed, first_half):
    rot = jnp.where(first_half, pltpu.roll(x, LANES - 16, 1), pltpu.roll(x, 16, 1))
    return x * cos + rot * sin_signed


def _attn_in_kernel(x_ref, g_ref, sc_ref, sh_ref, w_ref, wvt_ref, cos_ref, sin_ref,
                    qa_ref, ka_ref, qb_ref, kb_ref, vt_ref, gate_ref, *, rope):
    h = _norm_mod(x_ref[...], g_ref[...], sc_ref[...], sh_ref[...]).astype(jnp.bfloat16)
    p = jnp.dot(h, w_ref[...], preferred_element_type=jnp.float32)
    vt_ref[...] = lax.dot_general(wvt_ref[...], h, _NT,
                                  preferred_element_type=jnp.float32).astype(jnp.bfloat16)
    if rope:
        cos = cos_ref[...]
        sin = sin_ref[...]
        lane = lax.broadcasted_iota(jnp.int32, cos.shape, 1)
        first_half = (lane % 32) < 16

    def chunk(c):
        x = p[:, c * LANES:(c + 1) * LANES]
        return _rope(x, cos, sin, first_half) if rope else x

    o0 = A_WIDTH // LANES
    o1 = o0 + A_KV_WIDTH // LANES
    o2 = o1 + B_WIDTH // LANES
    o3 = o2 + B_WIDTH // LANES
    for c in range(o0):
        qa_ref[:, c * LANES:(c + 1) * LANES] = (chunk(c) * Q_SCALE_LOG2).astype(jnp.bfloat16)
    for c in range(o0, o1):
        ka_ref[:, (c - o0) * LANES:(c - o0 + 1) * LANES] = chunk(c).astype(jnp.bfloat16)
    for c in range(o1, o2):
        qb_ref[:, (c - o1) * LANES:(c - o1 + 1) * LANES] = (chunk(c) * Q_SCALE_LOG2).astype(jnp.bfloat16)
    for c in range(o2, o3):
        kb_ref[:, (c - o2) * LANES:(c - o2 + 1) * LANES] = chunk(c).astype(jnp.bfloat16)
    gate_ref[...] = p[:, o3 * LANES:].astype(jnp.bfloat16)


def _attn_in_proj(x, g, sc, sh, w, wvt, cos, sin, *, rope):
    n, d = x.shape
    tm = min(ROW_TILE, n)
    nw = w.shape[1]
    nv = wvt.shape[0]
    row = lambda i: (i, 0)
    fixed = lambda i: (0, 0)
    bf = jnp.bfloat16
    return pl.pallas_call(
        functools.partial(_attn_in_kernel, rope=rope),
        out_shape=[
            jax.ShapeDtypeStruct((n, A_WIDTH), bf),
            jax.ShapeDtypeStruct((n, A_KV_WIDTH), bf),
            jax.ShapeDtypeStruct((n, B_WIDTH), bf),
            jax.ShapeDtypeStruct((n, B_WIDTH), bf),
            jax.ShapeDtypeStruct((nv, n), bf),
            jax.ShapeDtypeStruct((n, ATTN_WIDTH), bf),
        ],
        grid=(n // tm,),
        in_specs=[
            pl.BlockSpec((tm, d), row),
            pl.BlockSpec((1, d), fixed),
            pl.BlockSpec((1, d), fixed),
            pl.BlockSpec((1, d), fixed),
            pl.BlockSpec((d, nw), fixed),
            pl.BlockSpec((nv, d), fixed),
            pl.BlockSpec((tm, LANES), row),
            pl.BlockSpec((tm, LANES), row),
        ],
        out_specs=[
            pl.BlockSpec((tm, A_WIDTH), row),
            pl.BlockSpec((tm, A_KV_WIDTH), row),
            pl.BlockSpec((tm, B_WIDTH), row),
            pl.BlockSpec((tm, B_WIDTH), row),
            pl.BlockSpec((nv, tm), lambda i: (0, i)),
            pl.BlockSpec((tm, ATTN_WIDTH), row),
        ],
        compiler_params=_params("parallel"),
        name="attn_in_proj",
    )(x, g, sc, sh, w, wvt, cos, sin)


def _gqa_scores(q, keys):
    tq = q.shape[0]
    lane = lax.broadcasted_iota(jnp.int32, (tq, LANES), 1)
    zero = jnp.zeros((tq, LANES), q.dtype)
    out = []
    for g in range(A_KV_HEADS):
        in_g = (lane >= g * HEAD_DIM) & (lane < (g + 1) * HEAD_DIM)
        qg = jnp.concatenate([jnp.where(in_g, q[:, j * LANES:(j + 1) * LANES], zero)
                              for j in range(A_GROUP)], axis=0)
        out.append(lax.dot_general(keys, qg, _NT, preferred_element_type=jnp.float32))
    return out


def _gqa_outputs(scores, vt, bias, sink_ref, tq):
    outs = [None] * A_Q_HEADS
    for g in range(A_KV_HEADS):
        s = scores[g]
        if bias is not None:
            s = jnp.concatenate([s[:3 * BLOCK] + bias, s[3 * BLOCK:]], axis=0)
        sk = sink_ref[g:g + 1, :]
        m = jnp.maximum(jnp.max(s, axis=0, keepdims=True), sk)
        acc = jnp.dot(vt, jnp.exp2((s - m).astype(jnp.bfloat16)),
                      preferred_element_type=jnp.float32)
        l = acc[A_KV_WIDTH:A_KV_WIDTH + 1] + jnp.exp2(sk - m)
        o = acc[g * HEAD_DIM:(g + 1) * HEAD_DIM] / l
        for j in range(A_GROUP):
            outs[2 * j + g] = o[:, j * tq:(j + 1) * tq]
    return jnp.concatenate(outs, axis=0).T


def _gqa_window_kernel(sink_ref, bias_ref, q_ref, *refs, nb):
    nk = GQA_BLOCKS + 2
    k_refs, vt_refs = refs[:nk], refs[nk:2 * nk]
    kx_ref, vxt_ref, o_ref = refs[2 * nk:]
    i = pl.program_id(0)
    ones = jnp.ones((SUM_ROWS, 3 * BLOCK + kx_ref.shape[0]), jnp.bfloat16)
    row = lax.broadcasted_iota(jnp.int32, (3 * BLOCK, 1), 0)
    all_scores, vts, biases = [], [], []
    for t in range(GQA_BLOCKS):
        keys = jnp.concatenate([r[...] for r in k_refs[t:t + 3]] + [kx_ref[...]], axis=0)
        vt = jnp.concatenate([r[...] for r in vt_refs[t:t + 3]] + [vxt_ref[...]], axis=1)
        vts.append(jnp.concatenate([vt, ones], axis=0))
        all_scores.append(_gqa_scores(q_ref[t * BLOCK:(t + 1) * BLOCK, :], keys))
        b = GQA_BLOCKS * i + t
        off_end = ((row < BLOCK) & (b == 0)) | ((row >= 2 * BLOCK) & (b == nb - 1))
        bias = bias_ref[...] + jnp.where(off_end, NEG_INF, 0.0)
        biases.append(jnp.concatenate([bias] * A_GROUP, axis=1))
    for t in range(GQA_BLOCKS):
        o = _gqa_outputs(all_scores[t], vts[t], biases[t], sink_ref, BLOCK)
        o_ref[t * BLOCK:(t + 1) * BLOCK, :] = o.astype(o_ref.dtype)


def _gqa_ctx_kernel(sink_ref, q_ref, kx_ref, vxt_ref, o_ref):
    vt = vxt_ref[...]
    vt = jnp.concatenate([vt, jnp.ones((SUM_ROWS, vt.shape[1]), vt.dtype)], axis=0)
    q = q_ref[...]
    o = _gqa_outputs(_gqa_scores(q, kx_ref[...]), vt, None, sink_ref, q.shape[0])
    o_ref[...] = o.astype(o_ref.dtype)


def _gqa_attention(sink_cols, bias, q, k, vt, kx, vxt, *, has_window):
    n = q.shape[0]
    nx = kx.shape[0]
    fixed = lambda i: (0, 0)
    if has_window:
        nb = n // BLOCK
        tq = GQA_BLOCKS * BLOCK
        steps = nb // GQA_BLOCKS
        assert steps * GQA_BLOCKS == nb
        offs = range(-1, GQA_BLOCKS + 1)
        blk = lambda i, o: jnp.clip(GQA_BLOCKS * i + o, 0, nb - 1)
        k_specs = [pl.BlockSpec((BLOCK, A_KV_WIDTH), lambda i, o=o: (blk(i, o), 0)) for o in offs]
        vt_specs = [pl.BlockSpec((A_KV_WIDTH, BLOCK), lambda i, o=o: (0, blk(i, o))) for o in offs]
        in_specs = ([pl.BlockSpec((A_KV_HEADS, A_GROUP * BLOCK), fixed),
                     pl.BlockSpec((3 * BLOCK, BLOCK), fixed),
                     pl.BlockSpec((tq, A_WIDTH), lambda i: (i, 0))] + k_specs + vt_specs +
                    [pl.BlockSpec((nx, A_KV_WIDTH), fixed), pl.BlockSpec((A_KV_WIDTH, nx), fixed)])
        args = (sink_cols, bias, q) + (k,) * len(offs) + (vt,) * len(offs) + (kx, vxt)
        body = functools.partial(_gqa_window_kernel, nb=nb)
    else:
        steps = 1
        tq = n
        in_specs = [pl.BlockSpec((A_KV_HEADS, A_GROUP * tq), fixed), pl.BlockSpec((tq, A_WIDTH), fixed),
                    pl.BlockSpec((nx, A_KV_WIDTH), fixed), pl.BlockSpec((A_KV_WIDTH, nx), fixed)]
        args = (sink_cols, q, kx, vxt)
        body = _gqa_ctx_kernel
    return pl.pallas_call(
        body,
        out_shape=jax.ShapeDtypeStruct((n, A_WIDTH), jnp.bfloat16),
        grid=(steps,),
        in_specs=in_specs,
        out_specs=pl.BlockSpec((tq, A_WIDTH), lambda i: (i, 0)),
        compiler_params=_params("parallel"),
        name="gqa_window" if has_window else "gqa_ctx",
    )(*args)


def _diff_masked_q(q_ref):
    q = q_ref[...]
    lane = lax.broadcasted_iota(jnp.int32, q.shape, 1)
    zero = jnp.zeros_like(q)
    return jnp.concatenate([jnp.where(lane < HEAD_DIM, q, zero),
                            jnp.where(lane >= HEAD_DIM, q, zero)], axis=0)


def _diff_probs(s_rel):
    return jnp.exp2(s_rel.astype(jnp.bfloat16))


def _diff_pv(vt, p):
    ones = jnp.ones((SUM_ROWS, vt.shape[1]), vt.dtype)
    return jnp.dot(jnp.concatenate([vt, ones], axis=0), p, preferred_element_type=jnp.float32)


def _diff_ctx_step(qz, kx, vxt):
    s = lax.dot_general(kx, qz, _NT, preferred_element_type=jnp.float32)
    m = jnp.max(s, axis=0, keepdims=True)
    return m, _diff_pv(vxt, _diff_probs(s - m))


def _diff_finalize(lam_ref, g_ref, acc, lam_init):
    hw = 2 * HEAD_DIM
    tq = acc.shape[1] // 2
    lv = lam_ref[...]
    lam = (jnp.exp(jnp.sum(lv[0:1] * lv[1:2], keepdims=True))
           - jnp.exp(jnp.sum(lv[2:3] * lv[3:4], keepdims=True)) + lam_init)
    on = acc[:hw] / acc[hw:hw + 1]
    o = on[:, :tq] - lam * on[:, tq:]
    ms = jnp.mean(o * o, axis=0, keepdims=True)
    o = (o * lax.rsqrt(ms + EPS)) * g_ref[...] * (1.0 - lam_init)
    return o.T.astype(jnp.bfloat16)


def _diff_ctx_kernel(lam_ref, g_ref, q_ref, kx_ref, vxt_ref, o_ref, *, lam_init):
    _, acc = _diff_ctx_step(_diff_masked_q(q_ref), kx_ref[...], vxt_ref[...])
    o_ref[...] = _diff_finalize(lam_ref, g_ref, acc, lam_init)


def _bf16_round(x):
    return x.astype(jnp.bfloat16).astype(jnp.float32)


def _diff_kernel(lam_ref, g_ref, q_ref, k_ref, vt_ref, kx_ref, vxt_ref, o_ref,
                 p0, p1, p2, p3, acc_ref, *, lam_init, n_chunks):
    tk = DIFF_TK
    qz = _diff_masked_q(q_ref)
    cols = qz.shape[0]
    qzt = qz.astype(jnp.float32).T.astype(jnp.bfloat16)
    ones_k = jnp.ones((tk, LANES), jnp.bfloat16)
    first_row = lax.broadcasted_iota(jnp.int32, (2 * SUBLANES, cols), 0) == 0
    zero_rows = jnp.zeros((LANES - 2 * SUBLANES, cols), jnp.bfloat16)
    p_buf = (p0, p1, p2, p3)

    def rhs_for(r):
        neg_r = jnp.where(first_row, -r, 0.0).astype(jnp.bfloat16)
        return jnp.concatenate([qzt, neg_r, zero_rows], axis=0)

    def rel_scores(c, rhs):
        k = k_ref[pl.ds(pl.multiple_of(c * tk, tk), tk), :]
        return jnp.dot(jnp.concatenate([k, ones_k], axis=1), rhs, preferred_element_type=jnp.float32)

    def pv(c, p_ref):
        return _diff_pv(vt_ref[:, pl.ds(pl.multiple_of(c * tk, tk), tk)], p_ref[...])

    def col_max(sa, sb):
        return jnp.maximum(jnp.max(sa, axis=0, keepdims=True), jnp.max(sb, axis=0, keepdims=True))

    s_ctx = lax.dot_general(kx_ref[...], qz, _NT, preferred_element_type=jnp.float32)
    m_ctx = jnp.max(s_ctx, axis=0, keepdims=True)
    r0 = _bf16_round(m_ctx)
    rhs0 = rhs_for(r0)
    sa, sb = rel_scores(0, rhs0), rel_scores(1, rhs0)
    acc_ref[...] = _diff_pv(vxt_ref[...], _diff_probs(s_ctx - r0))
    worst = col_max(sa, sb)
    p_buf[0][...] = _diff_probs(sa)
    p_buf[1][...] = _diff_probs(sb)

    def body(j, carry):
        run_max, pend, r_prev, r_cur, worst = carry
        for i in range(0, DIFF_STAGES, 2):
            c = DIFF_STAGES * j + i
            r_next = _bf16_round(run_max)
            rhs = rhs_for(r_next)
            pva = pv(c, p_buf[i % 4])
            sa = rel_scores(jnp.minimum(c + 2, n_chunks - 1), rhs)
            pvb = pv(c + 1, p_buf[(i + 1) % 4])
            sb = rel_scores(jnp.minimum(c + 3, n_chunks - 1), rhs)
            acc_ref[...] = jnp.exp2(r_prev - r_cur) * acc_ref[...] + pva + pvb
            top = col_max(sa, sb)
            p_buf[(i + 2) % 4][...] = _diff_probs(sa)
            p_buf[(i + 3) % 4][...] = _diff_probs(sb)
            worst = jnp.maximum(worst, top)
            run_max = jnp.maximum(run_max, pend)
            pend = top + r_next
            r_prev, r_cur = r_cur, r_next
        return run_max, pend, r_prev, r_cur, worst

    carry = lax.fori_loop(0, n_chunks // DIFF_STAGES, body, (m_ctx, worst + r0, r0, r0, worst))
    o_ref[...] = _diff_finalize(lam_ref, g_ref, acc_ref[...], lam_init)

    @pl.when(jnp.max(carry[4]) > DIFF_JUMP_LIMIT)
    def _():
        m0, acc0 = _diff_ctx_step(qz, kx_ref[...], vxt_ref[...])

        def exact(c, carry):
            m, acc = carry
            off = pl.multiple_of(c * tk, tk)
            s = lax.dot_general(k_ref[pl.ds(off, tk), :], qz, _NT, preferred_element_type=jnp.float32)
            m_new = jnp.maximum(m, jnp.max(s, axis=0, keepdims=True))
            acc = jnp.exp2(m - m_new) * acc + _diff_pv(vt_ref[:, pl.ds(off, tk)], _diff_probs(s - m_new))
            return m_new, acc

        _, acc = lax.fori_loop(0, n_chunks, exact, (m0, acc0))
        o_ref[...] = _diff_finalize(lam_ref, g_ref, acc, lam_init)


def _diff_attention(lamv, subln_col, q, k, vt, kx, vxt, *, has_latent, lam_init):
    n = q.shape[0]
    nx = kx.shape[0]
    tq = min(DIFF_TQ, n)
    hw = 2 * HEAD_DIM
    fixed = lambda h, i: (0, 0)
    in_specs = [pl.BlockSpec((4, HEAD_DIM), fixed), pl.BlockSpec((hw, 1), fixed),
                pl.BlockSpec((tq, hw), lambda h, i: (i, h))]
    args = [lamv, subln_col, q]
    scratch = []
    if has_latent:
        nk = k.shape[0]
        n_chunks = nk // DIFF_TK
        assert DIFF_STAGES % 4 == 0 and n_chunks % DIFF_STAGES == 0 and n_chunks * DIFF_TK == nk
        in_specs += [pl.BlockSpec((nk, hw), lambda h, i: (0, h)),
                     pl.BlockSpec((hw, nk), lambda h, i: (h + 1, 0))]
        args += [k, vt]
        body = functools.partial(_diff_kernel, lam_init=lam_init, n_chunks=n_chunks)
        scratch = ([pltpu.VMEM((DIFF_TK, 2 * tq), jnp.bfloat16)] * 4 +
                   [pltpu.VMEM((hw + SUM_ROWS, 2 * tq), jnp.float32)])
    else:
        body = functools.partial(_diff_ctx_kernel, lam_init=lam_init)
    in_specs += [pl.BlockSpec((nx, hw), lambda h, i: (0, h)),
                 pl.BlockSpec((hw, nx), lambda h, i: (h + 1, 0))]
    args += [kx, vxt]
    return pl.pallas_call(
        body,
        out_shape=jax.ShapeDtypeStruct((n, B_WIDTH), jnp.bfloat16),
        grid=(B_HEADS, n // tq),
        in_specs=in_specs,
        out_specs=pl.BlockSpec((tq, hw), lambda h, i: (i, h)),
        scratch_shapes=scratch,
        compiler_params=_params("parallel", "parallel"),
        name="diff_attn" if has_latent else "diff_attn_ctx",
    )(*args)


def _attn_out_rec_in_kernel(oa_ref, ob_ref, gate_ref, x_ref, gt_ref, w_ref,
                            g_ref, sc_ref, sh_ref, wr_ref, xl_ref, xr_ref, gr_ref):
    o = jnp.concatenate([oa_ref[...], ob_ref[...]], axis=-1).astype(jnp.float32)
    a = o * _silu(gate_ref[...].astype(jnp.float32))
    y = jnp.dot(a.astype(jnp.bfloat16), w_ref[...], preferred_element_type=jnp.float32)
    xl = x_ref[...] + gt_ref[...] * y
    xl_ref[...] = xl
    h = _norm_mod(xl, g_ref[...], sc_ref[...], sh_ref[...]).astype(jnp.bfloat16)
    p = jnp.dot(h, wr_ref[...], preferred_element_type=jnp.float32)
    xr_ref[...] = p[:, :RNN_WIDTH]
    gr_ref[...] = p[:, RNN_WIDTH:]


def _attn_out_rec_in(oa, ob, gate, x, gt, w, g, sc, sh, wr):
    n, d = x.shape
    tm = min(ROW_TILE, n)
    row = lambda i: (i, 0)
    fixed = lambda i: (0, 0)
    vec = pl.BlockSpec((1, d), fixed)
    return pl.pallas_call(
        _attn_out_rec_in_kernel,
        out_shape=[jax.ShapeDtypeStruct((n, d), jnp.float32),
                   jax.ShapeDtypeStruct((n, RNN_WIDTH), jnp.float32),
                   jax.ShapeDtypeStruct((n, RNN_WIDTH), jnp.float32)],
        grid=(n // tm,),
        in_specs=[
            pl.BlockSpec((tm, A_WIDTH), row),
            pl.BlockSpec((tm, B_WIDTH), row),
            pl.BlockSpec((tm, ATTN_WIDTH), row),
            pl.BlockSpec((tm, d), row),
            vec,
            pl.BlockSpec((ATTN_WIDTH, d), fixed),
            vec, vec, vec,
            pl.BlockSpec((d, 2 * RNN_WIDTH), fixed),
        ],
        out_specs=[pl.BlockSpec((tm, d), row), pl.BlockSpec((tm, RNN_WIDTH), row),
                   pl.BlockSpec((tm, RNN_WIDTH), row)],
        compiler_params=_params("parallel"),
        name="attn_out_rec_in",
    )(oa, ob, gate, x, gt, w, g, sc, sh, wr)


def _rec_scan_kernel(*refs, reverse, fuse_out, nt):
    if fuse_out:
        (x_ref, xp_ref, xn_ref, cw_ref, cb_ref, w_ref, bias_ref, lam_ref, h0_ref,
         yr_ref, gr_ref, xl_ref, gt_ref, wo_ref, fg_ref,
         o_ref, xe_ref, a_ref, b_ref, y_ref, h_ref) = refs
    else:
        (x_ref, xp_ref, xn_ref, cw_ref, cb_ref, w_ref, bias_ref, lam_ref, h0_ref,
         y_ref, hf_ref, xe_ref, a_ref, b_ref, h_ref) = refs
    i = pl.program_id(0)
    pos = i if not reverse else nt - 1 - i
    tm = x_ref.shape[0]

    @pl.when(i == 0)
    def _():
        h_ref[...] = h0_ref[...]

    xe_ref[0:SUBLANES, :] = jnp.where(pos > 0, xp_ref[...], 0.0)
    xe_ref[SUBLANES:SUBLANES + tm, :] = x_ref[...]
    xe_ref[SUBLANES + tm:, :] = jnp.where(pos < nt - 1, xn_ref[...], 0.0)
    cw = cw_ref[...]
    xe = xe_ref[...]
    taps = pltpu.roll(xe * cw[0:1], 1, 0)
    taps = pltpu.roll(taps + xe * cw[1:2], 1, 0)
    taps = taps + xe * cw[2:3] + pltpu.roll(xe * cw[3:4], tm + 2 * SUBLANES - 1, 0)
    uh = taps[SUBLANES:SUBLANES + tm] + cb_ref[...]
    ub = uh.astype(jnp.bfloat16)
    half = RNN_WIDTH // 2
    z0 = jnp.dot(ub[:, :half], w_ref[0], preferred_element_type=jnp.float32)
    z1 = jnp.dot(ub[:, half:], w_ref[1], preferred_element_type=jnp.float32)
    bias = bias_ref[...]
    ta = jnp.tanh(jnp.concatenate([z0[:, :half], z1[:, :half]], axis=1) + bias[:, :RNN_WIDTH])
    tx = jnp.tanh(jnp.concatenate([z0[:, half:], z1[:, half:]], axis=1) + bias[:, RNN_WIDTH:])
    nl = -lam_ref[...]
    softplus = jnp.maximum(nl, 0.0) + jnp.log1p(jnp.exp(-jnp.abs(nl)))
    coef = (-0.5 * RG_C * LOG2E) * softplus
    a = jnp.exp2(coef + coef * ta)
    a_ref[...] = a
    om = 1.0 - a * a
    b_ref[...] = (om * lax.rsqrt(jnp.maximum(om, 1e-30))) * ((1.0 + tx) * uh)

    def row(s, h):
        rr = (tm - 1 - s) if reverse else s
        h = a_ref[pl.ds(rr, 1), :] * h + b_ref[pl.ds(rr, 1), :]
        y_ref[pl.ds(rr, 1), :] = h
        return h

    h = lax.fori_loop(0, tm, row, h_ref[...], unroll=8 if fuse_out else 32)
    h_ref[...] = h

    if fuse_out:
        y = (y_ref[...] + yr_ref[...]) * _silu(gr_ref[...])
        out = jnp.dot(y.astype(jnp.bfloat16), wo_ref[...], preferred_element_type=jnp.float32)
        xl = xl_ref[...] + gt_ref[...] * out
        ms = jnp.mean(xl * xl, axis=-1, keepdims=True)
        o_ref[...] = xl * lax.rsqrt(ms + EPS) * fg_ref[...]
    else:
        hf_ref[...] = h


def _rec_scan(xr, cw, cb, w, bias, lam, h0, *, reverse, fused=None):
    assert CONV_WIDTH == 4
    n = xr.shape[0]
    tm = min(SCAN_TILE, n)
    nt = n // tm
    per = tm // SUBLANES
    nblk8 = n // SUBLANES
    pos = (lambda i: nt - 1 - i) if reverse else (lambda i: i)
    row = lambda i: (pos(i), 0)
    prev = lambda i: (jnp.maximum(pos(i) * per - 1, 0), 0)
    nxt = lambda i: (jnp.minimum((pos(i) + 1) * per, nblk8 - 1), 0)
    fixed = lambda i: (0, 0)
    vec = pl.BlockSpec((1, RNN_WIDTH), fixed)
    in_specs = [
        pl.BlockSpec((tm, RNN_WIDTH), row),
        pl.BlockSpec((SUBLANES, RNN_WIDTH), prev),
        pl.BlockSpec((SUBLANES, RNN_WIDTH), nxt),
        pl.BlockSpec((CONV_WIDTH, RNN_WIDTH), fixed),
        vec,
        pl.BlockSpec((2, RNN_WIDTH // 2, RNN_WIDTH), lambda i: (0, 0, 0)),
        pl.BlockSpec((1, 2 * RNN_WIDTH), fixed),
        vec,
        vec,
    ]
    args = [xr, xr, xr, cw, cb, w, bias, lam, h0]
    scratch = [pltpu.VMEM((tm + 2 * SUBLANES, RNN_WIDTH), jnp.float32),
               pltpu.VMEM((tm, RNN_WIDTH), jnp.float32),
               pltpu.VMEM((tm, RNN_WIDTH), jnp.float32)]
    if fused is not None:
        yr, gr, xl, gt, wo, fg = fused
        d = xl.shape[1]
        in_specs += [
            pl.BlockSpec((tm, RNN_WIDTH), row),
            pl.BlockSpec((tm, RNN_WIDTH), row),
            pl.BlockSpec((tm, d), row),
            pl.BlockSpec((1, d), fixed),
            pl.BlockSpec((RNN_WIDTH, d), fixed),
            pl.BlockSpec((1, d), fixed),
        ]
        args += [yr, gr, xl, gt, wo, fg]
        out_shape = jax.ShapeDtypeStruct((n, d), jnp.float32)
        out_specs = pl.BlockSpec((tm, d), row)
        scratch = scratch + [pltpu.VMEM((tm, RNN_WIDTH), jnp.float32)]
    else:
        out_shape = [jax.ShapeDtypeStruct((n, RNN_WIDTH), jnp.float32),
                     jax.ShapeDtypeStruct((1, RNN_WIDTH), jnp.float32)]
        out_specs = [pl.BlockSpec((tm, RNN_WIDTH), row), vec]
    scratch = scratch + [pltpu.VMEM((1, RNN_WIDTH), jnp.float32)]
    return pl.pallas_call(
        functools.partial(_rec_scan_kernel, reverse=reverse, fuse_out=fused is not None, nt=nt),
        out_shape=out_shape,
        grid=(nt,),
        in_specs=in_specs,
        out_specs=out_specs,
        scratch_shapes=scratch,
        compiler_params=_params("arbitrary"),
        name="rec_scan_out" if fused is not None else ("rec_scan_rev" if reverse else "rec_scan_fwd"),
    )(*args)


def _rope_tables(n):
    rows = n // GRID_W
    n_freq = HEAD_DIM // 4
    inv_freq = ROPE_BASE ** (-jnp.arange(n_freq, dtype=jnp.float32) / n_freq)
    ar = jnp.arange(rows, dtype=jnp.float32)[:, None] * inv_freq
    ac = jnp.arange(GRID_W, dtype=jnp.float32)[:, None] * inv_freq

    def table(fn, sign):
        r = jnp.broadcast_to(fn(ar)[:, None, :], (rows, GRID_W, n_freq)).reshape(n, n_freq)
        c = jnp.broadcast_to(fn(ac)[None, :, :], (rows, GRID_W, n_freq)).reshape(n, n_freq)
        head = jnp.concatenate([sign * r, r, sign * c, c], axis=-1)
        return jnp.tile(head, (1, 2))

    return table(jnp.cos, 1.0), table(jnp.sin, -1.0)


def _block_diag(w):
    nb, bi, bj = w.shape
    on_diag = jnp.eye(nb, dtype=bool)[:, None, :, None]
    return jnp.where(on_diag, w[:, :, None, :], 0).reshape(nb * bi, nb * bj)


def kernel(x, c, ctx, c_ctx, norm_g, ada_w, ada_b, attn_w_in, attn_w_out, attn_sink, lam_q1, lam_k1,
           lam_q2, lam_k2, subln_g, rec_w_in, rec_conv_w, rec_conv_b, rec_wa, rec_ba, rec_wx, rec_bx,
           rec_lam, rec_w_out, final_g):
    assert x.shape[0] == 1 and c.shape[0] == 1 and ctx.shape[0] == 1
    bf = jnp.bfloat16
    d = D_MODEL
    xl = x[0]
    xc = ctx[0]
    n = xl.shape[0]

    cv = jnp.zeros((SUBLANES, d), jnp.float32).at[0].set(c[0]).at[1].set(c_ctx)
    mods = _adaln(cv, ada_w, ada_b)

    def mod(l, r):
        m = mods[l, r]
        return m[None, :d], m[None, d:2 * d], m[None, 2 * d:]

    lam_init = 0.8 - 0.6 * math.exp(-0.3 * 0)
    s1 = A_WIDTH
    s2 = s1 + A_KV_WIDTH
    s3 = s2 + A_KV_WIDTH
    s5 = s3 + 2 * B_WIDTH
    s6 = s5 + B_WIDTH
    head_perm = jnp.array([g * A_GROUP + j for j in range(A_GROUP) for g in range(A_KV_HEADS)])
    col_perm = (head_perm[:, None] * HEAD_DIM + jnp.arange(HEAD_DIM)).reshape(-1)
    w_in = attn_w_in[0]
    w_main = jnp.concatenate([w_in[:, :s1][:, col_perm], w_in[:, s1:s2], w_in[:, s3:s5],
                              w_in[:, s6:s6 + A_WIDTH][:, col_perm], w_in[:, s6 + A_WIDTH:]],
                             axis=1).astype(bf)
    w_vt = jnp.concatenate([w_in[:, s2:s3], w_in[:, s5:s6]], axis=1).T.astype(bf)
    w_out = attn_w_out[0]
    w_out = jnp.concatenate([w_out[:A_WIDTH][col_perm], w_out[A_WIDTH:]], axis=0).astype(bf)
    g0 = norm_g[0][None]
    cos, sin = _rope_tables(n)
    sh, sc, gt = mod(0, 0)
    shc, scc, gtc = mod(0, 1)
    qa, ka, qb, kb, vt, gate = _attn_in_proj(xl, g0, sc, sh, w_main, w_vt, cos, sin, rope=True)
    nc = xc.shape[0]
    qac, kac, qbc, kbc, vtc, gatec = _attn_in_proj(
        xc, g0, scc, shc, w_main, w_vt, cos[:nc], sin[:nc], rope=False)

    sink_gj = (attn_sink[0] * LOG2E).reshape(A_KV_HEADS, A_GROUP)
    kj = jnp.arange(3 * BLOCK)[:, None]
    qi = jnp.arange(BLOCK)[None, :]
    band_bias = jnp.where(jnp.abs(kj - BLOCK - qi) <= WINDOW, 0.0, NEG_INF).astype(jnp.float32)
    lamv = jnp.stack([lam_q1[0], lam_k1[0], lam_q2[0], lam_k2[0]])
    subln_col = subln_g[0][:, None]
    oa = _gqa_attention(jnp.repeat(sink_gj, BLOCK, axis=1), band_bias, qa, ka, vt, kac, vtc,
                        has_window=True)
    ob = _diff_attention(lamv, subln_col, qb, kb, vt, kbc, vtc, has_latent=True, lam_init=lam_init)
    oac = _gqa_attention(jnp.repeat(sink_gj, nc, axis=1), None, qac, None, None, kac, vtc,
                         has_window=False)
    obc = _diff_attention(lamv, subln_col, qbc, None, None, kbc, vtc, has_latent=False,
                          lam_init=lam_init)

    g1 = norm_g[1][None]
    sh, sc, gt1 = mod(1, 0)
    shc, scc, _ = mod(1, 1)
    w_rin = rec_w_in[0].astype(bf)
    xl, xr, gr = _attn_out_rec_in(oa, ob, gate, xl, gt, w_out, g1, sc, sh, w_rin)
    _, xrc, _ = _attn_out_rec_in(oac, obc, gatec, xc, gtc, w_out, g1, scc, shc, w_rin)
    gt = gt1
    cw = 0.5 * rec_conv_w[0]
    cb = 0.5 * rec_conv_b[0][None]
    zero_h = jnp.zeros((1, RNN_WIDTH), jnp.float32)

    def coeff(dr):
        hb = RNN_BLOCKS // 2
        wa, wx = rec_wa[0, dr].astype(bf), rec_wx[0, dr].astype(bf)
        w = jnp.stack([jnp.concatenate([_block_diag(wa[i * hb:(i + 1) * hb]),
                                        _block_diag(wx[i * hb:(i + 1) * hb])], axis=1)
                       for i in range(2)])
        bias = 0.5 * jnp.concatenate([rec_ba[0, dr], rec_bx[0, dr]])[None]
        return w, bias, rec_lam[0, dr][None]

    wf, bias_f, lam_f = coeff(0)
    wr, bias_r, lam_r = coeff(1)
    _, h0f = _rec_scan(xrc, cw, cb, wf, bias_f, lam_f, zero_h, reverse=False)
    _, h0r = _rec_scan(xrc, cw, cb, wr, bias_r, lam_r, zero_h, reverse=True)
    yr, _ = _rec_scan(xr, cw, cb, wr, bias_r, lam_r, h0r, reverse=True)
    out = _rec_scan(xr, cw, cb, wf, bias_f, lam_f, h0f, reverse=False,
                    fused=(yr, gr, xl, gt, rec_w_out[0].astype(bf), final_g[None]))
    return out[None]
```
